```python
import math
import jax, jax.numpy as jnp
from jax import lax
import numpy as np

D_MODEL = 1024
BATCH = 16
SEQ = 2048
DEPTH = 4

N_MIXERS = 2
N_A = (DEPTH + N_MIXERS - 1) // N_MIXERS
N_B = DEPTH // N_MIXERS

RWKV_HEAD = 64
RWKV_HEADS = D_MODEL // RWKV_HEAD
D_DECAY_LORA = max(32, int(round(1.8 * D_MODEL ** 0.5 / 32)) * 32)
D_AAA_LORA = max(32, int(round(1.8 * D_MODEL ** 0.5 / 32)) * 32)
D_MV_LORA = max(32, int(round(1.3 * D_MODEL ** 0.5 / 32)) * 32)
D_GATE_LORA = max(32, int(round(0.6 * D_MODEL ** 0.8 / 32)) * 32)
GN_EPS = RWKV_HEAD * 1e-5

ATT_HEADS = 16
ATT_HEAD_DIM = D_MODEL // ATT_HEADS
KV_LATENT = 128
IDX_HEADS = 8
IDX_DIM = 64
TOPK_MAX = 256
TOPK_DIV = 4
Q_BLOCK = 128
DSA_PROJ = ATT_HEADS * ATT_HEAD_DIM + KV_LATENT + IDX_HEADS * IDX_DIM + IDX_DIM + IDX_HEADS

REL_BUCKETS = 32
REL_MAX_DIST = 128

FFN_HIDDEN = -(-8 * D_MODEL // (3 * 256)) * 256

DEEPNORM_ALPHA = (2 * DEPTH) ** 0.25
DEEPNORM_BETA = (8 * DEPTH) ** -0.25
LN_EPS = 1e-5

kernel_name = 'hybrid_rwkv7_dsa_deepnorm_adaln'


def _layernorm(x, g, b):
    xf = x.astype(jnp.float32)
    mu = xf.mean(-1, keepdims=True)
    var = jnp.square(xf - mu).mean(-1, keepdims=True)
    y = (xf - mu) * lax.rsqrt(var + LN_EPS) * g.astype(jnp.float32) + b.astype(jnp.float32)
    return y.astype(x.dtype)


def _rmsnorm(x, g):
    xf = x.astype(jnp.float32)
    y = xf * lax.rsqrt(jnp.mean(xf * xf, -1, keepdims=True) + 1e-6) * g.astype(jnp.float32)
    return y.astype(x.dtype)


def _t5_bucket(n):
    n = jnp.maximum(n, 0)
    max_exact = REL_BUCKETS // 2
    nf = jnp.maximum(n, 1).astype(jnp.float32)
    large = max_exact + (jnp.log(nf / max_exact) / math.log(REL_MAX_DIST / max_exact)
                         * (REL_BUCKETS - max_exact)).astype(jnp.int32)
    large = jnp.minimum(large, REL_BUCKETS - 1)
    return jnp.where(n < max_exact, n, large)


def _rwkv7_step(S, inp):
    r_t, w_t, k_t, v_t, a_t, b_t = inp
    sa = jnp.einsum('bhij,bhj->bhi', S, a_t)
    S = S * w_t[:, :, None, :] + sa[..., None] * b_t[:, :, None, :] + v_t[..., None] * k_t[:, :, None, :]
    y = jnp.einsum('bhij,bhj->bhi', S, r_t)
    return S, y


def _rwkv7_time_mix(h, v_first, mu, w_rkv, w0, w1, w2, a0, a1, a2, vres,
                    g1, g2, k_k, k_a, r_k, lnx_g, lnx_b, w_out):
    B, T, D = h.shape
    H, N = RWKV_HEADS, RWKV_HEAD
    f32 = jnp.float32
    h_prev = jnp.pad(h, ((0, 0), (1, 0), (0, 0)))[:, :-1]
    xx = h_prev - h
    xs = h[None] + xx[None] * mu[:, None, None, :]
    r, k, v = jnp.einsum('pbtd,pde->pbte', xs[:3], w_rkv)
    xv, xw, xa, xg = xs[2], xs[3], xs[4], xs[5]
    w = -jax.nn.softplus(-(w0 + jnp.tanh(xw @ w1) @ w2)) - 0.5
    if vres is None:
        v_first = v
    else:
        v0, v1, v2 = vres
        v = v + (v_first - v) * jax.nn.sigmoid(v0 + (xv @ v1) @ v2)
    a = jax.nn.sigmoid(a0 + (xa @ a1) @ a2)
    g = jax.nn.sigmoid(xg @ g1) @ g2

    def heads(t):
        return t.reshape(B, T, H, N).astype(f32)

    kk = heads(k * k_k)
    kk = kk / jnp.maximum(jnp.sqrt(jnp.sum(kk * kk, -1, keepdims=True)), 1e-12)
    k = k * (1 + (a - 1) * k_a)
    rh, kh, vh, ah = heads(r), heads(k), heads(v), heads(a)
    decay = jnp.exp(-jnp.exp(heads(w)))

    def tm(t):
        return jnp.swapaxes(t, 0, 1)

    S0 = jnp.zeros((B, H, N, N), f32)
    _, y = lax.scan(_rwkv7_step, S0,
                    (tm(rh), tm(decay), tm(kh), tm(vh), tm(-kk), tm(kk * ah)))
    y = tm(y)
    ym = y.mean(-1, keepdims=True)
    yv = jnp.square(y - ym).mean(-1, keepdims=True)
    yn = ((y - ym) * lax.rsqrt(yv + GN_EPS)).reshape(B, T, D)
    yn = yn * lnx_g.astype(f32) + lnx_b.astype(f32)
    bonus = jnp.sum(rh * kh * r_k.astype(f32), -1, keepdims=True) * vh
    out = (yn + bonus.reshape(B, T, D)).astype(h.dtype) * g
    return out @ w_out, v_first


def _dsa_attention(h, w_in, kv_norm, w_uk, w_uv, w_out, rel_bias):
    B, T, D = h.shape
    H, dh = ATT_HEADS, ATT_HEAD_DIM
    f32 = jnp.float32
    proj = h @ w_in
    c1 = H * dh
    c2 = c1 + KV_LATENT
    c3 = c2 + IDX_HEADS * IDX_DIM
    c4 = c3 + IDX_DIM
    q = proj[..., :c1].reshape(B, T, H, dh)
    ckv = _rmsnorm(proj[..., c1:c2], kv_norm)
    q_idx = proj[..., c2:c3].reshape(B, T, IDX_HEADS, IDX_DIM)
    k_idx = proj[..., c3:c4]
    w_idx = proj[..., c4:] * (IDX_HEADS ** -0.5 * IDX_DIM ** -0.5)

    k_sel = min(TOPK_MAX, T // TOPK_DIV)
    nblk = T // Q_BLOCK

    def to_blocks(t):
        return jnp.swapaxes(t.reshape(B, nblk, Q_BLOCK, *t.shape[2:]), 0, 1)

    pos_blocks = jnp.arange(T, dtype=jnp.int32).reshape(nblk, Q_BLOCK)
    key_pos = jnp.arange(T, dtype=jnp.int32)

    def block(args):
        qb, qib, wib, tpos = args
        s = jnp.einsum('bthd,bsd->bths', qib, k_idx)
        score = jnp.einsum('bth,bths->bts', wib, jax.nn.relu(s))
        causal = key_pos[None, :] <= tpos[:, None]
        score = jnp.where(causal[None], score, -jnp.inf)
        _, idx = lax.top_k(score, k_sel)
        sel = jax.vmap(lambda kv, ix: kv[ix])(ckv, idx)
        q_lat = jnp.einsum('bthd,hdc->bthc', qb, w_uk)
        logits = jnp.einsum('bthc,btkc->bthk', q_lat, sel).astype(f32) * (dh ** -0.5)
        rel = tpos[None, :, None] - idx
        bias = jnp.moveaxis(rel_bias[_t5_bucket(rel)], -1, 2)
        logits = logits + bias.astype(f32)
        logits = jnp.where((rel >= 0)[:, :, None, :], logits, -jnp.inf)
        p = jax.nn.softmax(logits, axis=-1).astype(h.dtype)
        o_lat = jnp.einsum('bthk,btkc->bthc', p, sel)
        o = jnp.einsum('bthc,hcd->bthd', o_lat, w_uv)
        return o.reshape(B, Q_BLOCK, H * dh)

    out = lax.map(block, (to_blocks(q), to_blocks(q_idx), to_blocks(w_idx), pos_blocks))
    out = jnp.swapaxes(out, 0, 1).reshape(B, T, H * dh)
    return out @ w_out


def _swiglu(h, w_in, w_out):
    gate, up = jnp.split(h @ w_in, 2, axis=-1)
    return (jax.nn.silu(gate) * up) @ w_out


def setup_inputs(seed: int = 0) -> dict:
    key = jax.random.key(seed)
    ks = iter(jax.random.split(key, 48))
    f32 = jnp.float32
    D = D_MODEL

    def nrm(shape, scale):
        return jax.random.normal(next(ks), shape, f32) * scale

    inp = {}
    inp['x'] = nrm((BATCH, SEQ, D), 1.0)
    inp['c'] = nrm((BATCH, D), 1.0)
    inp['ada_w'] = nrm((DEPTH, D, 6 * D), 0.5 * D ** -0.5)
    inp['ada_b'] = nrm((DEPTH, 6 * D), 0.02)
    inp['ln_g'] = 1.0 + nrm((DEPTH, 2, D), 0.02)
    inp['ln_b'] = nrm((DEPTH, 2, D), 0.02)
    inp['ffn_w_in'] = nrm((DEPTH, D, 2 * FFN_HIDDEN), D ** -0.5)
    inp['ffn_w_out'] = nrm((DEPTH, FFN_HIDDEN, D), FFN_HIDDEN ** -0.5 * DEEPNORM_BETA)
    inp['rwkv_mu'] = jax.random.uniform(next(ks), (N_A, 6, D), f32)
    inp['rwkv_w_rkv'] = nrm((N_A, 3, D, D), D ** -0.5)
    inp['rwkv_w0'] = jax.random.uniform(next(ks), (N_A, D), f32, -4.0, 1.0)
    inp['rwkv_w1'] = nrm((N_A, D, D_DECAY_LORA), D ** -0.5)
    inp['rwkv_w2'] = nrm((N_A, D_DECAY_LORA, D), 0.1 * D_DECAY_LORA ** -0.5)
    inp['rwkv_a0'] = nrm((N_A, D), 0.5)
    inp['rwkv_a1'] = nrm((N_A, D, D_AAA_LORA), D ** -0.5)
    inp['rwkv_a2'] = nrm((N_A, D_AAA_LORA, D), 0.1 * D_AAA_LORA ** -0.5)
    inp['rwkv_v0'] = nrm((N_A - 1, D), 0.5)
    inp['rwkv_v1'] = nrm((N_A - 1, D, D_MV_LORA), D ** -0.5)
    inp['rwkv_v2'] = nrm((N_A - 1, D_MV_LORA, D), 0.1 * D_MV_LORA ** -0.5)
    inp['rwkv_g1'] = nrm((N_A, D, D_GATE_LORA), D ** -0.5)
    inp['rwkv_g2'] = nrm((N_A, D_GATE_LORA, D), D_GATE_LORA ** -0.5)
    inp['rwkv_k_k'] = 0.85 + nrm((N_A, D), 0.02)
    inp['rwkv_k_a'] = 1.0 + nrm((N_A, D), 0.02)
    inp['rwkv_r_k'] = nrm((N_A, RWKV_HEADS, RWKV_HEAD), 0.1)
    inp['rwkv_lnx_g'] = 1.0 + nrm((N_A, D), 0.02)
    inp['rwkv_lnx_b'] = nrm((N_A, D), 0.02)
    inp['rwkv_w_out'] = nrm((N_A, D, D), D ** -0.5 * DEEPNORM_BETA)
    inp['dsa_w_in'] = nrm((N_B, D, DSA_PROJ), D ** -0.5)
    inp['dsa_kv_norm'] = 1.0 + nrm((N_B, KV_LATENT), 0.02)
    inp['dsa_w_uk'] = nrm((N_B, ATT_HEADS, ATT_HEAD_DIM, KV_LATENT), ATT_HEAD_DIM ** -0.5)
    inp['dsa_w_uv'] = nrm((N_B, ATT_HEADS, KV_LATENT, ATT_HEAD_DIM), KV_LATENT ** -0.5)
    inp['dsa_w_out'] = nrm((N_B, ATT_HEADS * ATT_HEAD_DIM, D),
                           (ATT_HEADS * ATT_HEAD_DIM) ** -0.5 * DEEPNORM_BETA)
    inp['rel_bias'] = nrm((REL_BUCKETS, ATT_HEADS), 0.3)
    return inp


def reference(x, c, ada_w, ada_b, ln_g, ln_b, ffn_w_in, ffn_w_out,
              rwkv_mu, rwkv_w_rkv, rwkv_w0, rwkv_w1, rwkv_w2, rwkv_a0, rwkv_a1, rwkv_a2,
              rwkv_v0, rwkv_v1, rwkv_v2, rwkv_g1, rwkv_g2, rwkv_k_k, rwkv_k_a, rwkv_r_k,
              rwkv_lnx_g, rwkv_lnx_b, rwkv_w_out,
              dsa_w_in, dsa_kv_norm, dsa_w_uk, dsa_w_uv, dsa_w_out, rel_bias):
    cond = jax.nn.silu(c)
    v_first = None
    for i in range(DEPTH):
        mod = cond @ ada_w[i] + ada_b[i]
        sh1, sc1, gt1, sh2, sc2, gt2 = jnp.split(mod[:, None, :], 6, axis=-1)
        hin = x * (1 + sc1) + sh1
        j = i // N_MIXERS
        if i % N_MIXERS == 0:
            vres = None if j == 0 else (rwkv_v0[j - 1], rwkv_v1[j - 1], rwkv_v2[j - 1])
            y, v_first = _rwkv7_time_mix(
                hin, v_first, rwkv_mu[j], rwkv_w_rkv[j], rwkv_w0[j], rwkv_w1[j], rwkv_w2[j],
                rwkv_a0[j], rwkv_a1[j], rwkv_a2[j], vres, rwkv_g1[j], rwkv_g2[j],
                rwkv_k_k[j], rwkv_k_a[j], rwkv_r_k[j], rwkv_lnx_g[j], rwkv_lnx_b[j], rwkv_w_out[j])
        else:
            y = _dsa_attention(hin, dsa_w_in[j], dsa_kv_norm[j], dsa_w_uk[j], dsa_w_uv[j],
                               dsa_w_out[j], rel_bias)
        x = _layernorm(DEEPNORM_ALPHA * x + (1 + gt1) * y, ln_g[i, 0], ln_b[i, 0])
        hin = x * (1 + sc2) + sh2
        y = _swiglu(hin, ffn_w_in[i], ffn_w_out[i])
        x = _layernorm(DEEPNORM_ALPHA * x + (1 + gt2) * y, ln_g[i, 1], ln_b[i, 1])
    return x
```

```python
import functools
import math

import numpy as np
import jax
import jax.numpy as jnp
from jax import lax
from jax.experimental import pallas as pl
from jax.experimental.pallas import tpu as pltpu

F32 = jnp.float32
BF16 = jnp.bfloat16

D_MODEL = 1024
DEPTH = 4
RWKV_HEAD = 64
RWKV_HEADS = D_MODEL // RWKV_HEAD
GN_EPS = RWKV_HEAD * 1e-5
ATT_HEADS = 16
ATT_HEAD_DIM = 64
KV_LATENT = 128
IDX_HEADS = 8
IDX_DIM = 64
TOPK_MAX = 256
TOPK_DIV = 4
REL_BUCKETS = 32
REL_MAX_DIST = 128
FFN_HIDDEN = 2816
DEEPNORM_ALPHA = (2 * DEPTH) ** 0.25
LN_EPS = 1e-5

LANES = 128
CHUNK = 64
PRE_TM = 256
ROW_TM = 256
FFN_TM = 512
FFN_TF = 1408
ATT_T = 256
MASK_NEG = -1e30
VMEM_LIMIT = 56 * 1024 * 1024
INT_MIN = -2 ** 31


def _cparams(sem):
    return pltpu.CompilerParams(dimension_semantics=sem, vmem_limit_bytes=VMEM_LIMIT)


def _split2(x):
    hi = x.astype(BF16)
    lo = (x - hi.astype(F32)).astype(BF16)
    return hi, lo


def _split3(x):
    hi = x.astype(BF16)
    r1 = x - hi.astype(F32)
    mid = r1.astype(BF16)
    lo = (r1 - mid.astype(F32)).astype(BF16)
    return hi, mid, lo


_NN = (((1,), (0,)), ((), ()))
_NT = (((1,), (1,)), ((), ()))


def _mm(a, b, dims=_NN):
    return lax.dot_general(a, b, dims, preferred_element_type=F32)


def _dot3(a, b, dims=_NN):
    ah, al = _split2(a)
    bh, bl = _split2(b)
    return _mm(ah, bh, dims) + (_mm(ah, bl, dims) + _mm(al, bh, dims))


def _dot_exact_rhs(a, b_bf16):
    h, m, l = _split3(a)
    return _mm(h, b_bf16) + (_mm(m, b_bf16) + _mm(l, b_bf16))


def _sigmoid(x):
    return 1.0 / (1.0 + jnp.exp(-x))


def _layernorm(xr, g, b):
    mu = jnp.mean(xr, axis=-1, keepdims=True)
    xc = xr - mu
    var = jnp.mean(xc * xc, axis=-1, keepdims=True)
    return xc * lax.rsqrt(var + LN_EPS) * g + b


def _adaln_body(c_ref, w_ref, b_ref, o_ref):
    c = c_ref[...]
    cond = c * _sigmoid(c)
    o_ref[0] = _dot3(cond, w_ref[0]) + b_ref[0]


def _adaln(c, ada_w, ada_b):
    depth, d, n = ada_w.shape
    bsz = c.shape[0]
    tn = n // 4
    return pl.pallas_call(
        _adaln_body,
        grid=(depth, n // tn),
        in_specs=[
            pl.BlockSpec((bsz, d), lambda i, j: (0, 0)),
            pl.BlockSpec((1, d, tn), lambda i, j: (i, 0, j)),
            pl.BlockSpec((1, 1, tn), lambda i, j: (i, 0, j)),
        ],
        out_specs=pl.BlockSpec((1, bsz, tn), lambda i, j: (i, 0, j)),
        out_shape=jax.ShapeDtypeStruct((depth, bsz, n), F32),
        compiler_params=_cparams(("parallel", "parallel")),
        name="adaln",
    )(c, ada_w, ada_b.reshape(depth, 1, n))


def _ffn_body(x_ref, mod_ref, wg_ref, wu_ref, wo_ref, lng_ref, lnb_ref, o_ref, hin_ref, acc_ref):
    j = pl.program_id(2)

    @pl.when(j == 0)
    def _():
        x = x_ref[0]
        hin_ref[...] = (x * (1.0 + mod_ref[0, 4:5, :]) + mod_ref[0, 3:4, :]).astype(BF16)
        acc_ref[...] = jnp.zeros_like(acc_ref)

    hin = hin_ref[...]
    gate = _mm(hin, wg_ref[...])
    up = _mm(hin, wu_ref[...])
    hid = (gate * _sigmoid(gate) * up).astype(BF16)
    acc_ref[...] += _mm(hid, wo_ref[...])

    @pl.when(j == pl.num_programs(2) - 1)
    def _():
        res = DEEPNORM_ALPHA * x_ref[0] + (1.0 + mod_ref[0, 5:6, :]) * acc_ref[...]
        o_ref[0] = _layernorm(res, lng_ref[...], lnb_ref[...])


def _ffn(x, mod, w_in, w_out, ln_g, ln_b):
    bsz, t, d = x.shape
    f = w_out.shape[0]
    nf = f // FFN_TF
    return pl.pallas_call(
        _ffn_body,
        grid=(bsz, t // FFN_TM, nf),
        in_specs=[
            pl.BlockSpec((1, FFN_TM, d), lambda b, i, j: (b, i, 0)),
            pl.BlockSpec((1, 6, d), lambda b, i, j: (b, 0, 0)),
            pl.BlockSpec((d, FFN_TF), lambda b, i, j: (0, j)),
            pl.BlockSpec((d, FFN_TF), lambda b, i, j: (0, nf + j)),
            pl.BlockSpec((FFN_TF, d), lambda b, i, j: (j, 0)),
            pl.BlockSpec((1, d), lambda b, i, j: (0, 0)),
            pl.BlockSpec((1, d), lambda b, i, j: (0, 0)),
        ],
        out_specs=pl.BlockSpec((1, FFN_TM, d), lambda b, i, j: (b, i, 0)),
        out_shape=jax.ShapeDtypeStruct((bsz, t, d), F32),
        scratch_shapes=[pltpu.VMEM((FFN_TM, d), BF16), pltpu.VMEM((FFN_TM, d), F32)],
        compiler_params=_cparams(("parallel", "parallel", "arbitrary")),
        name="ffn",
    )(x, mod, w_in, w_in, w_out, ln_g.reshape(1, d), ln_b.reshape(1, d))


def _post_body(gated, *refs):
    if gated:
        y_ref, g_ref, x_ref, mod_ref, w_ref, lng_ref, lnb_ref, o_ref = refs
        y = (y_ref[0] * g_ref[0]).astype(BF16)
    else:
        y_ref, x_ref, mod_ref, w_ref, lng_ref, lnb_ref, o_ref = refs
        y = y_ref[0].astype(BF16)
    out = _mm(y, w_ref[...])
    res = DEEPNORM_ALPHA * x_ref[0] + (1.0 + mod_ref[0, 2:3, :]) * out
    o_ref[0] = _layernorm(res, lng_ref[...], lnb_ref[...])


def _post(y, g, x, mod, w, ln_g, ln_b):
    bsz, t, d = x.shape
    row = pl.BlockSpec((1, ROW_TM, d), lambda b, i: (b, i, 0))
    vec = pl.BlockSpec((1, d), lambda b, i: (0, 0))
    gated = g is not None
    acts = [y, g] if gated else [y]
    return pl.pallas_call(
        functools.partial(_post_body, gated),
        grid=(bsz, t // ROW_TM),
        in_specs=[row] * len(acts) + [
            row,
            pl.BlockSpec((1, 6, d), lambda b, i: (b, 0, 0)),
            pl.BlockSpec((d, d), lambda b, i: (0, 0)),
            vec, vec,
        ],
        out_specs=row,
        out_shape=jax.ShapeDtypeStruct((bsz, t, d), F32),
        compiler_params=_cparams(("parallel", "parallel")),
        name="post",
    )(*acts, x, mod, w, ln_g.reshape(1, d), ln_b.reshape(1, d))


def _rwkv_pre_body(has_vres, *refs):
    if has_vres:
        (x_ref, xp_ref, mod_ref, mu_ref, wr_ref, wk_ref, wv_ref, w1_ref, w2_ref, a1_ref, a2_ref,
         g1_ref, g2_ref, vec_ref, seg_ref, segt_ref, tri_ref, vf_ref, v1_ref, v2_ref,
         r_o, k_o, a_o, b_o, v_o, g_o, gl_o) = refs
    else:
        (x_ref, xp_ref, mod_ref, mu_ref, wr_ref, wk_ref, wv_ref, w1_ref, w2_ref, a1_ref, a2_ref,
         g1_ref, g2_ref, vec_ref, seg_ref, segt_ref, tri_ref,
         r_o, k_o, a_o, b_o, v_o, g_o, gl_o) = refs
    i = pl.program_id(1)
    sc = 1.0 + mod_ref[0, 1:2, :]
    sh = mod_ref[0, 0:1, :]
    hin = x_ref[0] * sc + sh
    tm = hin.shape[0]
    prev_row = xp_ref[0, 7:8, :] * sc + sh
    prev_row = jnp.where(i == 0, 0.0, prev_row)
    rows = lax.broadcasted_iota(jnp.int32, hin.shape, 0)
    hprev = jnp.where(rows == 0, prev_row, pltpu.roll(hin, 1, 0))
    xx = hprev - hin

    def mix(p):
        return hin + xx * mu_ref[p:p + 1, :]

    xr, xk, xv = mix(0).astype(BF16), mix(1).astype(BF16), mix(2).astype(BF16)
    xw, xa, xg = mix(3).astype(BF16), mix(4).astype(BF16), mix(5).astype(BF16)
    w0, a0, kkw, kaw = vec_ref[0:1, :], vec_ref[1:2, :], vec_ref[2:3, :], vec_ref[3:4, :]

    r = _mm(xr, wr_ref[...])
    k = _mm(xk, wk_ref[...])
    v = _mm(xv, wv_ref[...])

    wl = w0 + _mm(jnp.tanh(_mm(xw, w1_ref[...])).astype(BF16), w2_ref[...])
    nz = -wl
    softplus = jnp.maximum(nz, 0.0) + jnp.log(1.0 + jnp.exp(-jnp.abs(nz)))
    logdec = -jnp.exp(-softplus - 0.5)

    a = _sigmoid(a0 + _mm(_mm(xa, a1_ref[...]).astype(BF16), a2_ref[...]))
    if has_vres:
        v0 = vec_ref[4:5, :]
        vmix = _sigmoid(v0 + _mm(_mm(xv, v1_ref[...]).astype(BF16), v2_ref[...]))
        v = v + (vf_ref[0] - v) * vmix
    g = _mm(_sigmoid(_mm(xg, g1_ref[...])).astype(BF16), g2_ref[...])

    kk = k * kkw
    ss = _dot_exact_rhs(kk * kk, seg_ref[...])
    nrm = jnp.maximum(jnp.sqrt(ss), 1e-12)
    inv = _dot_exact_rhs(1.0 / nrm, segt_ref[...])
    kk = kk * inv
    k = k * (1.0 + (a - 1.0) * kaw)

    cum = _dot_exact_rhs_lhs(tri_ref[...], logdec)
    ginc = jnp.exp(cum)
    ginv = jnp.exp(-cum)
    gprev = jnp.exp(cum - logdec)

    r_o[0] = r * ginc
    k_o[0] = k * ginv
    a_o[0] = -kk * gprev
    b_o[0] = kk * a * ginv
    v_o[0] = v
    g_o[0] = g
    for cc in range(tm // CHUNK):
        gl_o[0, cc] = ginc[cc * CHUNK + CHUNK - 1:cc * CHUNK + CHUNK, :]


def _dot_exact_rhs_lhs(m_bf16, x):
    h, mid, l = _split3(x)
    return _mm(m_bf16, h) + (_mm(m_bf16, mid) + _mm(m_bf16, l))


def _rwkv_pre(x, mod, p, v_first):
    bsz, t, d = x.shape
    tm = PRE_TM
    has_vres = v_first is not None
    row = pl.BlockSpec((1, tm, d), lambda b, i: (b, i, 0))

    def full(shape):
        return pl.BlockSpec(shape, lambda b, i: (0,) * len(shape))

    heads = d // RWKV_HEAD
    seg = np.zeros((d, LANES), np.float32)
    seg[np.arange(d), np.arange(d) // RWKV_HEAD] = 1.0
    tri = np.zeros((tm, tm), np.float32)
    idx = np.arange(tm)
    tri[(idx[:, None] >= idx[None, :]) & (idx[:, None] // CHUNK == idx[None, :] // CHUNK)] = 1.0
    vec_rows = [p['w0'], p['a0'], p['k_k'], p['k_a']] + ([p['v0']] if has_vres else [])
    vec = jnp.stack(vec_rows + [jnp.zeros_like(p['w0'])] * (8 - len(vec_rows)))
    dl, da, dg = p['w1'].shape[1], p['a1'].shape[1], p['g1'].shape[1]
    ins = [x, x, mod, p['mu'], p['wr'], p['wk'], p['wv'], p['w1'], p['w2'], p['a1'], p['a2'],
           p['g1'], p['g2'], vec, jnp.asarray(seg, BF16), jnp.asarray(seg.T, BF16),
           jnp.asarray(tri, BF16)]
    specs = [row,
             pl.BlockSpec((1, 8, d), lambda b, i: (b, jnp.maximum(i * (tm // 8) - 1, 0), 0)),
             pl.BlockSpec((1, 6, d), lambda b, i: (b, 0, 0)),
             full((6, d)), full((d, d)), full((d, d)), full((d, d)),
             full((d, dl)), full((dl, d)), full((d, da)), full((da, d)),
             full((d, dg)), full((dg, d)), full((8, d)),
             full((d, LANES)), full((LANES, d)), full((tm, tm))]
    if has_vres:
        dv = p['v1'].shape[1]
        ins += [v_first, p['v1'], p['v2']]
        specs += [row, full((d, dv)), full((dv, d))]
    act = jax.ShapeDtypeStruct((bsz, t, d), F32)
    nch = t // CHUNK
    outs = pl.pallas_call(
        functools.partial(_rwkv_pre_body, has_vres),
        grid=(bsz, t // tm),
        in_specs=specs,
        out_specs=[row] * 6 + [pl.BlockSpec((1, tm // CHUNK, 1, d), lambda b, i: (b, i, 0, 0))],
        out_shape=[act] * 6 + [jax.ShapeDtypeStruct((bsz, nch, 1, d), F32)],
        compiler_params=_cparams(("parallel", "parallel")),
        name="rwkv_pre",
    )(*ins)
    return outs


def _rwkv_scan_body(r_ref, k_ref, a_ref, b_ref, v_ref, gl_ref, vec_ref, o_ref, s_ref):
    c = pl.program_id(1)

    @pl.when(c == 0)
    def _():
        s_ref[...] = jnp.zeros_like(s_ref)

    L = CHUNK
    lane = lax.broadcasted_iota(jnp.int32, (L, LANES), 1)
    h0 = lane < RWKV_HEAD
    tt = lax.broadcasted_iota(jnp.int32, (L, LANES), 0)
    ss = lane & (RWKV_HEAD - 1)
    strict = ss < tt
    incl = ss <= tt
    eye = jnp.where(ss == tt, 1.0, 0.0)
    lvl0 = strict & ((ss >> 4) == (tt >> 4))
    lvl1 = strict & ((ss >> 5) == (tt >> 5)) & ((ss >> 4) != (tt >> 4))
    lvl2 = (ss >> 5) != (tt >> 5)
    ri = lax.broadcasted_iota(jnp.int32, (LANES, LANES), 0)
    ci = lax.broadcasted_iota(jnp.int32, (LANES, LANES), 1)
    blockdiag = (ri >> 6) == (ci >> 6)
    seg_mean = jnp.where(blockdiag, 1.0, 0.0).astype(BF16)

    def bd(x):
        return jnp.concatenate([jnp.where(h0, x, 0.0), jnp.where(h0, 0.0, x)], axis=0)

    def pmm(xp, y):
        return _dot3(xp, bd(y))

    for hp in range(RWKV_HEADS // 2):
        sl = slice(hp * LANES, (hp + 1) * LANES)
        R, K, A, Bv, V = r_ref[0, :, sl], k_ref[0, :, sl], a_ref[0, :, sl], b_ref[0, :, sl], v_ref[0, :, sl]
        gl = gl_ref[0, 0, :, sl]
        S = s_ref[hp]

        AR = jnp.concatenate([A, R], axis=0)
        BK = jnp.concatenate([bd(Bv), bd(K)], axis=0)
        G = _dot3(AR, BK, _NT)
        A_ab = jnp.where(strict, G[0:L, 0:LANES], 0.0)
        A_ak = jnp.where(strict, G[0:L, LANES:2 * LANES], 0.0)
        A_rb = jnp.where(incl, G[L:2 * L, 0:LANES], 0.0)
        A_rk = jnp.where(incl, G[L:2 * L, LANES:2 * LANES], 0.0)

        a0 = jnp.where(lvl0, A_ab, 0.0)
        Tm = eye + a0
        P = a0
        for _ in range(3):
            P = pmm(P, P)
            Tm = Tm + pmm(Tm, P)
        for lvl in (lvl1, lvl2):
            Tm = Tm + pmm(pmm(Tm, jnp.where(lvl, A_ab, 0.0)), Tm)

        PQ = _dot3(AR, S, _NT)
        W = PQ[0:L] + pmm(A_ak, V)
        U = pmm(Tm, W)
        Y = PQ[L:2 * L] + _dot3(jnp.concatenate([A_rb, A_rk], axis=1),
                                 jnp.concatenate([bd(U), bd(V)], axis=0))
        UV = jnp.concatenate([U, V], axis=0)
        BK2 = jnp.concatenate([Bv, K], axis=0)
        upd = _dot3(UV.T, BK2)
        s_ref[hp] = (S + jnp.where(blockdiag, upd, 0.0)) * gl

        inv_n = 1.0 / RWKV_HEAD
        mean = _dot_exact_rhs(Y, seg_mean) * inv_n
        yc = Y - mean
        var = _dot_exact_rhs(yc * yc, seg_mean) * inv_n
        yn = yc * lax.rsqrt(var + GN_EPS) * vec_ref[1:2, sl] + vec_ref[2:3, sl]
        bonus = _dot_exact_rhs(R * K * vec_ref[0:1, sl], seg_mean)
        o_ref[0, :, sl] = yn + bonus * V


def _rwkv_scan(r, k, a, b, v, gl, r_k, lnx_g, lnx_b):
    bsz, t, d = r.shape
    row = pl.BlockSpec((1, CHUNK, d), lambda bb, c: (bb, c, 0))
    vec = jnp.stack([r_k.reshape(d), lnx_g, lnx_b] + [jnp.zeros((d,), F32)] * 5)
    return pl.pallas_call(
        _rwkv_scan_body,
        grid=(bsz, t // CHUNK),
        in_specs=[row] * 5 + [
            pl.BlockSpec((1, 1, 1, d), lambda bb, c: (bb, c, 0, 0)),
            pl.BlockSpec((8, d), lambda bb, c: (0, 0)),
        ],
        out_specs=row,
        out_shape=jax.ShapeDtypeStruct((bsz, t, d), F32),
        scratch_shapes=[pltpu.VMEM((RWKV_HEADS // 2, LANES, LANES), F32)],
        compiler_params=_cparams(("parallel", "arbitrary")),
        name="rwkv_scan",
    )(r, k, a, b, v, gl, vec)


IDX_COLS = 768


def _dsa_proj_body(x_ref, mod_ref, wq_ref, wc_ref, wih_ref, wil_ref, kvn_ref,
                   q_o, ckv_o, qi_o, ki_o, wi_o):
    hin = x_ref[0] * (1.0 + mod_ref[0, 1:2, :]) + mod_ref[0, 0:1, :]
    hh, hl = _split2(hin)
    q_o[0] = _mm(hh, wq_ref[...])
    ckv = _mm(hh, wc_ref[...])
    ms = jnp.mean(ckv * ckv, axis=-1, keepdims=True)
    ckv_o[0] = ckv * lax.rsqrt(ms + 1e-6) * kvn_ref[...]
    idx = _mm(hh, wih_ref[...]) + (_mm(hh, wil_ref[...]) + _mm(hl, wih_ref[...]))
    nq = IDX_HEADS * IDX_DIM
    qi_o[0] = idx[:, 0:nq]
    ki_o[0] = idx[:, nq:nq + LANES]
    wi_o[0] = idx[:, nq + LANES:nq + 2 * LANES] * (IDX_HEADS ** -0.5 * IDX_DIM ** -0.5)


def _dsa_proj(x, mod, w_in, kv_norm):
    bsz, t, d = x.shape
    c1 = ATT_HEADS * ATT_HEAD_DIM
    c2 = c1 + KV_LATENT
    c3 = c2 + IDX_HEADS * IDX_DIM
    c4 = c3 + IDX_DIM
    wq = w_in[:, :c1].astype(BF16)
    wc = w_in[:, c1:c2].astype(BF16)
    pad = IDX_COLS - (c3 - c2) - 2 * IDX_DIM - IDX_HEADS
    widx = jnp.concatenate([w_in[:, c2:c3], w_in[:, c3:c4], w_in[:, c3:c4], w_in[:, c4:],
                            jnp.zeros((d, pad), F32)], axis=1)
    wih = widx.astype(BF16)
    wil = (widx - wih.astype(F32)).astype(BF16)
    tm = ROW_TM

    def full(shape):
        return pl.BlockSpec(shape, lambda b, i: (0,) * len(shape))

    def row(n):
        return pl.BlockSpec((1, tm, n), lambda b, i: (b, i, 0))

    def act(n):
        return jax.ShapeDtypeStruct((bsz, t, n), F32)

    nq = IDX_HEADS * IDX_DIM
    return pl.pallas_call(
        _dsa_proj_body,
        grid=(bsz, t // tm),
        in_specs=[row(d), pl.BlockSpec((1, 6, d), lambda b, i: (b, 0, 0)),
                  full((d, c1)), full((d, KV_LATENT)), full((d, IDX_COLS)), full((d, IDX_COLS)),
                  full((1, KV_LATENT))],
        out_specs=[row(c1), row(KV_LATENT), row(nq), row(LANES), row(LANES)],
        out_shape=[act(c1), act(KV_LATENT), act(nq), act(LANES), act(LANES)],
        compiler_params=_cparams(("parallel", "parallel")),
        name="dsa_proj",
    )(x, mod, wq, wc, wih, wil, kv_norm.reshape(1, KV_LATENT))


def _dsa_index_body(k_sel, qi_ref, wi_ref, ki_ref, o_ref, key_ref):
    i = pl.program_id(1)
    tq = qi_ref.shape[1]
    t = ki_ref.shape[1]
    kk2 = ki_ref[0]
    kh, kl = _split2(kk2)
    wt = wi_ref[0].T
    lane = lax.broadcasted_iota(jnp.int32, (tq, LANES), 1)
    first = lane < IDX_DIM
    score = jnp.zeros((t, tq), F32)
    for hp in range(IDX_HEADS // 2):
        qp = qi_ref[0, :, hp * LANES:(hp + 1) * LANES]
        for sub in range(2):
            qm = jnp.where(first, qp, 0.0) if sub == 0 else jnp.where(first, 0.0, qp)
            qh, ql = _split2(qm)
            s = _mm(kh, qh, _NT) + (_mm(kh, ql, _NT) + _mm(kl, qh, _NT))
            h = 2 * hp + sub
            score = score + wt[h:h + 1, :] * jnp.maximum(s, 0.0)
    score = jnp.where(score == 0.0, 0.0, score)
    bits = pltpu.bitcast(score, jnp.int32)
    skey = bits ^ ((bits >> 31) & 0x7FFFFFFF)
    kpos = lax.broadcasted_iota(jnp.int32, (t, tq), 0)
    qpos = i * tq + lax.broadcasted_iota(jnp.int32, (t, tq), 1)
    causal = kpos <= qpos
    key_ref[...] = jnp.where(causal, skey, INT_MIN)

    kf = float(k_sel)

    def count_ge(cand):
        return jnp.sum(jnp.where(key_ref[...] >= cand, 1.0, 0.0), axis=0, keepdims=True)

    thr0 = jnp.where(count_ge(jnp.zeros((1, tq), jnp.int32)) >= kf, 0, INT_MIN).astype(jnp.int32)

    def thr_step(n, thr):
        cand = thr | jnp.left_shift(jnp.int32(1), 30 - n)
        return jnp.where(count_ge(cand) >= kf, cand, thr)

    thr = lax.fori_loop(0, 31, thr_step, thr0)
    keys = key_ref[...]
    gt = keys > thr
    n_gt = jnp.sum(jnp.where(gt, 1.0, 0.0), axis=0, keepdims=True)
    need = kf - n_gt
    eqf = jnp.where(keys == thr, 1.0, 0.0)

    nbits = int(t - 1).bit_length()

    def cut_step(n, cut):
        cand = cut | jnp.left_shift(jnp.int32(1), nbits - 1 - n)
        cnt = jnp.sum(jnp.where(kpos < cand, eqf, 0.0), axis=0, keepdims=True)
        return jnp.where(cnt < need, cand, cut)

    cut = lax.fori_loop(0, nbits, cut_step, jnp.zeros((1, tq), jnp.int32))
    sel = (gt | ((eqf > 0.0) & (kpos <= cut))) & causal
    o_ref[0] = jnp.where(sel, 0.0, MASK_NEG).astype(BF16)


def _dsa_index(qi, ki, wi, k_sel):
    bsz, t, nq = qi.shape
    tq = ATT_T
    return pl.pallas_call(
        functools.partial(_dsa_index_body, k_sel),
        grid=(bsz, t // tq),
        in_specs=[pl.BlockSpec((1, tq, nq), lambda b, i: (b, i, 0)),
                  pl.BlockSpec((1, tq, LANES), lambda b, i: (b, i, 0)),
                  pl.BlockSpec((1, t, LANES), lambda b, i: (b, 0, 0))],
        out_specs=pl.BlockSpec((1, t, tq), lambda b, i: (b, 0, i)),
        out_shape=jax.ShapeDtypeStruct((bsz, t, t), BF16),
        scratch_shapes=[pltpu.VMEM((t, tq), jnp.int32)],
        compiler_params=_cparams(("parallel", "parallel")),
        name="dsa_index",
    )(qi, wi, ki)


def _t5_bucket_np(n):
    n = np.maximum(n, 0)
    max_exact = REL_BUCKETS // 2
    nf = np.maximum(n, 1).astype(np.float32)
    large = max_exact + (np.log(nf / np.float32(max_exact)) / np.float32(math.log(REL_MAX_DIST / max_exact))
                         * np.float32(REL_BUCKETS - max_exact)).astype(np.int32)
    large = np.minimum(large, REL_BUCKETS - 1)
    return np.where(n < max_exact, n, large).astype(np.int32)


def _band_body(bkt_ref, rb_ref, o_ref):
    h = pl.program_id(1)
    bkt = bkt_ref[0]
    acc = jnp.zeros(bkt.shape, F32)
    for b in range(REL_BUCKETS):
        acc = jnp.where(bkt == b, rb_ref[b, h], acc)
    o_ref[0, 0] = acc


def _band_bias(rel_bias):
    tt = ATT_T
    kc = np.arange(tt)[:, None]
    qr = np.arange(tt)[None, :]
    planes = [_t5_bucket_np(d * tt + qr - kc) for d in range(3)]
    assert (planes[2] == REL_BUCKETS - 1).all() and tt + 1 >= 113
    bkt = jnp.asarray(np.stack(planes))
    return pl.pallas_call(
        _band_body,
        grid=(3, ATT_HEADS),
        in_specs=[pl.BlockSpec((1, tt, tt), lambda d, h: (d, 0, 0)),
                  pl.BlockSpec(memory_space=pltpu.SMEM)],
        out_specs=pl.BlockSpec((1, 1, tt, tt), lambda d, h: (d, h, 0, 0)),
        out_shape=jax.ShapeDtypeStruct((3, ATT_HEADS, tt, tt), F32),
        compiler_params=_cparams(("parallel", "parallel")),
        name="band_bias",
    )(bkt, rel_bias)


def _dsa_attn_body(q_ref, ckv_ref, mask_ref, wuk_ref, wuv_ref, band_ref, o_ref,
                   ql_ref, m_ref, l_ref, acc_ref, ot_ref):
    i = pl.program_id(1)
    j = pl.program_id(2)
    nh = ATT_HEADS
    scale = ATT_HEAD_DIM ** -0.5

    @pl.when(j == 0)
    def _():
        for hp in range(nh // 2):
            qp = q_ref[0, :, hp * LANES:(hp + 1) * LANES].astype(BF16)
            qlat = _mm(qp, wuk_ref[hp])
            ql_ref[2 * hp] = qlat[:, 0:KV_LATENT].astype(BF16)
            ql_ref[2 * hp + 1] = qlat[:, KV_LATENT:2 * KV_LATENT].astype(BF16)
        m_ref[...] = jnp.full(m_ref.shape, MASK_NEG, F32)
        l_ref[...] = jnp.zeros_like(l_ref)
        acc_ref[...] = jnp.zeros_like(acc_ref)

    @pl.when(j <= i)
    def _():
        ckv = ckv_ref[0].astype(BF16)
        ckv_t = ckv_ref[0].T.astype(BF16)
        maskb = mask_ref[0].astype(F32)
        d = jnp.minimum(i - j, 2)
        for h in range(nh):
            s = _mm(ckv, ql_ref[h], _NT) * scale + band_ref[d, h] + maskb
            m_prev = m_ref[h]
            m_new = jnp.maximum(m_prev, jnp.max(s, axis=0, keepdims=True))
            alpha = jnp.exp(m_prev - m_new)
            p = jnp.exp(s - m_new)
            l_ref[h] = alpha * l_ref[h] + jnp.sum(p, axis=0, keepdims=True)
            acc_ref[h] = alpha * acc_ref[h] + _mm(ckv_t, p.astype(BF16))
            m_ref[h] = m_new

    @pl.when(j == i)
    def _():
        for hp in range(nh // 2):
            o0 = acc_ref[2 * hp] * (1.0 / l_ref[2 * hp])
            o1 = acc_ref[2 * hp + 1] * (1.0 / l_ref[2 * hp + 1])
            olat = jnp.concatenate([o0, o1], axis=0).astype(BF16)
            ot_ref[hp * LANES:(hp + 1) * LANES, :] = _mm(wuv_ref[hp], olat)
        o_ref[0] = ot_ref[...].T


def _dsa_attn(q, ckv, maskt, w_uk, w_uv, band):
    bsz, t, d = q.shape
    tt = ATT_T
    nt = t // tt
    nh = ATT_HEADS
    zk = jnp.zeros((nh // 2, ATT_HEAD_DIM, KV_LATENT), F32)
    wuk2 = jnp.concatenate([jnp.concatenate([w_uk[0::2], zk], axis=2),
                            jnp.concatenate([zk, w_uk[1::2]], axis=2)], axis=1).astype(BF16)
    wuv_t = jnp.swapaxes(w_uv, 1, 2)
    zv = jnp.zeros((nh // 2, ATT_HEAD_DIM, KV_LATENT), F32)
    wuv2 = jnp.concatenate([jnp.concatenate([wuv_t[0::2], zv], axis=2),
                            jnp.concatenate([zv, wuv_t[1::2]], axis=2)], axis=1).astype(BF16)
    return pl.pallas_call(
        _dsa_attn_body,
        grid=(bsz, nt, nt),
        in_specs=[pl.BlockSpec((1, tt, d), lambda b, i, j: (b, i, 0)),
                  pl.BlockSpec((1, tt, KV_LATENT), lambda b, i, j: (b, jnp.minimum(j, i), 0)),
                  pl.BlockSpec((1, tt, tt), lambda b, i, j: (b, jnp.minimum(j, i), i)),
                  pl.BlockSpec((nh // 2, LANES, 2 * KV_LATENT), lambda b, i, j: (0, 0, 0)),
                  pl.BlockSpec((nh // 2, LANES, 2 * KV_LATENT), lambda b, i, j: (0, 0, 0)),
                  pl.BlockSpec((3, nh, tt, tt), lambda b, i, j: (0, 0, 0, 0))],
        out_specs=pl.BlockSpec((1, tt, d), lambda b, i, j: (b, i, 0)),
        out_shape=jax.ShapeDtypeStruct((bsz, t, d), F32),
        scratch_shapes=[pltpu.VMEM((nh, tt, KV_LATENT), BF16),
                        pltpu.VMEM((nh, 1, tt), F32),
                        pltpu.VMEM((nh, 1, tt), F32),
                        pltpu.VMEM((nh, KV_LATENT, tt), F32),
                        pltpu.VMEM((d, tt), F32)],
        compiler_params=_cparams(("parallel", "parallel", "arbitrary")),
        name="dsa_attn",
    )(q, ckv, maskt, wuk2, wuv2, band)


def kernel(x, c, ada_w, ada_b, ln_g, ln_b, ffn_w_in, ffn_w_out, rwkv_mu, rwkv_w_rkv, rwkv_w0, rwkv_w1, rwkv_w2, rwkv_a0, rwkv_a1, rwkv_a2, rwkv_v0, rwkv_v1, rwkv_v2, rwkv_g1, rwkv_g2, rwkv_k_k, rwkv_k_a, rwkv_r_k, rwkv_lnx_g, rwkv_lnx_b, rwkv_w_out, dsa_w_in, dsa_kv_norm, dsa_w_uk, dsa_w_uv, dsa_w_out, rel_bias):
    bsz, t, d = x.shape
    assert d == D_MODEL and t % ATT_T == 0 and t % FFN_TM == 0 and t % PRE_TM == 0
    mod_all = _adaln(c, ada_w, ada_b).reshape(DEPTH, bsz, 6, d)
    band = _band_bias(rel_bias)
    k_sel = min(TOPK_MAX, t // TOPK_DIV)
    bf = lambda w: w.astype(BF16)
    v_first = None
    for i in range(DEPTH):
        mod = mod_all[i]
        j = i // 2
        if i % 2 == 0:
            p = dict(mu=rwkv_mu[j], wr=bf(rwkv_w_rkv[j, 0]), wk=bf(rwkv_w_rkv[j, 1]), wv=bf(rwkv_w_rkv[j, 2]),
                     w0=rwkv_w0[j], w1=bf(rwkv_w1[j]), w2=bf(rwkv_w2[j]),
                     a0=rwkv_a0[j], a1=bf(rwkv_a1[j]), a2=bf(rwkv_a2[j]),
                     g1=bf(rwkv_g1[j]), g2=bf(rwkv_g2[j]), k_k=rwkv_k_k[j], k_a=rwkv_k_a[j])
            if j > 0:
                p.update(v0=rwkv_v0[j - 1], v1=bf(rwkv_v1[j - 1]), v2=bf(rwkv_v2[j - 1]))
            r_s, k_s, a_s, b_s, v, g, gl = _rwkv_pre(x, mod, p, v_first if j > 0 else None)
            if j == 0:
                v_first = v
            z = _rwkv_scan(r_s, k_s, a_s, b_s, v, gl, rwkv_r_k[j], rwkv_lnx_g[j], rwkv_lnx_b[j])
            x = _post(z, g, x, mod, bf(rwkv_w_out[j]), ln_g[i, 0], ln_b[i, 0])
        else:
            q, ckv, qi, ki, wi = _dsa_proj(x, mod, dsa_w_in[j], dsa_kv_norm[j])
            maskt = _dsa_index(qi, ki, wi, k_sel)
            o = _dsa_attn(q, ckv, maskt, dsa_w_uk[j], dsa_w_uv[j], band)
            x = _post(o, None, x, mod, bf(dsa_w_out[j]), ln_g[i, 0], ln_b[i, 0])
        x = _ffn(x, mod, bf(ffn_w_in[i]), bf(ffn_w_out[i]), ln_g[i, 1], ln_b[i, 1])
    return x
```

```python
import functools
import math

import numpy as np
import jax
import jax.numpy as jnp
from jax import lax
from jax.experimental import pallas as pl
from jax.experimental.pallas import tpu as pltpu

F32 = jnp.float32
BF16 = jnp.bfloat16

D_MODEL = 1024
DEPTH = 4
RWKV_HEAD = 64
RWKV_HEADS = D_MODEL // RWKV_HEAD
GN_EPS = RWKV_HEAD * 1e-5
ATT_HEADS = 16
ATT_HEAD_DIM = 64
KV_LATENT = 128
IDX_HEADS = 8
IDX_DIM = 64
TOPK_MAX = 256
TOPK_DIV = 4
REL_BUCKETS = 32
REL_MAX_DIST = 128
FFN_HIDDEN = 2816
DEEPNORM_ALPHA = (2 * DEPTH) ** 0.25
LN_EPS = 1e-5

LANES = 128
CHUNK = 64
PRE_TM = 256
ROW_TM = 256
FFN_TM = 512
FFN_TF = 1408
ATT_T = 256
MASK_NEG = -1e30
VMEM_LIMIT = 56 * 1024 * 1024
INT_MIN = -2 ** 31


def _cparams(sem):
    return pltpu.CompilerParams(dimension_semantics=sem, vmem_limit_bytes=VMEM_LIMIT)


def _split2(x):
    hi = x.astype(BF16)
    lo = (x - hi.astype(F32)).astype(BF16)
    return hi, lo


def _split3(x):
    hi = x.astype(BF16)
    r1 = x - hi.astype(F32)
    mid = r1.astype(BF16)
    lo = (r1 - mid.astype(F32)).astype(BF16)
    return hi, mid, lo


_NN = (((1,), (0,)), ((), ()))
_NT = (((1,), (1,)), ((), ()))


def _mm(a, b, dims=_NN):
    return lax.dot_general(a, b, dims, preferred_element_type=F32)


def _dot3(a, b, dims=_NN):
    ah, al = _split2(a)
    bh, bl = _split2(b)
    return _mm(ah, bh, dims) + (_mm(ah, bl, dims) + _mm(al, bh, dims))


def _dot1(a, b, dims=_NN):
    return _mm(a.astype(BF16), b.astype(BF16), dims)


def _dot_exact_rhs(a, b_bf16):
    h, m, l = _split3(a)
    return _mm(h, b_bf16) + (_mm(m, b_bf16) + _mm(l, b_bf16))


def _sigmoid(x):
    return 1.0 / (1.0 + jnp.exp(-x))


def _layernorm(xr, g, b):
    mu = jnp.mean(xr, axis=-1, keepdims=True)
    xc = xr - mu
    var = jnp.mean(xc * xc, axis=-1, keepdims=True)
    return xc * lax.rsqrt(var + LN_EPS) * g + b


def _adaln_body(c_ref, w_ref, b_ref, o_ref):
    c = c_ref[...]
    cond = c * _sigmoid(c)
    o_ref[0] = _dot3(cond, w_ref[0]) + b_ref[0]


def _adaln(c, ada_w, ada_b):
    depth, d, n = ada_w.shape
    bsz = c.shape[0]
    tn = n // 4
    return pl.pallas_call(
        _adaln_body,
        grid=(depth, n // tn),
        in_specs=[
            pl.BlockSpec((bsz, d), lambda i, j: (0, 0)),
            pl.BlockSpec((1, d, tn), lambda i, j: (i, 0, j)),
            pl.BlockSpec((1, 1, tn), lambda i, j: (i, 0, j)),
        ],
        out_specs=pl.BlockSpec((1, bsz, tn), lambda i, j: (i, 0, j)),
        out_shape=jax.ShapeDtypeStruct((depth, bsz, n), F32),
        compiler_params=_cparams(("parallel", "parallel")),
        name="adaln",
    )(c, ada_w, ada_b.reshape(depth, 1, n))


def _ffn_body(x_ref, mod_ref, wg_ref, wu_ref, wo_ref, lng_ref, lnb_ref, o_ref, hin_ref, acc_ref):
    j = pl.program_id(2)

    @pl.when(j == 0)
    def _():
        x = x_ref[0]
        hin_ref[...] = (x * (1.0 + mod_ref[0, 4:5, :]) + mod_ref[0, 3:4, :]).astype(BF16)
        acc_ref[...] = jnp.zeros_like(acc_ref)

    hin = hin_ref[...]
    gate = _mm(hin, wg_ref[...])
    up = _mm(hin, wu_ref[...])
    hid = (gate * _sigmoid(gate) * up).astype(BF16)
    acc_ref[...] += _mm(hid, wo_ref[...])

    @pl.when(j == pl.num_programs(2) - 1)
    def _():
        res = DEEPNORM_ALPHA * x_ref[0] + (1.0 + mod_ref[0, 5:6, :]) * acc_ref[...]
        o_ref[0] = _layernorm(res, lng_ref[...], lnb_ref[...])


def _ffn(x, mod, w_in, w_out, ln_g, ln_b):
    bsz, t, d = x.shape
    f = w_out.shape[0]
    nf = f // FFN_TF
    return pl.pallas_call(
        _ffn_body,
        grid=(bsz, t // FFN_TM, nf),
        in_specs=[
            pl.BlockSpec((1, FFN_TM, d), lambda b, i, j: (b, i, 0)),
            pl.BlockSpec((1, 6, d), lambda b, i, j: (b, 0, 0)),
            pl.BlockSpec((d, FFN_TF), lambda b, i, j: (0, j)),
            pl.BlockSpec((d, FFN_TF), lambda b, i, j: (0, nf + j)),
            pl.BlockSpec((FFN_TF, d), lambda b, i, j: (j, 0)),
            pl.BlockSpec((1, d), lambda b, i, j: (0, 0)),
            pl.BlockSpec((1, d), lambda b, i, j: (0, 0)),
        ],
        out_specs=pl.BlockSpec((1, FFN_TM, d), lambda b, i, j: (b, i, 0)),
        out_shape=jax.ShapeDtypeStruct((bsz, t, d), F32),
        scratch_shapes=[pltpu.VMEM((FFN_TM, d), BF16), pltpu.VMEM((FFN_TM, d), F32)],
        compiler_params=_cparams(("parallel", "parallel", "arbitrary")),
        name="ffn",
    )(x, mod, w_in, w_in, w_out, ln_g.reshape(1, d), ln_b.reshape(1, d))


def _post_body(gated, *refs):
    if gated:
        y_ref, g_ref, x_ref, mod_ref, w_ref, lng_ref, lnb_ref, o_ref = refs
        y = (y_ref[0] * g_ref[0]).astype(BF16)
    else:
        y_ref, x_ref, mod_ref, w_ref, lng_ref, lnb_ref, o_ref = refs
        y = y_ref[0].astype(BF16)
    out = _mm(y, w_ref[...])
    res = DEEPNORM_ALPHA * x_ref[0] + (1.0 + mod_ref[0, 2:3, :]) * out
    o_ref[0] = _layernorm(res, lng_ref[...], lnb_ref[...])


def _post(y, g, x, mod, w, ln_g, ln_b):
    bsz, t, d = x.shape
    row = pl.BlockSpec((1, ROW_TM, d), lambda b, i: (b, i, 0))
    vec = pl.BlockSpec((1, d), lambda b, i: (0, 0))
    gated = g is not None
    acts = [y, g] if gated else [y]
    return pl.pallas_call(
        functools.partial(_post_body, gated),
        grid=(bsz, t // ROW_TM),
        in_specs=[row] * len(acts) + [
            row,
            pl.BlockSpec((1, 6, d), lambda b, i: (b, 0, 0)),
            pl.BlockSpec((d, d), lambda b, i: (0, 0)),
            vec, vec,
        ],
        out_specs=row,
        out_shape=jax.ShapeDtypeStruct((bsz, t, d), F32),
        compiler_params=_cparams(("parallel", "parallel")),
        name="post",
    )(*acts, x, mod, w, ln_g.reshape(1, d), ln_b.reshape(1, d))


def _rwkv_pre_body(has_vres, *refs):
    if has_vres:
        (x_ref, xp_ref, mod_ref, mu_ref, wr_ref, wk_ref, wv_ref, w1_ref, w2_ref, a1_ref, a2_ref,
         g1_ref, g2_ref, vec_ref, seg_ref, segt_ref, tri_ref, vf_ref, v1_ref, v2_ref,
         r_o, k_o, a_o, b_o, v_o, g_o, gl_o) = refs
    else:
        (x_ref, xp_ref, mod_ref, mu_ref, wr_ref, wk_ref, wv_ref, w1_ref, w2_ref, a1_ref, a2_ref,
         g1_ref, g2_ref, vec_ref, seg_ref, segt_ref, tri_ref,
         r_o, k_o, a_o, b_o, v_o, g_o, gl_o) = refs
    i = pl.program_id(1)
    sc = 1.0 + mod_ref[0, 1:2, :]
    sh = mod_ref[0, 0:1, :]
    hin = x_ref[0] * sc + sh
    tm = hin.shape[0]
    prev_row = xp_ref[0, 7:8, :] * sc + sh
    prev_row = jnp.where(i == 0, 0.0, prev_row)
    rows = lax.broadcasted_iota(jnp.int32, hin.shape, 0)
    hprev = jnp.where(rows == 0, prev_row, pltpu.roll(hin, 1, 0))
    xx = hprev - hin

    def mix(p):
        return hin + xx * mu_ref[p:p + 1, :]

    xr, xk, xv = mix(0).astype(BF16), mix(1).astype(BF16), mix(2).astype(BF16)
    xw, xa, xg = mix(3).astype(BF16), mix(4).astype(BF16), mix(5).astype(BF16)
    w0, a0, kkw, kaw = vec_ref[0:1, :], vec_ref[1:2, :], vec_ref[2:3, :], vec_ref[3:4, :]

    r = _mm(xr, wr_ref[...])
    k = _mm(xk, wk_ref[...])
    v = _mm(xv, wv_ref[...])

    wl = w0 + _mm(jnp.tanh(_mm(xw, w1_ref[...])).astype(BF16), w2_ref[...])
    nz = -wl
    softplus = jnp.maximum(nz, 0.0) + jnp.log(1.0 + jnp.exp(-jnp.abs(nz)))
    logdec = -jnp.exp(-softplus - 0.5)

    a = _sigmoid(a0 + _mm(_mm(xa, a1_ref[...]).astype(BF16), a2_ref[...]))
    if has_vres:
        v0 = vec_ref[4:5, :]
        vmix = _sigmoid(v0 + _mm(_mm(xv, v1_ref[...]).astype(BF16), v2_ref[...]))
        v = v + (vf_ref[0] - v) * vmix
    g = _mm(_sigmoid(_mm(xg, g1_ref[...])).astype(BF16), g2_ref[...])

    kk = k * kkw
    ss = _dot_exact_rhs(kk * kk, seg_ref[...])
    nrm = jnp.maximum(jnp.sqrt(ss), 1e-12)
    inv = _dot_exact_rhs(1.0 / nrm, segt_ref[...])
    kk = kk * inv
    k = k * (1.0 + (a - 1.0) * kaw)

    cum = _dot_exact_rhs_lhs(tri_ref[...], logdec)
    ginc = jnp.exp(cum)
    ginv = jnp.exp(-cum)
    gprev = jnp.exp(cum - logdec)

    r_o[0] = r * ginc
    k_o[0] = k * ginv
    a_o[0] = -kk * gprev
    b_o[0] = kk * a * ginv
    v_o[0] = v
    g_o[0] = g
    for cc in range(tm // CHUNK):
        gl_o[0, cc] = ginc[cc * CHUNK + CHUNK - 1:cc * CHUNK + CHUNK, :]


def _dot_exact_rhs_lhs(m_bf16, x):
    h, mid, l = _split3(x)
    return _mm(m_bf16, h) + (_mm(m_bf16, mid) + _mm(m_bf16, l))


def _rwkv_pre(x, mod, p, v_first):
    bsz, t, d = x.shape
    tm = PRE_TM
    has_vres = v_first is not None
    row = pl.BlockSpec((1, tm, d), lambda b, i: (b, i, 0))

    def full(shape):
        return pl.BlockSpec(shape, lambda b, i: (0,) * len(shape))

    heads = d // RWKV_HEAD
    seg = np.zeros((d, LANES), np.float32)
    seg[np.arange(d), np.arange(d) // RWKV_HEAD] = 1.0
    tri = np.zeros((tm, tm), np.float32)
    idx = np.arange(tm)
    tri[(idx[:, None] >= idx[None, :]) & (idx[:, None] // CHUNK == idx[None, :] // CHUNK)] = 1.0
    vec_rows = [p['w0'], p['a0'], p['k_k'], p['k_a']] + ([p['v0']] if has_vres else [])
    vec = jnp.stack(vec_rows + [jnp.zeros_like(p['w0'])] * (8 - len(vec_rows)))
    dl, da, dg = p['w1'].shape[1], p['a1'].shape[1], p['g1'].shape[1]
    ins = [x, x, mod, p['mu'], p['wr'], p['wk'], p['wv'], p['w1'], p['w2'], p['a1'], p['a2'],
           p['g1'], p['g2'], vec, jnp.asarray(seg, BF16), jnp.asarray(seg.T, BF16),
           jnp.asarray(tri, BF16)]
    specs = [row,
             pl.BlockSpec((1, 8, d), lambda b, i: (b, jnp.maximum(i * (tm // 8) - 1, 0), 0)),
             pl.BlockSpec((1, 6, d), lambda b, i: (b, 0, 0)),
             full((6, d)), full((d, d)), full((d, d)), full((d, d)),
             full((d, dl)), full((dl, d)), full((d, da)), full((da, d)),
             full((d, dg)), full((dg, d)), full((8, d)),
             full((d, LANES)), full((LANES, d)), full((tm, tm))]
    if has_vres:
        dv = p['v1'].shape[1]
        ins += [v_first, p['v1'], p['v2']]
        specs += [row, full((d, dv)), full((dv, d))]
    act = jax.ShapeDtypeStruct((bsz, t, d), F32)
    nch = t // CHUNK
    outs = pl.pallas_call(
        functools.partial(_rwkv_pre_body, has_vres),
        grid=(bsz, t // tm),
        in_specs=specs,
        out_specs=[row] * 6 + [pl.BlockSpec((1, tm // CHUNK, 1, d), lambda b, i: (b, i, 0, 0))],
        out_shape=[act] * 6 + [jax.ShapeDtypeStruct((bsz, nch, 1, d), F32)],
        compiler_params=_cparams(("parallel", "parallel")),
        name="rwkv_pre",
    )(*ins)
    return outs


_BNN = (((2,), (1,)), ((0,), (0,)))
_BNT = (((2,), (2,)), ((0,), (0,)))


def _rwkv_scan_body(r_ref, k_ref, a_ref, b_ref, v_ref, gl_ref, vec_ref, o_ref, s_ref):
    c = pl.program_id(1)

    @pl.when(c == 0)
    def _():
        s_ref[...] = jnp.zeros_like(s_ref)

    L = CHUNK
    NP = RWKV_HEADS // 2
    shp = (NP, L, LANES)
    lane = lax.broadcasted_iota(jnp.int32, shp, 2)
    h0 = lane < RWKV_HEAD
    tt = lax.broadcasted_iota(jnp.int32, shp, 1)
    ss = lane & (RWKV_HEAD - 1)
    strict = ss < tt
    incl = ss <= tt
    eye = jnp.where(ss == tt, 1.0, 0.0)
    lvl0 = strict & ((ss >> 4) == (tt >> 4))
    lvl1 = strict & ((ss >> 5) == (tt >> 5)) & ((ss >> 4) != (tt >> 4))
    lvl2 = (ss >> 5) != (tt >> 5)
    ri = lax.broadcasted_iota(jnp.int32, (NP, LANES, LANES), 1)
    ci = lax.broadcasted_iota(jnp.int32, (NP, LANES, LANES), 2)
    blockdiag = (ri >> 6) == (ci >> 6)
    r2 = lax.broadcasted_iota(jnp.int32, (LANES, LANES), 0)
    c2 = lax.broadcasted_iota(jnp.int32, (LANES, LANES), 1)
    seg_mean = jnp.where((r2 >> 6) == (c2 >> 6), 1.0, 0.0).astype(BF16)

    def grp(ref):
        return jnp.stack([ref[0, :, hp * LANES:(hp + 1) * LANES] for hp in range(NP)])

    def bd(x):
        return jnp.concatenate([jnp.where(h0, x, 0.0), jnp.where(h0, 0.0, x)], axis=1)

    def pmm(xp, y):
        return _dot1(xp, bd(y), _BNN)

    R, K, A, Bv, V = grp(r_ref), grp(k_ref), grp(a_ref), grp(b_ref), grp(v_ref)
    gl = jnp.stack([gl_ref[0, 0, :, hp * LANES:(hp + 1) * LANES] for hp in range(NP)])
    S = s_ref[...]

    AR = jnp.concatenate([A, R], axis=1)
    BK = jnp.concatenate([bd(Bv), bd(K)], axis=1)
    G = _dot1(AR, BK, _BNT)
    A_ab = jnp.where(strict, G[:, 0:L, 0:LANES], 0.0)
    A_ak = jnp.where(strict, G[:, 0:L, LANES:2 * LANES], 0.0)
    A_rb = jnp.where(incl, G[:, L:2 * L, 0:LANES], 0.0)
    A_rk = jnp.where(incl, G[:, L:2 * L, LANES:2 * LANES], 0.0)

    a0 = jnp.where(lvl0, A_ab, 0.0)
    Tm = eye + a0
    P = a0
    for _ in range(3):
        P = pmm(P, P)
        Tm = Tm + pmm(Tm, P)
    for lvl in (lvl1, lvl2):
        Tm = Tm + pmm(pmm(Tm, jnp.where(lvl, A_ab, 0.0)), Tm)

    PQ = _dot1(AR, S, _BNT)
    W = PQ[:, 0:L] + pmm(A_ak, V)
    U = pmm(Tm, W)
    Y = PQ[:, L:2 * L] + _dot1(jnp.concatenate([A_rb, A_rk], axis=2),
                               jnp.concatenate([bd(U), bd(V)], axis=1), _BNN)
    UV = jnp.concatenate([U, V], axis=1)
    UVt = jnp.stack([UV[hp].T for hp in range(NP)])
    BK2 = jnp.concatenate([Bv, K], axis=1)
    upd = _dot1(UVt, BK2, _BNN)
    s_ref[...] = (S + jnp.where(blockdiag, upd, 0.0)) * gl

    vrow = lambda n: jnp.stack([vec_ref[n:n + 1, hp * LANES:(hp + 1) * LANES] for hp in range(NP)])
    inv_n = 1.0 / RWKV_HEAD
    flat = lambda x: x.reshape(NP * L, LANES)
    mean = _dot_exact_rhs(flat(Y), seg_mean).reshape(shp) * inv_n
    yc = Y - mean
    var = _dot_exact_rhs(flat(yc * yc), seg_mean).reshape(shp) * inv_n
    yn = yc * lax.rsqrt(var + GN_EPS) * vrow(1) + vrow(2)
    bonus = _dot_exact_rhs(flat(R * K * vrow(0)), seg_mean).reshape(shp)
    out = yn + bonus * V
    for hp in range(NP):
        o_ref[0, :, hp * LANES:(hp + 1) * LANES] = out[hp]


def _rwkv_scan(r, k, a, b, v, gl, r_k, lnx_g, lnx_b):
    bsz, t, d = r.shape
    row = pl.BlockSpec((1, CHUNK, d), lambda bb, c: (bb, c, 0))
    vec = jnp.stack([r_k.reshape(d), lnx_g, lnx_b] + [jnp.zeros((d,), F32)] * 5)
    return pl.pallas_call(
        _rwkv_scan_body,
        grid=(bsz, t // CHUNK),
        in_specs=[row] * 5 + [
            pl.BlockSpec((1, 1, 1, d), lambda bb, c: (bb, c, 0, 0)),
            pl.BlockSpec((8, d), lambda bb, c: (0, 0)),
        ],
        out_specs=row,
        out_shape=jax.ShapeDtypeStruct((bsz, t, d), F32),
        scratch_shapes=[pltpu.VMEM((RWKV_HEADS // 2, LANES, LANES), F32)],
        compiler_params=_cparams(("parallel", "arbitrary")),
        name="rwkv_scan",
    )(r, k, a, b, v, gl, vec)


IDX_COLS = 768


def _dsa_proj_body(x_ref, mod_ref, wq_ref, wc_ref, wih_ref, wil_ref, kvn_ref,
                   q_o, ckv_o, qi_o, ki_o, wi_o):
    hin = x_ref[0] * (1.0 + mod_ref[0, 1:2, :]) + mod_ref[0, 0:1, :]
    hh, hl = _split2(hin)
    q_o[0] = _mm(hh, wq_ref[...])
    ckv = _mm(hh, wc_ref[...])
    ms = jnp.mean(ckv * ckv, axis=-1, keepdims=True)
    ckv_o[0] = ckv * lax.rsqrt(ms + 1e-6) * kvn_ref[...]
    idx = _mm(hh, wih_ref[...]) + (_mm(hh, wil_ref[...]) + _mm(hl, wih_ref[...]))
    nq = IDX_HEADS * IDX_DIM
    qi_o[0] = idx[:, 0:nq]
    ki_o[0] = idx[:, nq:nq + LANES]
    wi_o[0] = idx[:, nq + LANES:nq + 2 * LANES] * (IDX_HEADS ** -0.5 * IDX_DIM ** -0.5)


def _dsa_proj(x, mod, w_in, kv_norm):
    bsz, t, d = x.shape
    c1 = ATT_HEADS * ATT_HEAD_DIM
    c2 = c1 + KV_LATENT
    c3 = c2 + IDX_HEADS * IDX_DIM
    c4 = c3 + IDX_DIM
    wq = w_in[:, :c1].astype(BF16)
    wc = w_in[:, c1:c2].astype(BF16)
    pad = IDX_COLS - (c3 - c2) - 2 * IDX_DIM - IDX_HEADS
    widx = jnp.concatenate([w_in[:, c2:c3], w_in[:, c3:c4], w_in[:, c3:c4], w_in[:, c4:],
                            jnp.zeros((d, pad), F32)], axis=1)
    wih = widx.astype(BF16)
    wil = (widx - wih.astype(F32)).astype(BF16)
    tm = ROW_TM

    def full(shape):
        return pl.BlockSpec(shape, lambda b, i: (0,) * len(shape))

    def row(n):
        return pl.BlockSpec((1, tm, n), lambda b, i: (b, i, 0))

    def act(n):
        return jax.ShapeDtypeStruct((bsz, t, n), F32)

    nq = IDX_HEADS * IDX_DIM
    return pl.pallas_call(
        _dsa_proj_body,
        grid=(bsz, t // tm),
        in_specs=[row(d), pl.BlockSpec((1, 6, d), lambda b, i: (b, 0, 0)),
                  full((d, c1)), full((d, KV_LATENT)), full((d, IDX_COLS)), full((d, IDX_COLS)),
                  full((1, KV_LATENT))],
        out_specs=[row(c1), row(KV_LATENT), row(nq), row(LANES), row(LANES)],
        out_shape=[act(c1), act(KV_LATENT), act(nq), act(LANES), act(LANES)],
        compiler_params=_cparams(("parallel", "parallel")),
        name="dsa_proj",
    )(x, mod, wq, wc, wih, wil, kv_norm.reshape(1, KV_LATENT))


def _dsa_index_body(k_sel, qi_ref, wi_ref, ki_ref, o_ref, key_ref):
    i = pl.program_id(1)
    tq = qi_ref.shape[1]
    t = ki_ref.shape[1]
    nchunk = t // tq
    wt = wi_ref[0].T
    lane = lax.broadcasted_iota(jnp.int32, (tq, LANES), 1)
    first = lane < IDX_DIM
    qsplit = []
    for hp in range(IDX_HEADS // 2):
        qp = qi_ref[0, :, hp * LANES:(hp + 1) * LANES]
        qsplit.append(_split2(jnp.where(first, qp, 0.0)))
        qsplit.append(_split2(jnp.where(first, 0.0, qp)))
    krow = lax.broadcasted_iota(jnp.int32, (tq, tq), 0)
    qpos = i * tq + lax.broadcasted_iota(jnp.int32, (tq, tq), 1)

    def chunk_start(c):
        return pl.multiple_of(c * tq, tq)

    def score_chunk(c, carry):
        k0 = chunk_start(c)
        kh, kl = _split2(ki_ref[0, pl.ds(k0, tq), :])
        score = jnp.zeros((tq, tq), F32)
        for h, (qh, ql) in enumerate(qsplit):
            s = _mm(kh, qh, _NT) + (_mm(kh, ql, _NT) + _mm(kl, qh, _NT))
            score = score + wt[h:h + 1, :] * jnp.maximum(s, 0.0)
        score = jnp.where(score == 0.0, 0.0, score)
        bits = pltpu.bitcast(score, jnp.int32)
        skey = bits ^ ((bits >> 31) & 0x7FFFFFFF)
        key_ref[pl.ds(k0, tq), :] = jnp.where(k0 + krow <= qpos, skey, INT_MIN)
        return carry

    lax.fori_loop(0, i + 1, score_chunk, 0)

    def count(fn):
        def body(c, acc):
            k0 = chunk_start(c)
            ind = fn(key_ref[pl.ds(k0, tq), :], k0 + krow)
            return acc + jnp.sum(ind.reshape(tq // 8, 8, tq), axis=0)
        acc = lax.fori_loop(0, i + 1, body, jnp.zeros((8, tq), F32))
        return jnp.sum(acc, axis=0, keepdims=True)

    kf = float(k_sel)

    def count_ge(cand):
        return count(lambda keys, kpos: jnp.where(keys >= cand, 1.0, 0.0))

    thr0 = jnp.where(count_ge(jnp.zeros((1, tq), jnp.int32)) >= kf, 0, INT_MIN).astype(jnp.int32)

    def thr_step(n, thr):
        cand = thr | jnp.left_shift(jnp.int32(1), 30 - n)
        return jnp.where(count_ge(cand) >= kf, cand, thr)

    thr = lax.fori_loop(0, 31, thr_step, thr0)
    n_gt = count(lambda keys, kpos: jnp.where(keys > thr, 1.0, 0.0))
    n_eq = count(lambda keys, kpos: jnp.where(keys == thr, 1.0, 0.0))
    need = kf - n_gt
    nbits = int(t - 1).bit_length()

    def tie_cut():
        def cut_step(n, cut):
            cand = cut | jnp.left_shift(jnp.int32(1), nbits - 1 - n)
            cnt = count(lambda keys, kpos: jnp.where(keys == thr, jnp.where(kpos < cand, 1.0, 0.0), 0.0))
            return jnp.where(cnt < need, cand, cut)
        return lax.fori_loop(0, nbits, cut_step, jnp.zeros((1, tq), jnp.int32))

    cut = lax.cond(jnp.max(n_eq - need) > 0.0, tie_cut, lambda: jnp.full((1, tq), t, jnp.int32))

    def write_chunk(c, carry):
        k0 = chunk_start(c)
        keys = key_ref[pl.ds(k0, tq), :]
        kpos = k0 + krow
        tie = jnp.where(keys == thr, jnp.where(kpos <= cut, 0.0, MASK_NEG), MASK_NEG)
        bias = jnp.where(keys > thr, 0.0, tie)
        o_ref[0, pl.ds(k0, tq), :] = jnp.where(kpos <= qpos, bias, MASK_NEG).astype(BF16)
        return carry

    lax.fori_loop(0, i + 1, write_chunk, 0)

    def masked_chunk(c, carry):
        o_ref[0, pl.ds(chunk_start(c), tq), :] = jnp.full((tq, tq), MASK_NEG, BF16)
        return carry

    lax.fori_loop(i + 1, nchunk, masked_chunk, 0)


def _dsa_index(qi, ki, wi, k_sel):
    bsz, t, nq = qi.shape
    tq = ATT_T
    return pl.pallas_call(
        functools.partial(_dsa_index_body, k_sel),
        grid=(bsz, t // tq),
        in_specs=[pl.BlockSpec((1, tq, nq), lambda b, i: (b, i, 0)),
                  pl.BlockSpec((1, tq, LANES), lambda b, i: (b, i, 0)),
                  pl.BlockSpec((1, t, LANES), lambda b, i: (b, 0, 0))],
        out_specs=pl.BlockSpec((1, t, tq), lambda b, i: (b, 0, i)),
        out_shape=jax.ShapeDtypeStruct((bsz, t, t), BF16),
        scratch_shapes=[pltpu.VMEM((t, tq), jnp.int32)],
        compiler_params=_cparams(("parallel", "parallel")),
        name="dsa_index",
    )(qi, wi, ki)


def _t5_bucket_np(n):
    n = np.maximum(n, 0)
    max_exact = REL_BUCKETS // 2
    nf = np.maximum(n, 1).astype(np.float32)
    large = max_exact + (np.log(nf / np.float32(max_exact)) / np.float32(math.log(REL_MAX_DIST / max_exact))
                         * np.float32(REL_BUCKETS - max_exact)).astype(np.int32)
    large = np.minimum(large, REL_BUCKETS - 1)
    return np.where(n < max_exact, n, large).astype(np.int32)


def _band_body(bkt_ref, rb_ref, o_ref):
    h = pl.program_id(1)
    bkt = bkt_ref[0]
    acc = jnp.zeros(bkt.shape, F32)
    for b in range(REL_BUCKETS):
        acc = jnp.where(bkt == b, rb_ref[b, h], acc)
    o_ref[0] = acc * LOG2E


def _band_bias(rel_bias):
    tt = ATT_T
    kc = np.arange(tt)[:, None]
    qr = np.arange(tt)[None, :]
    planes = [_t5_bucket_np(d * tt + qr - kc) for d in range(3)]
    assert (planes[2] == REL_BUCKETS - 1).all() and tt + 1 >= 113
    bkt = jnp.asarray(np.stack(planes))
    return pl.pallas_call(
        _band_body,
        grid=(3, ATT_HEADS),
        in_specs=[pl.BlockSpec((1, tt, tt), lambda d, h: (d, 0, 0)),
                  pl.BlockSpec(memory_space=pltpu.SMEM)],
        out_specs=pl.BlockSpec((1, tt, tt), lambda d, h: (d, 0, h)),
        out_shape=jax.ShapeDtypeStruct((3, tt, ATT_HEADS * tt), F32),
        compiler_params=_cparams(("parallel", "parallel")),
        name="band_bias",
    )(bkt, rel_bias)


LOG2E = 1.4426950408889634
ACC_ROWS = KV_LATENT + 16


def _dsa_attn_body(q_ref, ckv_ref, mask_ref, wuk_ref, wuv_ref, near_ref, far_ref, o_ref,
                   ql_ref, m_ref, acc_ref, p_ref, ot_ref):
    i = pl.program_id(1)
    j = pl.program_id(2)
    nh = ATT_HEADS
    tq = q_ref.shape[1]
    qscale = ATT_HEAD_DIM ** -0.5 * LOG2E

    @pl.when(j == 0)
    def _():
        for hp in range(nh // 2):
            qp = q_ref[0, :, hp * LANES:(hp + 1) * LANES].astype(BF16)
            qlat = _mm(qp, wuk_ref[hp]) * qscale
            ql_ref[2 * hp * tq:(2 * hp + 1) * tq, :] = qlat[:, 0:KV_LATENT].astype(BF16)
            ql_ref[(2 * hp + 1) * tq:(2 * hp + 2) * tq, :] = qlat[:, KV_LATENT:2 * KV_LATENT].astype(BF16)
        m_ref[...] = jnp.full(m_ref.shape, MASK_NEG, F32)
        acc_ref[...] = jnp.zeros_like(acc_ref)

    def step(near):
        ckv = ckv_ref[0]
        tk = ckv.shape[0]
        s_all = _mm(ckv.astype(BF16), ql_ref[...], _NT)
        ckv_aug = jnp.concatenate([ckv.T, jnp.ones((ACC_ROWS - KV_LATENT, tk), F32)], axis=0).astype(BF16)
        maskb = mask_ref[0].astype(F32)
        m_prev = m_ref[...]
        for h in range(nh):
            hs = slice(h * tq, (h + 1) * tq)
            s = s_all[:, hs] + maskb
            if near:
                s = s + near_ref[i - j, :, hs]
                m_new = jnp.maximum(m_prev[:, hs], jnp.max(s, axis=0, keepdims=True))
                shift = m_new
            else:
                cvec = far_ref[:, hs]
                m_new = jnp.maximum(m_prev[:, hs], jnp.max(s, axis=0, keepdims=True) + cvec)
                shift = m_new - cvec
            p_ref[:, hs] = jnp.exp2(s - shift).astype(BF16)
            m_ref[:, hs] = m_new
        alpha = jnp.exp2(m_prev - m_ref[...])
        acc_ref[...] = alpha * acc_ref[...] + _mm(ckv_aug, p_ref[...])

    @pl.when((j <= i) & (i - j < 2))
    def _():
        step(True)

    @pl.when(i - j >= 2)
    def _():
        step(False)

    @pl.when(j == i)
    def _():
        def norm(h):
            a = acc_ref[:, h * tq:(h + 1) * tq]
            return a[0:KV_LATENT] * (1.0 / a[KV_LATENT:KV_LATENT + 1])
        for hp in range(nh // 2):
            olat = jnp.concatenate([norm(2 * hp), norm(2 * hp + 1)], axis=0).astype(BF16)
            ot_ref[hp * LANES:(hp + 1) * LANES, :] = _mm(wuv_ref[hp], olat)
        o_ref[0] = ot_ref[...].T


def _dsa_attn(q, ckv, maskt, w_uk, w_uv, band):
    bsz, t, d = q.shape
    tt = ATT_T
    nt = t // tt
    nh = ATT_HEADS
    zk = jnp.zeros((nh // 2, ATT_HEAD_DIM, KV_LATENT), F32)
    wuk2 = jnp.concatenate([jnp.concatenate([w_uk[0::2], zk], axis=2),
                            jnp.concatenate([zk, w_uk[1::2]], axis=2)], axis=1).astype(BF16)
    wuv_t = jnp.swapaxes(w_uv, 1, 2)
    zv = jnp.zeros((nh // 2, ATT_HEAD_DIM, KV_LATENT), F32)
    wuv2 = jnp.concatenate([jnp.concatenate([wuv_t[0::2], zv], axis=2),
                            jnp.concatenate([zv, wuv_t[1::2]], axis=2)], axis=1).astype(BF16)
    return pl.pallas_call(
        _dsa_attn_body,
        grid=(bsz, nt, nt),
        in_specs=[pl.BlockSpec((1, tt, d), lambda b, i, j: (b, i, 0)),
                  pl.BlockSpec((1, tt, KV_LATENT), lambda b, i, j: (b, jnp.minimum(j, i), 0)),
                  pl.BlockSpec((1, tt, tt), lambda b, i, j: (b, jnp.minimum(j, i), i)),
                  pl.BlockSpec((nh // 2, LANES, 2 * KV_LATENT), lambda b, i, j: (0, 0, 0)),
                  pl.BlockSpec((nh // 2, LANES, 2 * KV_LATENT), lambda b, i, j: (0, 0, 0)),
                  pl.BlockSpec((2, tt, nh * tt), lambda b, i, j: (0, 0, 0)),
                  pl.BlockSpec((1, nh * tt), lambda b, i, j: (0, 0))],
        out_specs=pl.BlockSpec((1, tt, d), lambda b, i, j: (b, i, 0)),
        out_shape=jax.ShapeDtypeStruct((bsz, t, d), F32),
        scratch_shapes=[pltpu.VMEM((nh * tt, KV_LATENT), BF16),
                        pltpu.VMEM((1, nh * tt), F32),
                        pltpu.VMEM((ACC_ROWS, nh * tt), F32),
                        pltpu.VMEM((tt, nh * tt), BF16),
                        pltpu.VMEM((d, tt), F32)],
        compiler_params=_cparams(("parallel", "parallel", "arbitrary")),
        name="dsa_attn",
    )(q, ckv, maskt, wuk2, wuv2, band[:2], band[2, 0:1, :])


def kernel(x, c, ada_w, ada_b, ln_g, ln_b, ffn_w_in, ffn_w_out, rwkv_mu, rwkv_w_rkv, rwkv_w0, rwkv_w1, rwkv_w2, rwkv_a0, rwkv_a1, rwkv_a2, rwkv_v0, rwkv_v1, rwkv_v2, rwkv_g1, rwkv_g2, rwkv_k_k, rwkv_k_a, rwkv_r_k, rwkv_lnx_g, rwkv_lnx_b, rwkv_w_out, dsa_w_in, dsa_kv_norm, dsa_w_uk, dsa_w_uv, dsa_w_out, rel_bias):
    bsz, t, d = x.shape
    assert d == D_MODEL and t % ATT_T == 0 and t % FFN_TM == 0 and t % PRE_TM == 0
    mod_all = _adaln(c, ada_w, ada_b).reshape(DEPTH, bsz, 6, d)
    band = _band_bias(rel_bias)
    k_sel = min(TOPK_MAX, t // TOPK_DIV)
    bf = lambda w: w.astype(BF16)
    v_first = None
    for i in range(DEPTH):
        mod = mod_all[i]
        j = i // 2
        if i % 2 == 0:
            p = dict(mu=rwkv_mu[j], wr=bf(rwkv_w_rkv[j, 0]), wk=bf(rwkv_w_rkv[j, 1]), wv=bf(rwkv_w_rkv[j, 2]),
                     w0=rwkv_w0[j], w1=bf(rwkv_w1[j]), w2=bf(rwkv_w2[j]),
                     a0=rwkv_a0[j], a1=bf(rwkv_a1[j]), a2=bf(rwkv_a2[j]),
                     g1=bf(rwkv_g1[j]), g2=bf(rwkv_g2[j]), k_k=rwkv_k_k[j], k_a=rwkv_k_a[j])
            if j > 0:
                p.update(v0=rwkv_v0[j - 1], v1=bf(rwkv_v1[j - 1]), v2=bf(rwkv_v2[j - 1]))
            r_s, k_s, a_s, b_s, v, g, gl = _rwkv_pre(x, mod, p, v_first if j > 0 else None)
            if j == 0:
                v_first = v
            z = _rwkv_scan(r_s, k_s, a_s, b_s, v, gl, rwkv_r_k[j], rwkv_lnx_g[j], rwkv_lnx_b[j])
            x = _post(z, g, x, mod, bf(rwkv_w_out[j]), ln_g[i, 0], ln_b[i, 0])
        else:
            q, ckv, qi, ki, wi = _dsa_proj(x, mod, dsa_w_in[j], dsa_kv_norm[j])
            maskt = _dsa_index(qi, ki, wi, k_sel)
            o = _dsa_attn(q, ckv, maskt, dsa_w_uk[j], dsa_w_uv[j], band)
            x = _post(o, None, x, mod, bf(dsa_w_out[j]), ln_g[i, 0], ln_b[i, 0])
        x = _ffn(x, mod, bf(ffn_w_in[i]), bf(ffn_w_out[i]), ln_g[i, 1], ln_b[i, 1])
    return x
```

```python
import functools
import math

import numpy as np
import jax
import jax.numpy as jnp
from jax import lax
from jax.experimental import pallas as pl
from jax.experimental.pallas import tpu as pltpu

F32 = jnp.float32
BF16 = jnp.bfloat16

D_MODEL = 1024
DEPTH = 4
RWKV_HEAD = 64
RWKV_HEADS = D_MODEL // RWKV_HEAD
GN_EPS = RWKV_HEAD * 1e-5
ATT_HEADS = 16
ATT_HEAD_DIM = 64
KV_LATENT = 128
IDX_HEADS = 8
IDX_DIM = 64
TOPK_MAX = 256
TOPK_DIV = 4
REL_BUCKETS = 32
REL_MAX_DIST = 128
FFN_HIDDEN = 2816
DEEPNORM_ALPHA = (2 * DEPTH) ** 0.25
LN_EPS = 1e-5

LANES = 128
CHUNK = 64
PRE_TM = 256
ROW_TM = 256
FFN_TM = 512
FFN_TF = 1408
ATT_T = 256
MASK_NEG = -1e30
VMEM_LIMIT = 56 * 1024 * 1024
INT_MIN = -2 ** 31


def _cparams(sem):
    return pltpu.CompilerParams(dimension_semantics=sem, vmem_limit_bytes=VMEM_LIMIT)


def _split2(x):
    hi = x.astype(BF16)
    lo = (x - hi.astype(F32)).astype(BF16)
    return hi, lo


def _split3(x):
    hi = x.astype(BF16)
    r1 = x - hi.astype(F32)
    mid = r1.astype(BF16)
    lo = (r1 - mid.astype(F32)).astype(BF16)
    return hi, mid, lo


_NN = (((1,), (0,)), ((), ()))
_NT = (((1,), (1,)), ((), ()))


def _mm(a, b, dims=_NN):
    return lax.dot_general(a, b, dims, preferred_element_type=F32)


def _dot3(a, b, dims=_NN):
    ah, al = _split2(a)
    bh, bl = _split2(b)
    return _mm(ah, bh, dims) + (_mm(ah, bl, dims) + _mm(al, bh, dims))


def _dot1(a, b, dims=_NN):
    return _mm(a.astype(BF16), b.astype(BF16), dims)


def _dot_hilo_rhs(a, b_bf16):
    h, l = _split2(a)
    return _mm(h, b_bf16) + _mm(l, b_bf16)


def _sigmoid(x):
    return 1.0 / (1.0 + jnp.exp(-x))


def _layernorm(xr, g, b):
    mu = jnp.mean(xr, axis=-1, keepdims=True)
    xc = xr - mu
    var = jnp.mean(xc * xc, axis=-1, keepdims=True)
    return xc * lax.rsqrt(var + LN_EPS) * g + b


def _adaln_body(c_ref, w_ref, b_ref, o_ref):
    c = c_ref[...]
    cond = c * _sigmoid(c)
    o_ref[0] = _dot3(cond, w_ref[0]) + b_ref[0]


def _adaln(c, ada_w, ada_b):
    depth, d, n = ada_w.shape
    bsz = c.shape[0]
    tn = n // 4
    return pl.pallas_call(
        _adaln_body,
        grid=(depth, n // tn),
        in_specs=[
            pl.BlockSpec((bsz, d), lambda i, j: (0, 0)),
            pl.BlockSpec((1, d, tn), lambda i, j: (i, 0, j)),
            pl.BlockSpec((1, 1, tn), lambda i, j: (i, 0, j)),
        ],
        out_specs=pl.BlockSpec((1, bsz, tn), lambda i, j: (i, 0, j)),
        out_shape=jax.ShapeDtypeStruct((depth, bsz, n), F32),
        compiler_params=_cparams(("parallel", "parallel")),
        name="adaln",
    )(c, ada_w, ada_b.reshape(depth, 1, n))


def _ffn_body(x_ref, mod_ref, wg_ref, wu_ref, wo_ref, lng_ref, lnb_ref, o_ref, hin_ref, acc_ref):
    j = pl.program_id(2)

    @pl.when(j == 0)
    def _():
        x = x_ref[0]
        hin_ref[...] = (x * (1.0 + mod_ref[0, 4:5, :]) + mod_ref[0, 3:4, :]).astype(BF16)
        acc_ref[...] = jnp.zeros_like(acc_ref)

    hin = hin_ref[...]
    gate = _mm(hin, wg_ref[...])
    up = _mm(hin, wu_ref[...])
    hid = (gate * _sigmoid(gate) * up).astype(BF16)
    acc_ref[...] += _mm(hid, wo_ref[...])

    @pl.when(j == pl.num_programs(2) - 1)
    def _():
        res = DEEPNORM_ALPHA * x_ref[0] + (1.0 + mod_ref[0, 5:6, :]) * acc_ref[...]
        o_ref[0] = _layernorm(res, lng_ref[...], lnb_ref[...])


def _ffn(x, mod, w_in, w_out, ln_g, ln_b):
    bsz, t, d = x.shape
    f = w_out.shape[0]
    nf = f // FFN_TF
    return pl.pallas_call(
        _ffn_body,
        grid=(bsz, t // FFN_TM, nf),
        in_specs=[
            pl.BlockSpec((1, FFN_TM, d), lambda b, i, j: (b, i, 0)),
            pl.BlockSpec((1, 6, d), lambda b, i, j: (b, 0, 0)),
            pl.BlockSpec((d, FFN_TF), lambda b, i, j: (0, j)),
            pl.BlockSpec((d, FFN_TF), lambda b, i, j: (0, nf + j)),
            pl.BlockSpec((FFN_TF, d), lambda b, i, j: (j, 0)),
            pl.BlockSpec((1, d), lambda b, i, j: (0, 0)),
            pl.BlockSpec((1, d), lambda b, i, j: (0, 0)),
        ],
        out_specs=pl.BlockSpec((1, FFN_TM, d), lambda b, i, j: (b, i, 0)),
        out_shape=jax.ShapeDtypeStruct((bsz, t, d), F32),
        scratch_shapes=[pltpu.VMEM((FFN_TM, d), BF16), pltpu.VMEM((FFN_TM, d), F32)],
        compiler_params=_cparams(("parallel", "parallel", "arbitrary")),
        name="ffn",
    )(x, mod, w_in, w_in, w_out, ln_g.reshape(1, d), ln_b.reshape(1, d))


def _post_body(gated, *refs):
    if gated:
        y_ref, g_ref, x_ref, mod_ref, w_ref, lng_ref, lnb_ref, o_ref = refs
        y = (y_ref[0] * g_ref[0]).astype(BF16)
    else:
        y_ref, x_ref, mod_ref, w_ref, lng_ref, lnb_ref, o_ref = refs
        y = y_ref[0].astype(BF16)
    out = _mm(y, w_ref[...])
    res = DEEPNORM_ALPHA * x_ref[0] + (1.0 + mod_ref[0, 2:3, :]) * out
    o_ref[0] = _layernorm(res, lng_ref[...], lnb_ref[...])


def _post(y, g, x, mod, w, ln_g, ln_b):
    bsz, t, d = x.shape
    row = pl.BlockSpec((1, ROW_TM, d), lambda b, i: (b, i, 0))
    vec = pl.BlockSpec((1, d), lambda b, i: (0, 0))
    gated = g is not None
    acts = [y, g] if gated else [y]
    return pl.pallas_call(
        functools.partial(_post_body, gated),
        grid=(bsz, t // ROW_TM),
        in_specs=[row] * len(acts) + [
            row,
            pl.BlockSpec((1, 6, d), lambda b, i: (b, 0, 0)),
            pl.BlockSpec((d, d), lambda b, i: (0, 0)),
            vec, vec,
        ],
        out_specs=row,
        out_shape=jax.ShapeDtypeStruct((bsz, t, d), F32),
        compiler_params=_cparams(("parallel", "parallel")),
        name="post",
    )(*acts, x, mod, w, ln_g.reshape(1, d), ln_b.reshape(1, d))


def _rwkv_pre_body(has_vres, *refs):
    if has_vres:
        (x_ref, xp_ref, mod_ref, mu_ref, wr_ref, wk_ref, wv_ref, w1_ref, w2_ref, a1_ref, a2_ref,
         g1_ref, g2_ref, vec_ref, seg_ref, segt_ref, tri_ref, vf_ref, v1_ref, v2_ref,
         r_o, k_o, a_o, b_o, v_o, g_o, gl_o) = refs
    else:
        (x_ref, xp_ref, mod_ref, mu_ref, wr_ref, wk_ref, wv_ref, w1_ref, w2_ref, a1_ref, a2_ref,
         g1_ref, g2_ref, vec_ref, seg_ref, segt_ref, tri_ref,
         r_o, k_o, a_o, b_o, v_o, g_o, gl_o) = refs
    i = pl.program_id(1)
    sc = 1.0 + mod_ref[0, 1:2, :]
    sh = mod_ref[0, 0:1, :]
    hin = x_ref[0] * sc + sh
    tm = hin.shape[0]
    prev_row = xp_ref[0, 7:8, :] * sc + sh
    prev_row = jnp.where(i == 0, 0.0, prev_row)
    rows = lax.broadcasted_iota(jnp.int32, hin.shape, 0)
    hprev = jnp.where(rows == 0, prev_row, pltpu.roll(hin, 1, 0))
    xx = hprev - hin

    def mix(p):
        return hin + xx * mu_ref[p:p + 1, :]

    xr, xk, xv = mix(0).astype(BF16), mix(1).astype(BF16), mix(2).astype(BF16)
    xw, xa, xg = mix(3).astype(BF16), mix(4).astype(BF16), mix(5).astype(BF16)
    w0, a0, kkw, kaw = vec_ref[0:1, :], vec_ref[1:2, :], vec_ref[2:3, :], vec_ref[3:4, :]

    r = _mm(xr, wr_ref[...])
    k = _mm(xk, wk_ref[...])
    v = _mm(xv, wv_ref[...])

    wl = w0 + _mm(jnp.tanh(_mm(xw, w1_ref[...])).astype(BF16), w2_ref[...])
    nz = -wl
    softplus = jnp.maximum(nz, 0.0) + jnp.log(1.0 + jnp.exp(-jnp.abs(nz)))
    logdec = -jnp.exp(-softplus - 0.5)

    a = _sigmoid(a0 + _mm(_mm(xa, a1_ref[...]).astype(BF16), a2_ref[...]))
    if has_vres:
        v0 = vec_ref[4:5, :]
        vmix = _sigmoid(v0 + _mm(_mm(xv, v1_ref[...]).astype(BF16), v2_ref[...]))
        v = v + (vf_ref[0] - v) * vmix
    g = _mm(_sigmoid(_mm(xg, g1_ref[...])).astype(BF16), g2_ref[...])

    kk = k * kkw
    ss = _dot_hilo_rhs(kk * kk, seg_ref[...])
    nrm = jnp.maximum(jnp.sqrt(ss), 1e-12)
    inv = _dot_hilo_rhs(1.0 / nrm, segt_ref[...])
    kk = kk * inv
    k = k * (1.0 + (a - 1.0) * kaw)

    cum = _dot_exact_rhs_lhs(tri_ref[...], logdec)
    ginc = jnp.exp(cum)
    ginv = jnp.exp(-cum)
    gprev = jnp.exp(cum - logdec)

    r_o[0] = r * ginc
    k_o[0] = k * ginv
    a_o[0] = -kk * gprev
    b_o[0] = kk * a * ginv
    v_o[0] = v
    g_o[0] = g
    for cc in range(tm // CHUNK):
        gl_o[0, cc] = ginc[cc * CHUNK + CHUNK - 1:cc * CHUNK + CHUNK, :]


def _dot_exact_rhs_lhs(m_bf16, x):
    h, mid, l = _split3(x)
    return _mm(m_bf16, h) + (_mm(m_bf16, mid) + _mm(m_bf16, l))


def _rwkv_pre(x, mod, p, v_first):
    bsz, t, d = x.shape
    tm = PRE_TM
    has_vres = v_first is not None
    row = pl.BlockSpec((1, tm, d), lambda b, i: (b, i, 0))

    def full(shape):
        return pl.BlockSpec(shape, lambda b, i: (0,) * len(shape))

    heads = d // RWKV_HEAD
    seg = np.zeros((d, LANES), np.float32)
    seg[np.arange(d), np.arange(d) // RWKV_HEAD] = 1.0
    tri = np.zeros((tm, tm), np.float32)
    idx = np.arange(tm)
    tri[(idx[:, None] >= idx[None, :]) & (idx[:, None] // CHUNK == idx[None, :] // CHUNK)] = 1.0
    vec_rows = [p['w0'], p['a0'], p['k_k'], p['k_a']] + ([p['v0']] if has_vres else [])
    vec = jnp.stack(vec_rows + [jnp.zeros_like(p['w0'])] * (8 - len(vec_rows)))
    dl, da, dg = p['w1'].shape[1], p['a1'].shape[1], p['g1'].shape[1]
    ins = [x, x, mod, p['mu'], p['wr'], p['wk'], p['wv'], p['w1'], p['w2'], p['a1'], p['a2'],
           p['g1'], p['g2'], vec, jnp.asarray(seg, BF16), jnp.asarray(seg.T, BF16),
           jnp.asarray(tri, BF16)]
    specs = [row,
             pl.BlockSpec((1, 8, d), lambda b, i: (b, jnp.maximum(i * (tm // 8) - 1, 0), 0)),
             pl.BlockSpec((1, 6, d), lambda b, i: (b, 0, 0)),
             full((6, d)), full((d, d)), full((d, d)), full((d, d)),
             full((d, dl)), full((dl, d)), full((d, da)), full((da, d)),
             full((d, dg)), full((dg, d)), full((8, d)),
             full((d, LANES)), full((LANES, d)), full((tm, tm))]
    if has_vres:
        dv = p['v1'].shape[1]
        ins += [v_first, p['v1'], p['v2']]
        specs += [row, full((d, dv)), full((dv, d))]
    act = jax.ShapeDtypeStruct((bsz, t, d), F32)
    nch = t // CHUNK
    outs = pl.pallas_call(
        functools.partial(_rwkv_pre_body, has_vres),
        grid=(bsz, t // tm),
        in_specs=specs,
        out_specs=[row] * 6 + [pl.BlockSpec((1, tm // CHUNK, 1, d), lambda b, i: (b, i, 0, 0))],
        out_shape=[act] * 6 + [jax.ShapeDtypeStruct((bsz, nch, 1, d), F32)],
        compiler_params=_cparams(("parallel", "parallel")),
        name="rwkv_pre",
    )(*ins)
    return outs


_BNN = (((2,), (1,)), ((0,), (0,)))
_BNT = (((2,), (2,)), ((0,), (0,)))


def _rwkv_scan_body(r_ref, k_ref, a_ref, b_ref, v_ref, gl_ref, vec_ref, o_ref, s_ref):
    c = pl.program_id(1)

    @pl.when(c == 0)
    def _():
        s_ref[...] = jnp.zeros_like(s_ref)

    L = CHUNK
    NP = RWKV_HEADS // 2
    shp = (NP, L, LANES)
    lane = lax.broadcasted_iota(jnp.int32, shp, 2)
    h0 = lane < RWKV_HEAD
    tt = lax.broadcasted_iota(jnp.int32, shp, 1)
    ss = lane & (RWKV_HEAD - 1)
    strict = ss < tt
    incl = ss <= tt
    eye = jnp.where(ss == tt, 1.0, 0.0)
    lvl0 = strict & ((ss >> 4) == (tt >> 4))
    lvl1 = strict & ((ss >> 5) == (tt >> 5)) & ((ss >> 4) != (tt >> 4))
    lvl2 = (ss >> 5) != (tt >> 5)
    ri = lax.broadcasted_iota(jnp.int32, (NP, LANES, LANES), 1)
    ci = lax.broadcasted_iota(jnp.int32, (NP, LANES, LANES), 2)
    blockdiag = (ri >> 6) == (ci >> 6)
    r2 = lax.broadcasted_iota(jnp.int32, (LANES, LANES), 0)
    c2 = lax.broadcasted_iota(jnp.int32, (LANES, LANES), 1)
    seg_mean = jnp.where((r2 >> 6) == (c2 >> 6), 1.0, 0.0).astype(BF16)

    def grp(ref):
        return jnp.stack([ref[0, :, hp * LANES:(hp + 1) * LANES] for hp in range(NP)])

    def bd(x):
        return jnp.concatenate([jnp.where(h0, x, 0.0), jnp.where(h0, 0.0, x)], axis=1)

    def pmm(xp, y):
        return _dot1(xp, bd(y), _BNN)

    R, K, A, Bv, V = grp(r_ref), grp(k_ref), grp(a_ref), grp(b_ref), grp(v_ref)
    gl = jnp.stack([gl_ref[0, 0, :, hp * LANES:(hp + 1) * LANES] for hp in range(NP)])
    S = s_ref[...]

    AR = jnp.concatenate([A, R], axis=1)
    BK = jnp.concatenate([bd(Bv), bd(K)], axis=1)
    G = _dot1(AR, BK, _BNT)
    A_ab = jnp.where(strict, G[:, 0:L, 0:LANES], 0.0)
    A_ak = jnp.where(strict, G[:, 0:L, LANES:2 * LANES], 0.0)
    A_rb = jnp.where(incl, G[:, L:2 * L, 0:LANES], 0.0)
    A_rk = jnp.where(incl, G[:, L:2 * L, LANES:2 * LANES], 0.0)

    a0 = jnp.where(lvl0, A_ab, 0.0)
    Tm = eye + a0
    P = a0
    for _ in range(3):
        P = pmm(P, P)
        Tm = Tm + pmm(Tm, P)
    for lvl in (lvl1, lvl2):
        Tm = Tm + pmm(pmm(Tm, jnp.where(lvl, A_ab, 0.0)), Tm)

    PQ = _dot1(AR, S, _BNT)
    W = PQ[:, 0:L] + pmm(A_ak, V)
    U = pmm(Tm, W)
    Y = PQ[:, L:2 * L] + _dot1(jnp.concatenate([A_rb, A_rk], axis=2),
                               jnp.concatenate([bd(U), bd(V)], axis=1), _BNN)
    UV = jnp.concatenate([U, V], axis=1)
    UVt = jnp.stack([UV[hp].T for hp in range(NP)])
    BK2 = jnp.concatenate([Bv, K], axis=1)
    upd = _dot1(UVt, BK2, _BNN)
    s_ref[...] = (S + jnp.where(blockdiag, upd, 0.0)) * gl

    vrow = lambda n: jnp.stack([vec_ref[n:n + 1, hp * LANES:(hp + 1) * LANES] for hp in range(NP)])
    inv_n = 1.0 / RWKV_HEAD
    flat = lambda x: x.reshape(NP * L, LANES)
    mean = _dot_hilo_rhs(flat(Y), seg_mean).reshape(shp) * inv_n
    yc = Y - mean
    var = _dot_hilo_rhs(flat(yc * yc), seg_mean).reshape(shp) * inv_n
    yn = yc * lax.rsqrt(var + GN_EPS) * vrow(1) + vrow(2)
    bonus = _dot_hilo_rhs(flat(R * K * vrow(0)), seg_mean).reshape(shp)
    out = yn + bonus * V
    for hp in range(NP):
        o_ref[0, :, hp * LANES:(hp + 1) * LANES] = out[hp]


def _rwkv_scan(r, k, a, b, v, gl, r_k, lnx_g, lnx_b):
    bsz, t, d = r.shape
    row = pl.BlockSpec((1, CHUNK, d), lambda bb, c: (bb, c, 0))
    vec = jnp.stack([r_k.reshape(d), lnx_g, lnx_b] + [jnp.zeros((d,), F32)] * 5)
    return pl.pallas_call(
        _rwkv_scan_body,
        grid=(bsz, t // CHUNK),
        in_specs=[row] * 5 + [
            pl.BlockSpec((1, 1, 1, d), lambda bb, c: (bb, c, 0, 0)),
            pl.BlockSpec((8, d), lambda bb, c: (0, 0)),
        ],
        out_specs=row,
        out_shape=jax.ShapeDtypeStruct((bsz, t, d), F32),
        scratch_shapes=[pltpu.VMEM((RWKV_HEADS // 2, LANES, LANES), F32)],
        compiler_params=_cparams(("parallel", "arbitrary")),
        name="rwkv_scan",
    )(r, k, a, b, v, gl, vec)


IDX_COLS = 768


def _dsa_proj_body(x_ref, mod_ref, wq_ref, wc_ref, wi_ref, kvn_ref,
                   q_o, ckv_o, qi_o, ki_o, wi_o):
    hin = (x_ref[0] * (1.0 + mod_ref[0, 1:2, :]) + mod_ref[0, 0:1, :]).astype(BF16)
    q_o[0] = _mm(hin, wq_ref[...]).astype(BF16)
    ckv = _mm(hin, wc_ref[...])
    ms = jnp.mean(ckv * ckv, axis=-1, keepdims=True)
    ckv_o[0] = (ckv * lax.rsqrt(ms + 1e-6) * kvn_ref[...]).astype(BF16)
    idx = _mm(hin, wi_ref[...])
    nq = IDX_HEADS * IDX_DIM
    qi_o[0] = idx[:, 0:nq].astype(BF16)
    ki_o[0] = idx[:, nq:nq + LANES].astype(BF16)
    wi_o[0] = idx[:, nq + LANES:nq + 2 * LANES] * (IDX_HEADS ** -0.5 * IDX_DIM ** -0.5)


def _dsa_proj(x, mod, w_in, kv_norm):
    bsz, t, d = x.shape
    c1 = ATT_HEADS * ATT_HEAD_DIM
    c2 = c1 + KV_LATENT
    c3 = c2 + IDX_HEADS * IDX_DIM
    c4 = c3 + IDX_DIM
    wq = w_in[:, :c1].astype(BF16)
    wc = w_in[:, c1:c2].astype(BF16)
    pad = IDX_COLS - (c3 - c2) - 2 * IDX_DIM - IDX_HEADS
    widx = jnp.concatenate([w_in[:, c2:c3], w_in[:, c3:c4], w_in[:, c3:c4], w_in[:, c4:],
                            jnp.zeros((d, pad), F32)], axis=1).astype(BF16)
    tm = ROW_TM

    def full(shape):
        return pl.BlockSpec(shape, lambda b, i: (0,) * len(shape))

    def row(n):
        return pl.BlockSpec((1, tm, n), lambda b, i: (b, i, 0))

    def act(n, dtype):
        return jax.ShapeDtypeStruct((bsz, t, n), dtype)

    nq = IDX_HEADS * IDX_DIM
    return pl.pallas_call(
        _dsa_proj_body,
        grid=(bsz, t // tm),
        in_specs=[row(d), pl.BlockSpec((1, 6, d), lambda b, i: (b, 0, 0)),
                  full((d, c1)), full((d, KV_LATENT)), full((d, IDX_COLS)),
                  full((1, KV_LATENT))],
        out_specs=[row(c1), row(KV_LATENT), row(nq), row(LANES), row(LANES)],
        out_shape=[act(c1, BF16), act(KV_LATENT, BF16), act(nq, BF16), act(LANES, BF16), act(LANES, F32)],
        compiler_params=_cparams(("parallel", "parallel")),
        name="dsa_proj",
    )(x, mod, wq, wc, widx, kv_norm.reshape(1, KV_LATENT))


def _dsa_index_body(k_sel, qi_ref, wi_ref, ki_ref, o_ref, key_ref):
    i = pl.program_id(1)
    tq = qi_ref.shape[1]
    t = ki_ref.shape[1]
    nchunk = t // tq
    wt = wi_ref[0].T
    lane = lax.broadcasted_iota(jnp.int32, (tq, LANES), 1)
    first = lane < IDX_DIM
    qheads = []
    for hp in range(IDX_HEADS // 2):
        qp = qi_ref[0, :, hp * LANES:(hp + 1) * LANES].astype(F32)
        qheads.append(jnp.where(first, qp, 0.0).astype(BF16))
        qheads.append(jnp.where(first, 0.0, qp).astype(BF16))
    q_all = jnp.concatenate(qheads, axis=0)
    krow = lax.broadcasted_iota(jnp.int32, (tq, tq), 0)
    qpos = i * tq + lax.broadcasted_iota(jnp.int32, (tq, tq), 1)

    def chunk_start(c):
        return pl.multiple_of(c * tq, tq)

    def score_chunk(c, carry):
        k0 = chunk_start(c)
        kk2 = ki_ref[0, pl.ds(k0, tq), :]
        score = jnp.zeros((tq, tq), F32)
        s_all = _mm(kk2, q_all, _NT)
        for h in range(IDX_HEADS):
            score = score + wt[h:h + 1, :] * jnp.maximum(s_all[:, h * tq:(h + 1) * tq], 0.0)
        score = jnp.where(score == 0.0, 0.0, score)
        bits = pltpu.bitcast(score, jnp.int32)
        skey = bits ^ ((bits >> 31) & 0x7FFFFFFF)
        key_ref[pl.ds(k0, tq), :] = jnp.where(k0 + krow <= qpos, skey, INT_MIN)
        return carry

    lax.fori_loop(0, i + 1, score_chunk, 0)

    def count(fn):
        def body(c, acc):
            k0 = chunk_start(c)
            ind = fn(key_ref[pl.ds(k0, tq), :], k0 + krow)
            return acc + jnp.sum(ind.reshape(tq // 8, 8, tq), axis=0)
        acc = lax.fori_loop(0, i + 1, body, jnp.zeros((8, tq), F32))
        return jnp.sum(acc, axis=0, keepdims=True)

    kf = float(k_sel)

    def count_ge(cand):
        return count(lambda keys, kpos: jnp.where(keys >= cand, 1.0, 0.0))

    thr0 = jnp.where(count_ge(jnp.zeros((1, tq), jnp.int32)) >= kf, 0, INT_MIN).astype(jnp.int32)

    def thr_step(n, thr):
        cand = thr | jnp.left_shift(jnp.int32(1), 30 - n)
        return jnp.where(count_ge(cand) >= kf, cand, thr)

    thr = lax.fori_loop(0, 31, thr_step, thr0)
    n_gt = count(lambda keys, kpos: jnp.where(keys > thr, 1.0, 0.0))
    n_eq = count(lambda keys, kpos: jnp.where(keys == thr, 1.0, 0.0))
    need = kf - n_gt
    nbits = int(t - 1).bit_length()

    def tie_cut():
        def cut_step(n, cut):
            cand = cut | jnp.left_shift(jnp.int32(1), nbits - 1 - n)
            cnt = count(lambda keys, kpos: jnp.where(keys == thr, jnp.where(kpos < cand, 1.0, 0.0), 0.0))
            return jnp.where(cnt < need, cand, cut)
        return lax.fori_loop(0, nbits, cut_step, jnp.zeros((1, tq), jnp.int32))

    cut = lax.cond(jnp.max(n_eq - need) > 0.0, tie_cut, lambda: jnp.full((1, tq), t, jnp.int32))

    def write_chunk(c, carry):
        k0 = chunk_start(c)
        keys = key_ref[pl.ds(k0, tq), :]
        kpos = k0 + krow
        tie = jnp.where(keys == thr, jnp.where(kpos <= cut, 0.0, MASK_NEG), MASK_NEG)
        bias = jnp.where(keys > thr, 0.0, tie)
        o_ref[0, pl.ds(k0, tq), :] = jnp.where(kpos <= qpos, bias, MASK_NEG).astype(BF16)
        return carry

    lax.fori_loop(0, i + 1, write_chunk, 0)

    def masked_chunk(c, carry):
        o_ref[0, pl.ds(chunk_start(c), tq), :] = jnp.full((tq, tq), MASK_NEG, BF16)
        return carry

    lax.fori_loop(i + 1, nchunk, masked_chunk, 0)


def _dsa_index(qi, ki, wi, k_sel):
    bsz, t, nq = qi.shape
    tq = ATT_T
    return pl.pallas_call(
        functools.partial(_dsa_index_body, k_sel),
        grid=(bsz, t // tq),
        in_specs=[pl.BlockSpec((1, tq, nq), lambda b, i: (b, i, 0)),
                  pl.BlockSpec((1, tq, LANES), lambda b, i: (b, i, 0)),
                  pl.BlockSpec((1, t, LANES), lambda b, i: (b, 0, 0))],
        out_specs=pl.BlockSpec((1, t, tq), lambda b, i: (b, 0, i)),
        out_shape=jax.ShapeDtypeStruct((bsz, t, t), BF16),
        scratch_shapes=[pltpu.VMEM((t, tq), jnp.int32)],
        compiler_params=_cparams(("parallel", "parallel")),
        name="dsa_index",
    )(qi, wi, ki)


def _t5_bucket_np(n):
    n = np.maximum(n, 0)
    max_exact = REL_BUCKETS // 2
    nf = np.maximum(n, 1).astype(np.float32)
    large = max_exact + (np.log(nf / np.float32(max_exact)) / np.float32(math.log(REL_MAX_DIST / max_exact))
                         * np.float32(REL_BUCKETS - max_exact)).astype(np.int32)
    large = np.minimum(large, REL_BUCKETS - 1)
    return np.where(n < max_exact, n, large).astype(np.int32)


def _band_body(bkt_ref, rb_ref, o_ref):
    h = pl.program_id(1)
    bkt = bkt_ref[0]
    acc = jnp.zeros(bkt.shape, F32)
    for b in range(REL_BUCKETS):
        acc = jnp.where(bkt == b, rb_ref[b, h], acc)
    o_ref[0] = acc * LOG2E


def _band_bias(rel_bias):
    tt = ATT_T
    kc = np.arange(tt)[:, None]
    qr = np.arange(tt)[None, :]
    planes = [_t5_bucket_np(d * tt + qr - kc) for d in range(3)]
    assert (planes[2] == REL_BUCKETS - 1).all() and tt + 1 >= 113
    bkt = jnp.asarray(np.stack(planes))
    return pl.pallas_call(
        _band_body,
        grid=(3, ATT_HEADS),
        in_specs=[pl.BlockSpec((1, tt, tt), lambda d, h: (d, 0, 0)),
                  pl.BlockSpec(memory_space=pltpu.SMEM)],
        out_specs=pl.BlockSpec((1, tt, tt), lambda d, h: (d, 0, h)),
        out_shape=jax.ShapeDtypeStruct((3, tt, ATT_HEADS * tt), F32),
        compiler_params=_cparams(("parallel", "parallel")),
        name="band_bias",
    )(bkt, rel_bias)


LOG2E = 1.4426950408889634
ACC_ROWS = KV_LATENT + 16


def _dsa_attn_body(qt_ref, kt_ref, q_ref, ckv_ref, mask_ref, wuk_ref, wuv_ref, near_ref, far_ref, o_ref,
                   ql_ref, m_ref, acc_ref, p_ref, ot_ref):
    i = qt_ref[pl.program_id(1)]
    j = kt_ref[pl.program_id(1)]
    nh = ATT_HEADS
    tq = q_ref.shape[1]
    qscale = ATT_HEAD_DIM ** -0.5 * LOG2E

    @pl.when(j == 0)
    def _():
        for hp in range(nh // 2):
            qp = q_ref[0, :, hp * LANES:(hp + 1) * LANES].astype(BF16)
            qlat = _mm(qp, wuk_ref[hp]) * qscale
            ql_ref[2 * hp * tq:(2 * hp + 1) * tq, :] = qlat[:, 0:KV_LATENT].astype(BF16)
            ql_ref[(2 * hp + 1) * tq:(2 * hp + 2) * tq, :] = qlat[:, KV_LATENT:2 * KV_LATENT].astype(BF16)
        m_ref[...] = jnp.full(m_ref.shape, MASK_NEG, F32)
        acc_ref[...] = jnp.zeros_like(acc_ref)

    def step(near):
        ckv = ckv_ref[0]
        tk = ckv.shape[0]
        s_all = _mm(ckv, ql_ref[...], _NT)
        ckv_aug = jnp.concatenate([ckv.astype(F32).T, jnp.ones((ACC_ROWS - KV_LATENT, tk), F32)],
                                  axis=0).astype(BF16)
        maskb = mask_ref[0].astype(F32)
        m_prev = m_ref[...]
        for h in range(nh):
            hs = slice(h * tq, (h + 1) * tq)
            s = s_all[:, hs] + maskb
            if near:
                s = s + near_ref[i - j, :, hs]
                m_new = jnp.maximum(m_prev[:, hs], jnp.max(s, axis=0, keepdims=True))
                shift = m_new
            else:
                cvec = far_ref[:, hs]
                m_new = jnp.maximum(m_prev[:, hs], jnp.max(s, axis=0, keepdims=True) + cvec)
                shift = m_new - cvec
            p_ref[:, hs] = jnp.exp2(s - shift).astype(BF16)
            m_ref[:, hs] = m_new
        alpha = jnp.exp2(m_prev - m_ref[...])
        acc_ref[...] = alpha * acc_ref[...] + _mm(ckv_aug, p_ref[...])

    @pl.when((j <= i) & (i - j < 2))
    def _():
        step(True)

    @pl.when(i - j >= 2)
    def _():
        step(False)

    @pl.when(j == i)
    def _():
        def norm(h):
            a = acc_ref[:, h * tq:(h + 1) * tq]
            return a[0:KV_LATENT] * (1.0 / a[KV_LATENT:KV_LATENT + 1])
        for hp in range(nh // 2):
            olat = jnp.concatenate([norm(2 * hp), norm(2 * hp + 1)], axis=0).astype(BF16)
            ot_ref[hp * LANES:(hp + 1) * LANES, :] = _mm(wuv_ref[hp], olat)
        o_ref[0] = ot_ref[...].T


def _dsa_attn(q, ckv, maskt, w_uk, w_uv, band):
    bsz, t, d = q.shape
    tt = ATT_T
    nt = t // tt
    nh = ATT_HEADS
    zk = jnp.zeros((nh // 2, ATT_HEAD_DIM, KV_LATENT), F32)
    wuk2 = jnp.concatenate([jnp.concatenate([w_uk[0::2], zk], axis=2),
                            jnp.concatenate([zk, w_uk[1::2]], axis=2)], axis=1).astype(BF16)
    wuv_t = jnp.swapaxes(w_uv, 1, 2)
    zv = jnp.zeros((nh // 2, ATT_HEAD_DIM, KV_LATENT), F32)
    wuv2 = jnp.concatenate([jnp.concatenate([wuv_t[0::2], zv], axis=2),
                            jnp.concatenate([zv, wuv_t[1::2]], axis=2)], axis=1).astype(BF16)
    pairs = [(i, j) for i in range(nt) for j in range(i + 1)]
    q_tab = jnp.asarray([p[0] for p in pairs], jnp.int32)
    k_tab = jnp.asarray([p[1] for p in pairs], jnp.int32)
    grid_spec = pltpu.PrefetchScalarGridSpec(
        num_scalar_prefetch=2,
        grid=(bsz, len(pairs)),
        in_specs=[pl.BlockSpec((1, tt, d), lambda b, s, qt, kt: (b, qt[s], 0)),
                  pl.BlockSpec((1, tt, KV_LATENT), lambda b, s, qt, kt: (b, kt[s], 0)),
                  pl.BlockSpec((1, tt, tt), lambda b, s, qt, kt: (b, kt[s], qt[s])),
                  pl.BlockSpec((nh // 2, LANES, 2 * KV_LATENT), lambda b, s, qt, kt: (0, 0, 0)),
                  pl.BlockSpec((nh // 2, LANES, 2 * KV_LATENT), lambda b, s, qt, kt: (0, 0, 0)),
                  pl.BlockSpec((2, tt, nh * tt), lambda b, s, qt, kt: (0, 0, 0)),
                  pl.BlockSpec((1, nh * tt), lambda b, s, qt, kt: (0, 0))],
        out_specs=pl.BlockSpec((1, tt, d), lambda b, s, qt, kt: (b, qt[s], 0)),
        scratch_shapes=[pltpu.VMEM((nh * tt, KV_LATENT), BF16),
                        pltpu.VMEM((1, nh * tt), F32),
                        pltpu.VMEM((ACC_ROWS, nh * tt), F32),
                        pltpu.VMEM((tt, nh * tt), BF16),
                        pltpu.VMEM((d, tt), F32)])
    return pl.pallas_call(
        _dsa_attn_body,
        grid_spec=grid_spec,
        out_shape=jax.ShapeDtypeStruct((bsz, t, d), F32),
        compiler_params=_cparams(("parallel", "arbitrary")),
        name="dsa_attn",
    )(q_tab, k_tab, q, ckv, maskt, wuk2, wuv2, band[:2], band[2, 0:1, :])


def kernel(x, c, ada_w, ada_b, ln_g, ln_b, ffn_w_in, ffn_w_out, rwkv_mu, rwkv_w_rkv, rwkv_w0, rwkv_w1, rwkv_w2, rwkv_a0, rwkv_a1, rwkv_a2, rwkv_v0, rwkv_v1, rwkv_v2, rwkv_g1, rwkv_g2, rwkv_k_k, rwkv_k_a, rwkv_r_k, rwkv_lnx_g, rwkv_lnx_b, rwkv_w_out, dsa_w_in, dsa_kv_norm, dsa_w_uk, dsa_w_uv, dsa_w_out, rel_bias):
    bsz, t, d = x.shape
    assert d == D_MODEL and t % ATT_T == 0 and t % FFN_TM == 0 and t % PRE_TM == 0
    mod_all = _adaln(c, ada_w, ada_b).reshape(DEPTH, bsz, 6, d)
    band = _band_bias(rel_bias)
    k_sel = min(TOPK_MAX, t // TOPK_DIV)
    bf = lambda w: w.astype(BF16)
    v_first = None
    for i in range(DEPTH):
        mod = mod_all[i]
        j = i // 2
        if i % 2 == 0:
            p = dict(mu=rwkv_mu[j], wr=bf(rwkv_w_rkv[j, 0]), wk=bf(rwkv_w_rkv[j, 1]), wv=bf(rwkv_w_rkv[j, 2]),
                     w0=rwkv_w0[j], w1=bf(rwkv_w1[j]), w2=bf(rwkv_w2[j]),
                     a0=rwkv_a0[j], a1=bf(rwkv_a1[j]), a2=bf(rwkv_a2[j]),
                     g1=bf(rwkv_g1[j]), g2=bf(rwkv_g2[j]), k_k=rwkv_k_k[j], k_a=rwkv_k_a[j])
            if j > 0:
                p.update(v0=rwkv_v0[j - 1], v1=bf(rwkv_v1[j - 1]), v2=bf(rwkv_v2[j - 1]))
            r_s, k_s, a_s, b_s, v, g, gl = _rwkv_pre(x, mod, p, v_first if j > 0 else None)
            if j == 0:
                v_first = v
            z = _rwkv_scan(r_s, k_s, a_s, b_s, v, gl, rwkv_r_k[j], rwkv_lnx_g[j], rwkv_lnx_b[j])
            x = _post(z, g, x, mod, bf(rwkv_w_out[j]), ln_g[i, 0], ln_b[i, 0])
        else:
            q, ckv, qi, ki, wi = _dsa_proj(x, mod, dsa_w_in[j], dsa_kv_norm[j])
            maskt = _dsa_index(qi, ki, wi, k_sel)
            o = _dsa_attn(q, ckv, maskt, dsa_w_uk[j], dsa_w_uv[j], band)
            x = _post(o, None, x, mod, bf(dsa_w_out[j]), ln_g[i, 0], ln_b[i, 0])
        x = _ffn(x, mod, bf(ffn_w_in[i]), bf(ffn_w_out[i]), ln_g[i, 1], ln_b[i, 1])
    return x
```

```python
import functools
import math

import numpy as np
import jax
import jax.numpy as jnp
from jax import lax
from jax.experimental import pallas as pl
from jax.experimental.pallas import tpu as pltpu

F32 = jnp.float32
BF16 = jnp.bfloat16

D_MODEL = 1024
DEPTH = 4
RWKV_HEAD = 64
RWKV_HEADS = D_MODEL // RWKV_HEAD
GN_EPS = RWKV_HEAD * 1e-5
ATT_HEADS = 16
ATT_HEAD_DIM = 64
KV_LATENT = 128
IDX_HEADS = 8
IDX_DIM = 64
TOPK_MAX = 256
TOPK_DIV = 4
REL_BUCKETS = 32
REL_MAX_DIST = 128
FFN_HIDDEN = 2816
DEEPNORM_ALPHA = (2 * DEPTH) ** 0.25
LN_EPS = 1e-5

LANES = 128
CHUNK = 128
PRE_TM = 256
ROW_TM = 256
PROJ_TM = 512
FFN_TM = 512
FFN_TF = 1408
ATT_T = 256
MASK_NEG = -1e30
VMEM_LIMIT = 56 * 1024 * 1024
INT_MIN = -2 ** 31


def _cparams(sem):
    return pltpu.CompilerParams(dimension_semantics=sem, vmem_limit_bytes=VMEM_LIMIT)


def _split2(x):
    hi = x.astype(BF16)
    lo = (x - hi.astype(F32)).astype(BF16)
    return hi, lo


def _split3(x):
    hi = x.astype(BF16)
    r1 = x - hi.astype(F32)
    mid = r1.astype(BF16)
    lo = (r1 - mid.astype(F32)).astype(BF16)
    return hi, mid, lo


_NN = (((1,), (0,)), ((), ()))
_NT = (((1,), (1,)), ((), ()))


def _mm(a, b, dims=_NN):
    return lax.dot_general(a, b, dims, preferred_element_type=F32)


def _dot3(a, b, dims=_NN):
    ah, al = _split2(a)
    bh, bl = _split2(b)
    return _mm(ah, bh, dims) + (_mm(ah, bl, dims) + _mm(al, bh, dims))


def _dot1(a, b, dims=_NN):
    return _mm(a.astype(BF16), b.astype(BF16), dims)


def _dot_hilo_rhs(a, b_bf16):
    h, l = _split2(a)
    return _mm(h, b_bf16) + _mm(l, b_bf16)


def _sigmoid(x):
    return 1.0 / (1.0 + jnp.exp(-x))


def _layernorm(xr, g, b):
    mu = jnp.mean(xr, axis=-1, keepdims=True)
    xc = xr - mu
    var = jnp.mean(xc * xc, axis=-1, keepdims=True)
    return xc * lax.rsqrt(var + LN_EPS) * g + b


def _adaln_body(c_ref, w_ref, b_ref, o_ref):
    c = c_ref[...]
    cond = c * _sigmoid(c)
    o_ref[0] = _dot3(cond, w_ref[0]) + b_ref[0]


def _adaln(c, ada_w, ada_b):
    depth, d, n = ada_w.shape
    bsz = c.shape[0]
    tn = n // 4
    return pl.pallas_call(
        _adaln_body,
        grid=(depth, n // tn),
        in_specs=[
            pl.BlockSpec((bsz, d), lambda i, j: (0, 0)),
            pl.BlockSpec((1, d, tn), lambda i, j: (i, 0, j)),
            pl.BlockSpec((1, 1, tn), lambda i, j: (i, 0, j)),
        ],
        out_specs=pl.BlockSpec((1, bsz, tn), lambda i, j: (i, 0, j)),
        out_shape=jax.ShapeDtypeStruct((depth, bsz, n), F32),
        compiler_params=_cparams(("parallel", "parallel")),
        name="adaln",
    )(c, ada_w, ada_b.reshape(depth, 1, n))


def _ffn_body(x_ref, mod_ref, wg_ref, wu_ref, wo_ref, lng_ref, lnb_ref, o_ref, hin_ref, acc_ref):
    j = pl.program_id(2)

    @pl.when(j == 0)
    def _():
        x = x_ref[0]
        hin_ref[...] = (x * (1.0 + mod_ref[0, 4:5, :]) + mod_ref[0, 3:4, :]).astype(BF16)
        acc_ref[...] = jnp.zeros_like(acc_ref)

    hin = hin_ref[...]
    gate = _mm(hin, wg_ref[...])
    up = _mm(hin, wu_ref[...])
    hid = (gate * _sigmoid(gate) * up).astype(BF16)
    acc_ref[...] += _mm(hid, wo_ref[...])

    @pl.when(j == pl.num_programs(2) - 1)
    def _():
        res = DEEPNORM_ALPHA * x_ref[0] + (1.0 + mod_ref[0, 5:6, :]) * acc_ref[...]
        o_ref[0] = _layernorm(res, lng_ref[...], lnb_ref[...])


def _ffn(x, mod, w_in, w_out, ln_g, ln_b):
    bsz, t, d = x.shape
    f = w_out.shape[0]
    nf = f // FFN_TF
    return pl.pallas_call(
        _ffn_body,
        grid=(bsz, t // FFN_TM, nf),
        in_specs=[
            pl.BlockSpec((1, FFN_TM, d), lambda b, i, j: (b, i, 0)),
            pl.BlockSpec((1, 6, d), lambda b, i, j: (b, 0, 0)),
            pl.BlockSpec((d, FFN_TF), lambda b, i, j: (0, j)),
            pl.BlockSpec((d, FFN_TF), lambda b, i, j: (0, nf + j)),
            pl.BlockSpec((FFN_TF, d), lambda b, i, j: (j, 0)),
            pl.BlockSpec((1, d), lambda b, i, j: (0, 0)),
            pl.BlockSpec((1, d), lambda b, i, j: (0, 0)),
        ],
        out_specs=pl.BlockSpec((1, FFN_TM, d), lambda b, i, j: (b, i, 0)),
        out_shape=jax.ShapeDtypeStruct((bsz, t, d), F32),
        scratch_shapes=[pltpu.VMEM((FFN_TM, d), BF16), pltpu.VMEM((FFN_TM, d), F32)],
        compiler_params=_cparams(("parallel", "parallel", "arbitrary")),
        name="ffn",
    )(x, mod, w_in, w_in, w_out, ln_g.reshape(1, d), ln_b.reshape(1, d))


def _post_body(gated, *refs):
    if gated:
        y_ref, g_ref, x_ref, mod_ref, w_ref, lng_ref, lnb_ref, o_ref = refs
        y = (y_ref[0] * g_ref[0]).astype(BF16)
    else:
        y_ref, x_ref, mod_ref, w_ref, lng_ref, lnb_ref, o_ref = refs
        y = y_ref[0].astype(BF16)
    out = _mm(y, w_ref[...])
    res = DEEPNORM_ALPHA * x_ref[0] + (1.0 + mod_ref[0, 2:3, :]) * out
    o_ref[0] = _layernorm(res, lng_ref[...], lnb_ref[...])


def _post(y, g, x, mod, w, ln_g, ln_b):
    bsz, t, d = x.shape
    row = pl.BlockSpec((1, ROW_TM, d), lambda b, i: (b, i, 0))
    vec = pl.BlockSpec((1, d), lambda b, i: (0, 0))
    gated = g is not None
    acts = [y, g] if gated else [y]
    return pl.pallas_call(
        functools.partial(_post_body, gated),
        grid=(bsz, t // ROW_TM),
        in_specs=[row] * len(acts) + [
            row,
            pl.BlockSpec((1, 6, d), lambda b, i: (b, 0, 0)),
            pl.BlockSpec((d, d), lambda b, i: (0, 0)),
            vec, vec,
        ],
        out_specs=row,
        out_shape=jax.ShapeDtypeStruct((bsz, t, d), F32),
        compiler_params=_cparams(("parallel", "parallel")),
        name="post",
    )(*acts, x, mod, w, ln_g.reshape(1, d), ln_b.reshape(1, d))


def _rwkv_pre_body(has_vres, *refs):
    if has_vres:
        (x_ref, xp_ref, mod_ref, mu_ref, wr_ref, wk_ref, wv_ref, w1_ref, w2_ref, a1_ref, a2_ref,
         g1_ref, g2_ref, vec_ref, seg_ref, segt_ref, tri_ref, vf_ref, v1_ref, v2_ref,
         r_o, k_o, a_o, b_o, v_o, g_o, gl_o) = refs
    else:
        (x_ref, xp_ref, mod_ref, mu_ref, wr_ref, wk_ref, wv_ref, w1_ref, w2_ref, a1_ref, a2_ref,
         g1_ref, g2_ref, vec_ref, seg_ref, segt_ref, tri_ref,
         r_o, k_o, a_o, b_o, v_o, g_o, gl_o) = refs
    i = pl.program_id(1)
    sc = 1.0 + mod_ref[0, 1:2, :]
    sh = mod_ref[0, 0:1, :]
    hin = x_ref[0] * sc + sh
    tm = hin.shape[0]
    prev_row = xp_ref[0, 7:8, :] * sc + sh
    prev_row = jnp.where(i == 0, 0.0, prev_row)
    rows = lax.broadcasted_iota(jnp.int32, hin.shape, 0)
    hprev = jnp.where(rows == 0, prev_row, pltpu.roll(hin, 1, 0))
    xx = hprev - hin

    def mix(p):
        return hin + xx * mu_ref[p:p + 1, :]

    xr, xk, xv = mix(0).astype(BF16), mix(1).astype(BF16), mix(2).astype(BF16)
    xw, xa, xg = mix(3).astype(BF16), mix(4).astype(BF16), mix(5).astype(BF16)
    w0, a0, kkw, kaw = vec_ref[0:1, :], vec_ref[1:2, :], vec_ref[2:3, :], vec_ref[3:4, :]

    r = _mm(xr, wr_ref[...])
    k = _mm(xk, wk_ref[...])
    v = _mm(xv, wv_ref[...])

    wl = w0 + _mm(jnp.tanh(_mm(xw, w1_ref[...])).astype(BF16), w2_ref[...])
    nz = -wl
    softplus = jnp.maximum(nz, 0.0) + jnp.log(1.0 + jnp.exp(-jnp.abs(nz)))
    logdec = -jnp.exp(-softplus - 0.5)

    a = _sigmoid(a0 + _mm(_mm(xa, a1_ref[...]).astype(BF16), a2_ref[...]))
    if has_vres:
        v0 = vec_ref[4:5, :]
        vmix = _sigmoid(v0 + _mm(_mm(xv, v1_ref[...]).astype(BF16), v2_ref[...]))
        v = v + (vf_ref[0] - v) * vmix
    g = _mm(_sigmoid(_mm(xg, g1_ref[...])).astype(BF16), g2_ref[...])

    kk = k * kkw
    ss = _dot_hilo_rhs(kk * kk, seg_ref[...])
    nrm = jnp.maximum(jnp.sqrt(ss), 1e-12)
    inv = _dot_hilo_rhs(1.0 / nrm, segt_ref[...])
    kk = kk * inv
    k = k * (1.0 + (a - 1.0) * kaw)

    cum = _dot_exact_rhs_lhs(tri_ref[...], logdec)
    ginc = jnp.exp(cum)
    ginv = jnp.exp(-cum)
    gprev = jnp.exp(cum - logdec)

    r_o[0] = r * ginc
    k_o[0] = k * ginv
    a_o[0] = -kk * gprev
    b_o[0] = kk * a * ginv
    v_o[0] = v
    g_o[0] = g
    for cc in range(tm // CHUNK):
        first, last = cc * CHUNK, cc * CHUNK + CHUNK - 1
        gl_o[0, cc, 0:1] = jnp.exp(logdec[first:first + 1, :] - cum[first:first + 1, :])
        gl_o[0, cc, 1:2] = ginc[last:last + 1, :]


def _dot_exact_rhs_lhs(m_bf16, x):
    h, mid, l = _split3(x)
    return _mm(m_bf16, h) + (_mm(m_bf16, mid) + _mm(m_bf16, l))


def _rwkv_pre(x, mod, p, v_first):
    bsz, t, d = x.shape
    tm = PRE_TM
    has_vres = v_first is not None
    row = pl.BlockSpec((1, tm, d), lambda b, i: (b, i, 0))

    def full(shape):
        return pl.BlockSpec(shape, lambda b, i: (0,) * len(shape))

    heads = d // RWKV_HEAD
    seg = np.zeros((d, LANES), np.float32)
    seg[np.arange(d), np.arange(d) // RWKV_HEAD] = 1.0
    idx = np.arange(tm)
    same = idx[:, None] // CHUNK == idx[None, :] // CHUNK
    col = idx[None, :] % CHUNK
    tri = (same & (col <= idx[:, None] % CHUNK)).astype(np.float32) \
        - (same & (col <= CHUNK // 2 - 1)).astype(np.float32)
    vec_rows = [p['w0'], p['a0'], p['k_k'], p['k_a']] + ([p['v0']] if has_vres else [])
    vec = jnp.stack(vec_rows + [jnp.zeros_like(p['w0'])] * (8 - len(vec_rows)))
    dl, da, dg = p['w1'].shape[1], p['a1'].shape[1], p['g1'].shape[1]
    ins = [x, x, mod, p['mu'], p['wr'], p['wk'], p['wv'], p['w1'], p['w2'], p['a1'], p['a2'],
           p['g1'], p['g2'], vec, jnp.asarray(seg, BF16), jnp.asarray(seg.T, BF16),
           jnp.asarray(tri, BF16)]
    specs = [row,
             pl.BlockSpec((1, 8, d), lambda b, i: (b, jnp.maximum(i * (tm // 8) - 1, 0), 0)),
             pl.BlockSpec((1, 6, d), lambda b, i: (b, 0, 0)),
             full((6, d)), full((d, d)), full((d, d)), full((d, d)),
             full((d, dl)), full((dl, d)), full((d, da)), full((da, d)),
             full((d, dg)), full((dg, d)), full((8, d)),
             full((d, LANES)), full((LANES, d)), full((tm, tm))]
    if has_vres:
        dv = p['v1'].shape[1]
        ins += [v_first, p['v1'], p['v2']]
        specs += [row, full((d, dv)), full((dv, d))]
    act = jax.ShapeDtypeStruct((bsz, t, d), F32)
    nch = t // CHUNK
    outs = pl.pallas_call(
        functools.partial(_rwkv_pre_body, has_vres),
        grid=(bsz, t // tm),
        in_specs=specs,
        out_specs=[row] * 6 + [pl.BlockSpec((1, tm // CHUNK, 2, d), lambda b, i: (b, i, 0, 0))],
        out_shape=[act] * 6 + [jax.ShapeDtypeStruct((bsz, nch, 2, d), F32)],
        compiler_params=_cparams(("parallel", "parallel")),
        name="rwkv_pre",
    )(*ins)
    return outs


_BNN = (((2,), (1,)), ((0,), (0,)))
_BNT = (((2,), (2,)), ((0,), (0,)))


def _rwkv_scan_body(r_ref, k_ref, a_ref, b_ref, v_ref, gl_ref, vec_ref, o_ref, s_ref):
    c = pl.program_id(1)

    @pl.when(c == 0)
    def _():
        s_ref[...] = jnp.zeros_like(s_ref)

    L = CHUNK
    NP = RWKV_HEADS // 2
    shp = (NP, L, LANES)
    plane = lax.broadcasted_iota(jnp.int32, (1, L, 2 * L), 2)
    ph0 = plane < L
    tt = lax.broadcasted_iota(jnp.int32, (1, L, 2 * L), 1)
    ss = plane & (L - 1)
    strict = ss < tt
    incl = ss <= tt
    eye = jnp.where(ss == tt, 1.0, 0.0)
    base_bits = 4

    def same_block(bits):
        return (ss >> bits) == (tt >> bits)

    lvl0 = strict & same_block(base_bits)
    merges = [strict & same_block(bits + 1) & ((ss >> bits) != (tt >> bits))
              for bits in range(base_bits, (L - 1).bit_length())]
    dh0 = lax.broadcasted_iota(jnp.int32, (1, L, LANES), 2) < RWKV_HEAD
    ri = lax.broadcasted_iota(jnp.int32, (1, LANES, LANES), 1)
    ci = lax.broadcasted_iota(jnp.int32, (1, LANES, LANES), 2)
    blockdiag = (ri >> 6) == (ci >> 6)
    seg_mean = jnp.where(blockdiag[0], 1.0, 0.0).astype(BF16)

    def grp(ref):
        return jnp.stack([ref[0, :, hp * LANES:(hp + 1) * LANES] for hp in range(NP)])

    def bd(x):
        return jnp.concatenate([jnp.where(dh0, x, 0.0), jnp.where(dh0, 0.0, x)], axis=1)

    def bdp(y):
        return jnp.concatenate([jnp.where(ph0, y, 0.0), jnp.where(ph0, 0.0, y)], axis=1)

    def pmm(xp, y):
        return _dot1(xp, bd(y), _BNN)

    def ppm(xp, yp):
        return _dot1(xp, bdp(yp), _BNN)

    R, K, A, Bv, V = grp(r_ref), grp(k_ref), grp(a_ref), grp(b_ref), grp(v_ref)
    gvec = lambda n: jnp.stack([gl_ref[0, 0, n:n + 1, hp * LANES:(hp + 1) * LANES] for hp in range(NP)])
    S = s_ref[...] * gvec(0)

    AR = jnp.concatenate([A, R], axis=1)
    BK = jnp.concatenate([bd(Bv), bd(K)], axis=1)
    G = _dot1(AR, BK, _BNT)
    A_ab = jnp.where(strict, G[:, 0:L, 0:2 * L], 0.0)
    A_ak = jnp.where(strict, G[:, 0:L, 2 * L:4 * L], 0.0)
    A_rb = jnp.where(incl, G[:, L:2 * L, 0:2 * L], 0.0)
    A_rk = jnp.where(incl, G[:, L:2 * L, 2 * L:4 * L], 0.0)

    a0 = jnp.where(lvl0, A_ab, 0.0)
    Tm = eye + a0
    P = a0
    for _ in range(base_bits - 1):
        P = ppm(P, P)
        Tm = Tm + ppm(Tm, P)
    for lvl in merges:
        Tm = Tm + ppm(ppm(Tm, jnp.where(lvl, A_ab, 0.0)), Tm)

    PQ = _dot1(AR, S, _BNT)
    W = PQ[:, 0:L] + pmm(A_ak, V)
    U = pmm(Tm, W)
    Y = PQ[:, L:2 * L] + _dot1(jnp.concatenate([A_rb, A_rk], axis=2),
                               jnp.concatenate([bd(U), bd(V)], axis=1), _BNN)
    UV = jnp.concatenate([U, V], axis=1)
    UVt = jnp.stack([UV[hp].T for hp in range(NP)])
    BK2 = jnp.concatenate([Bv, K], axis=1)
    upd = _dot1(UVt, BK2, _BNN)
    s_ref[...] = (S + jnp.where(blockdiag, upd, 0.0)) * gvec(1)

    vrow = lambda n: jnp.stack([vec_ref[n:n + 1, hp * LANES:(hp + 1) * LANES] for hp in range(NP)])
    inv_n = 1.0 / RWKV_HEAD
    flat = lambda x: x.reshape(NP * L, LANES)
    mean = _dot_hilo_rhs(flat(Y), seg_mean).reshape(shp) * inv_n
    yc = Y - mean
    var = _dot_hilo_rhs(flat(yc * yc), seg_mean).reshape(shp) * inv_n
    yn = yc * lax.rsqrt(var + GN_EPS) * vrow(1) + vrow(2)
    bonus = _dot_hilo_rhs(flat(R * K * vrow(0)), seg_mean).reshape(shp)
    out = yn + bonus * V
    for hp in range(NP):
        o_ref[0, :, hp * LANES:(hp + 1) * LANES] = out[hp]


def _rwkv_scan(r, k, a, b, v, gl, r_k, lnx_g, lnx_b):
    bsz, t, d = r.shape
    row = pl.BlockSpec((1, CHUNK, d), lambda bb, c: (bb, c, 0))
    vec = jnp.stack([r_k.reshape(d), lnx_g, lnx_b] + [jnp.zeros((d,), F32)] * 5)
    return pl.pallas_call(
        _rwkv_scan_body,
        grid=(bsz, t // CHUNK),
        in_specs=[row] * 5 + [
            pl.BlockSpec((1, 1, 2, d), lambda bb, c: (bb, c, 0, 0)),
            pl.BlockSpec((8, d), lambda bb, c: (0, 0)),
        ],
        out_specs=row,
        out_shape=jax.ShapeDtypeStruct((bsz, t, d), F32),
        scratch_shapes=[pltpu.VMEM((RWKV_HEADS // 2, LANES, LANES), F32)],
        compiler_params=_cparams(("parallel", "arbitrary")),
        name="rwkv_scan",
    )(r, k, a, b, v, gl, vec)


IDX_COLS = 768


def _dsa_proj_body(x_ref, mod_ref, wq_ref, wc_ref, wi_ref, kvn_ref,
                   q_o, ckv_o, qi_o, ki_o, wi_o):
    hin = (x_ref[0] * (1.0 + mod_ref[0, 1:2, :]) + mod_ref[0, 0:1, :]).astype(BF16)
    q_o[0] = _mm(hin, wq_ref[...]).astype(BF16)
    ckv = _mm(hin, wc_ref[...])
    ms = jnp.mean(ckv * ckv, axis=-1, keepdims=True)
    ckv_o[0] = (ckv * lax.rsqrt(ms + 1e-6) * kvn_ref[...]).astype(BF16)
    idx = _mm(hin, wi_ref[...])
    nq = IDX_HEADS * IDX_DIM
    qi_o[0] = idx[:, 0:nq].astype(BF16)
    ki_o[0] = idx[:, nq:nq + LANES].astype(BF16)
    wi_o[0] = idx[:, nq + LANES:nq + 2 * LANES] * (IDX_HEADS ** -0.5 * IDX_DIM ** -0.5)


def _dsa_proj(x, mod, w_in, kv_norm):
    bsz, t, d = x.shape
    c1 = ATT_HEADS * ATT_HEAD_DIM
    c2 = c1 + KV_LATENT
    c3 = c2 + IDX_HEADS * IDX_DIM
    c4 = c3 + IDX_DIM
    wq = w_in[:, :c1].astype(BF16)
    wc = w_in[:, c1:c2].astype(BF16)
    pad = IDX_COLS - (c3 - c2) - 2 * IDX_DIM - IDX_HEADS
    widx = jnp.concatenate([w_in[:, c2:c3], w_in[:, c3:c4], w_in[:, c3:c4], w_in[:, c4:],
                            jnp.zeros((d, pad), F32)], axis=1).astype(BF16)
    tm = PROJ_TM

    def full(shape):
        return pl.BlockSpec(shape, lambda b, i: (0,) * len(shape))

    def row(n):
        return pl.BlockSpec((1, tm, n), lambda b, i: (b, i, 0))

    def act(n, dtype):
        return jax.ShapeDtypeStruct((bsz, t, n), dtype)

    nq = IDX_HEADS * IDX_DIM
    return pl.pallas_call(
        _dsa_proj_body,
        grid=(bsz, t // tm),
        in_specs=[row(d), pl.BlockSpec((1, 6, d), lambda b, i: (b, 0, 0)),
                  full((d, c1)), full((d, KV_LATENT)), full((d, IDX_COLS)),
                  full((1, KV_LATENT))],
        out_specs=[row(c1), row(KV_LATENT), row(nq), row(LANES), row(LANES)],
        out_shape=[act(c1, BF16), act(KV_LATENT, BF16), act(nq, BF16), act(LANES, BF16), act(LANES, F32)],
        compiler_params=_cparams(("parallel", "parallel")),
        name="dsa_proj",
    )(x, mod, wq, wc, widx, kv_norm.reshape(1, KV_LATENT))


def _dsa_index_body(k_sel, qi_ref, wi_ref, ki_ref, o_ref, key_ref):
    i = pl.program_id(1)
    tq = qi_ref.shape[1]
    t = ki_ref.shape[1]
    nchunk = t // tq
    wt = wi_ref[0].T
    lane = lax.broadcasted_iota(jnp.int32, (tq, LANES), 1)
    first = lane < IDX_DIM
    qheads = []
    for hp in range(IDX_HEADS // 2):
        qp = qi_ref[0, :, hp * LANES:(hp + 1) * LANES].astype(F32)
        qheads.append(jnp.where(first, qp, 0.0).astype(BF16))
        qheads.append(jnp.where(first, 0.0, qp).astype(BF16))
    q_all = jnp.concatenate(qheads, axis=0)
    krow = lax.broadcasted_iota(jnp.int32, (tq, tq), 0)
    qpos = i * tq + lax.broadcasted_iota(jnp.int32, (tq, tq), 1)

    def chunk_start(c):
        return pl.multiple_of(c * tq, tq)

    def score_chunk(c, carry):
        k0 = chunk_start(c)
        kk2 = ki_ref[0, pl.ds(k0, tq), :]
        score = jnp.zeros((tq, tq), F32)
        s_all = _mm(kk2, q_all, _NT)
        for h in range(IDX_HEADS):
            score = score + wt[h:h + 1, :] * jnp.maximum(s_all[:, h * tq:(h + 1) * tq], 0.0)
        score = jnp.where(score == 0.0, 0.0, score)
        bits = pltpu.bitcast(score, jnp.int32)
        skey = bits ^ ((bits >> 31) & 0x7FFFFFFF)
        key_ref[pl.ds(k0, tq), :] = jnp.where(k0 + krow <= qpos, skey, INT_MIN)
        return carry

    lax.fori_loop(0, i + 1, score_chunk, 0)

    def count(fn):
        def body(c, acc):
            k0 = chunk_start(c)
            ind = fn(key_ref[pl.ds(k0, tq), :], k0 + krow)
            part = jnp.sum(ind.reshape(tq // 32, 4, 8, tq), axis=0)
            return acc + jnp.sum(part, axis=0)
        acc = lax.fori_loop(0, i + 1, body, jnp.zeros((8, tq), F32))
        return jnp.sum(acc, axis=0, keepdims=True)

    kf = float(k_sel)

    def count_ge(cand):
        return count(lambda keys, kpos: jnp.where(keys >= cand, 1.0, 0.0))

    thr0 = jnp.where(count_ge(jnp.zeros((1, tq), jnp.int32)) >= kf, 0, INT_MIN).astype(jnp.int32)

    def thr_step(n, thr):
        cand = thr | jnp.left_shift(jnp.int32(1), 30 - n)
        return jnp.where(count_ge(cand) >= kf, cand, thr)

    thr = lax.fori_loop(0, 31, thr_step, thr0)
    n_gt = count(lambda keys, kpos: jnp.where(keys > thr, 1.0, 0.0))
    n_eq = count(lambda keys, kpos: jnp.where(keys == thr, 1.0, 0.0))
    need = kf - n_gt
    nbits = int(t - 1).bit_length()

    def tie_cut():
        def cut_step(n, cut):
            cand = cut | jnp.left_shift(jnp.int32(1), nbits - 1 - n)
            cnt = count(lambda keys, kpos: jnp.where(keys == thr, jnp.where(kpos < cand, 1.0, 0.0), 0.0))
            return jnp.where(cnt < need, cand, cut)
        return lax.fori_loop(0, nbits, cut_step, jnp.zeros((1, tq), jnp.int32))

    cut = lax.cond(jnp.max(n_eq - need) > 0.0, tie_cut, lambda: jnp.full((1, tq), t, jnp.int32))

    def write_chunk(c, carry):
        k0 = chunk_start(c)
        keys = key_ref[pl.ds(k0, tq), :]
        kpos = k0 + krow
        tie = jnp.where(keys == thr, jnp.where(kpos <= cut, 0.0, MASK_NEG), MASK_NEG)
        bias = jnp.where(keys > thr, 0.0, tie)
        o_ref[0, pl.ds(k0, tq), :] = jnp.where(kpos <= qpos, bias, MASK_NEG).astype(BF16)
        return carry

    lax.fori_loop(0, i + 1, write_chunk, 0)

    def masked_chunk(c, carry):
        o_ref[0, pl.ds(chunk_start(c), tq), :] = jnp.full((tq, tq), MASK_NEG, BF16)
        return carry

    lax.fori_loop(i + 1, nchunk, masked_chunk, 0)


def _dsa_index(qi, ki, wi, k_sel):
    bsz, t, nq = qi.shape
    tq = ATT_T
    return pl.pallas_call(
        functools.partial(_dsa_index_body, k_sel),
        grid=(bsz, t // tq),
        in_specs=[pl.BlockSpec((1, tq, nq), lambda b, i: (b, i, 0)),
                  pl.BlockSpec((1, tq, LANES), lambda b, i: (b, i, 0)),
                  pl.BlockSpec((1, t, LANES), lambda b, i: (b, 0, 0))],
        out_specs=pl.BlockSpec((1, t, tq), lambda b, i: (b, 0, i)),
        out_shape=jax.ShapeDtypeStruct((bsz, t, t), BF16),
        scratch_shapes=[pltpu.VMEM((t, tq), jnp.int32)],
        compiler_params=_cparams(("parallel", "parallel")),
        name="dsa_index",
    )(qi, wi, ki)


def _t5_bucket_np(n):
    n = np.maximum(n, 0)
    max_exact = REL_BUCKETS // 2
    nf = np.maximum(n, 1).astype(np.float32)
    large = max_exact + (np.log(nf / np.float32(max_exact)) / np.float32(math.log(REL_MAX_DIST / max_exact))
                         * np.float32(REL_BUCKETS - max_exact)).astype(np.int32)
    large = np.minimum(large, REL_BUCKETS - 1)
    return np.where(n < max_exact, n, large).astype(np.int32)


def _band_body(bkt_ref, rb_ref, o_ref):
    h = pl.program_id(1)
    bkt = bkt_ref[0]
    acc = jnp.zeros(bkt.shape, F32)
    for b in range(REL_BUCKETS):
        acc = jnp.where(bkt == b, rb_ref[b, h], acc)
    o_ref[0] = acc * LOG2E


def _band_bias(rel_bias):
    tt = ATT_T
    kc = np.arange(tt)[:, None]
    qr = np.arange(tt)[None, :]
    planes = [_t5_bucket_np(d * tt + qr - kc) for d in range(3)]
    assert (planes[2] == REL_BUCKETS - 1).all() and tt + 1 >= 113
    bkt = jnp.asarray(np.stack(planes))
    return pl.pallas_call(
        _band_body,
        grid=(3, ATT_HEADS),
        in_specs=[pl.BlockSpec((1, tt, tt), lambda d, h: (d, 0, 0)),
                  pl.BlockSpec(memory_space=pltpu.SMEM)],
        out_specs=pl.BlockSpec((1, tt, tt), lambda d, h: (d, 0, h)),
        out_shape=jax.ShapeDtypeStruct((3, tt, ATT_HEADS * tt), F32),
        compiler_params=_cparams(("parallel", "parallel")),
        name="band_bias",
    )(bkt, rel_bias)


LOG2E = 1.4426950408889634
ACC_ROWS = KV_LATENT + 16


def _dsa_attn_body(qt_ref, kt_ref, q_ref, ckv_ref, mask_ref, wuk_ref, wuv_ref, near_ref, far_ref,
                   x_ref, mod_ref, wout_ref, lng_ref, lnb_ref, o_ref,
                   ql_ref, m_ref, acc_ref, p_ref, ot_ref):
    i = qt_ref[pl.program_id(1)]
    j = kt_ref[pl.program_id(1)]
    nh = ATT_HEADS
    tq = q_ref.shape[1]
    qscale = ATT_HEAD_DIM ** -0.5 * LOG2E

    @pl.when(j == 0)
    def _():
        for hp in range(nh // 2):
            qp = q_ref[0, :, hp * LANES:(hp + 1) * LANES].astype(BF16)
            qlat = _mm(qp, wuk_ref[hp]) * qscale
            ql_ref[2 * hp * tq:(2 * hp + 1) * tq, :] = qlat[:, 0:KV_LATENT].astype(BF16)
            ql_ref[(2 * hp + 1) * tq:(2 * hp + 2) * tq, :] = qlat[:, KV_LATENT:2 * KV_LATENT].astype(BF16)
        m_ref[...] = jnp.full(m_ref.shape, MASK_NEG, F32)
        acc_ref[...] = jnp.zeros_like(acc_ref)

    def step(near):
        ckv = ckv_ref[0]
        tk = ckv.shape[0]
        s_all = _mm(ckv, ql_ref[...], _NT)
        ckv_aug = jnp.concatenate([ckv.astype(F32).T, jnp.ones((ACC_ROWS - KV_LATENT, tk), F32)],
                                  axis=0).astype(BF16)
        maskb = mask_ref[0].astype(F32)
        m_prev = m_ref[...]
        for h in range(nh):
            hs = slice(h * tq, (h + 1) * tq)
            s = s_all[:, hs] + maskb
            if near:
                s = s + near_ref[i - j, :, hs]
                m_new = jnp.maximum(m_prev[:, hs], jnp.max(s, axis=0, keepdims=True))
                shift = m_new
            else:
                cvec = far_ref[:, hs]
                m_new = jnp.maximum(m_prev[:, hs], jnp.max(s, axis=0, keepdims=True) + cvec)
                shift = m_new - cvec
            p_ref[:, hs] = jnp.exp2(s - shift).astype(BF16)
            m_ref[:, hs] = m_new
        alpha = jnp.exp2(m_prev - m_ref[...])
        acc_ref[...] = alpha * acc_ref[...] + _mm(ckv_aug, p_ref[...])

    @pl.when((j <= i) & (i - j < 2))
    def _():
        step(True)

    @pl.when(i - j >= 2)
    def _():
        step(False)

    @pl.when(j == i)
    def _():
        def norm(h):
            a = acc_ref[:, h * tq:(h + 1) * tq]
            return a[0:KV_LATENT] * (1.0 / a[KV_LATENT:KV_LATENT + 1])
        for hp in range(nh // 2):
            olat = jnp.concatenate([norm(2 * hp), norm(2 * hp + 1)], axis=0).astype(BF16)
            ot_ref[hp * LANES:(hp + 1) * LANES, :] = _mm(wuv_ref[hp], olat)
        y = _mm(ot_ref[...].T.astype(BF16), wout_ref[...])
        res = DEEPNORM_ALPHA * x_ref[0] + (1.0 + mod_ref[0, 2:3, :]) * y
        o_ref[0] = _layernorm(res, lng_ref[...], lnb_ref[...])


def _dsa_attn(q, ckv, maskt, w_uk, w_uv, band, x, mod, w_out, ln_g, ln_b):
    bsz, t, d = q.shape
    tt = ATT_T
    nt = t // tt
    nh = ATT_HEADS
    zk = jnp.zeros((nh // 2, ATT_HEAD_DIM, KV_LATENT), F32)
    wuk2 = jnp.concatenate([jnp.concatenate([w_uk[0::2], zk], axis=2),
                            jnp.concatenate([zk, w_uk[1::2]], axis=2)], axis=1).astype(BF16)
    wuv_t = jnp.swapaxes(w_uv, 1, 2)
    zv = jnp.zeros((nh // 2, ATT_HEAD_DIM, KV_LATENT), F32)
    wuv2 = jnp.concatenate([jnp.concatenate([wuv_t[0::2], zv], axis=2),
                            jnp.concatenate([zv, wuv_t[1::2]], axis=2)], axis=1).astype(BF16)
    pairs = [(i, j) for i in range(nt) for j in range(i + 1)]
    q_tab = jnp.asarray([p[0] for p in pairs], jnp.int32)
    k_tab = jnp.asarray([p[1] for p in pairs], jnp.int32)
    grid_spec = pltpu.PrefetchScalarGridSpec(
        num_scalar_prefetch=2,
        grid=(bsz, len(pairs)),
        in_specs=[pl.BlockSpec((1, tt, d), lambda b, s, qt, kt: (b, qt[s], 0)),
                  pl.BlockSpec((1, tt, KV_LATENT), lambda b, s, qt, kt: (b, kt[s], 0)),
                  pl.BlockSpec((1, tt, tt), lambda b, s, qt, kt: (b, kt[s], qt[s])),
                  pl.BlockSpec((nh // 2, LANES, 2 * KV_LATENT), lambda b, s, qt, kt: (0, 0, 0)),
                  pl.BlockSpec((nh // 2, LANES, 2 * KV_LATENT), lambda b, s, qt, kt: (0, 0, 0)),
                  pl.BlockSpec((2, tt, nh * tt), lambda b, s, qt, kt: (0, 0, 0)),
                  pl.BlockSpec((1, nh * tt), lambda b, s, qt, kt: (0, 0)),
                  pl.BlockSpec((1, tt, d), lambda b, s, qt, kt: (b, qt[s], 0)),
                  pl.BlockSpec((1, 6, d), lambda b, s, qt, kt: (b, 0, 0)),
                  pl.BlockSpec((d, d), lambda b, s, qt, kt: (0, 0)),
                  pl.BlockSpec((1, d), lambda b, s, qt, kt: (0, 0)),
                  pl.BlockSpec((1, d), lambda b, s, qt, kt: (0, 0))],
        out_specs=pl.BlockSpec((1, tt, d), lambda b, s, qt, kt: (b, qt[s], 0)),
        scratch_shapes=[pltpu.VMEM((nh * tt, KV_LATENT), BF16),
                        pltpu.VMEM((1, nh * tt), F32),
                        pltpu.VMEM((ACC_ROWS, nh * tt), F32),
                        pltpu.VMEM((tt, nh * tt), BF16),
                        pltpu.VMEM((d, tt), F32)])
    return pl.pallas_call(
        _dsa_attn_body,
        grid_spec=grid_spec,
        out_shape=jax.ShapeDtypeStruct((bsz, t, d), F32),
        compiler_params=_cparams(("parallel", "arbitrary")),
        name="dsa_attn",
    )(q_tab, k_tab, q, ckv, maskt, wuk2, wuv2, band[:2], band[2, 0:1, :],
      x, mod, w_out, ln_g.reshape(1, d), ln_b.reshape(1, d))


def kernel(x, c, ada_w, ada_b, ln_g, ln_b, ffn_w_in, ffn_w_out, rwkv_mu, rwkv_w_rkv, rwkv_w0, rwkv_w1, rwkv_w2, rwkv_a0, rwkv_a1, rwkv_a2, rwkv_v0, rwkv_v1, rwkv_v2, rwkv_g1, rwkv_g2, rwkv_k_k, rwkv_k_a, rwkv_r_k, rwkv_lnx_g, rwkv_lnx_b, rwkv_w_out, dsa_w_in, dsa_kv_norm, dsa_w_uk, dsa_w_uv, dsa_w_out, rel_bias):
    bsz, t, d = x.shape
    assert d == D_MODEL and t % ATT_T == 0 and t % FFN_TM == 0 and t % PRE_TM == 0
    mod_all = _adaln(c, ada_w, ada_b).reshape(DEPTH, bsz, 6, d)
    band = _band_bias(rel_bias)
    k_sel = min(TOPK_MAX, t // TOPK_DIV)
    bf = lambda w: w.astype(BF16)
    v_first = None
    for i in range(DEPTH):
        mod = mod_all[i]
        j = i // 2
        if i % 2 == 0:
            p = dict(mu=rwkv_mu[j], wr=bf(rwkv_w_rkv[j, 0]), wk=bf(rwkv_w_rkv[j, 1]), wv=bf(rwkv_w_rkv[j, 2]),
                     w0=rwkv_w0[j], w1=bf(rwkv_w1[j]), w2=bf(rwkv_w2[j]),
                     a0=rwkv_a0[j], a1=bf(rwkv_a1[j]), a2=bf(rwkv_a2[j]),
                     g1=bf(rwkv_g1[j]), g2=bf(rwkv_g2[j]), k_k=rwkv_k_k[j], k_a=rwkv_k_a[j])
            if j > 0:
                p.update(v0=rwkv_v0[j - 1], v1=bf(rwkv_v1[j - 1]), v2=bf(rwkv_v2[j - 1]))
            r_s, k_s, a_s, b_s, v, g, gl = _rwkv_pre(x, mod, p, v_first if j > 0 else None)
            if j == 0:
                v_first = v
            z = _rwkv_scan(r_s, k_s, a_s, b_s, v, gl, rwkv_r_k[j], rwkv_lnx_g[j], rwkv_lnx_b[j])
            x = _post(z, g, x, mod, bf(rwkv_w_out[j]), ln_g[i, 0], ln_b[i, 0])
        else:
            q, ckv, qi, ki, wi = _dsa_proj(x, mod, dsa_w_in[j], dsa_kv_norm[j])
            maskt = _dsa_index(qi, ki, wi, k_sel)
            x = _dsa_attn(q, ckv, maskt, dsa_w_uk[j], dsa_w_uv[j], band,
                          x, mod, bf(dsa_w_out[j]), ln_g[i, 0], ln_b[i, 0])
        x = _ffn(x, mod, bf(ffn_w_in[i]), bf(ffn_w_out[i]), ln_g[i, 1], ln_b[i, 1])
    return x
```

```python
import functools
import math

import numpy as np
import jax
import jax.numpy as jnp
from jax import lax
from jax.experimental import pallas as pl
from jax.experimental.pallas import tpu as pltpu

F32 = jnp.float32
BF16 = jnp.bfloat16

D_MODEL = 1024
DEPTH = 4
RWKV_HEAD = 64
RWKV_HEADS = D_MODEL // RWKV_HEAD
GN_EPS = RWKV_HEAD * 1e-5
ATT_HEADS = 16
ATT_HEAD_DIM = 64
KV_LATENT = 128
IDX_HEADS = 8
IDX_DIM = 64
TOPK_MAX = 256
TOPK_DIV = 4
REL_BUCKETS = 32
REL_MAX_DIST = 128
FFN_HIDDEN = 2816
DEEPNORM_ALPHA = (2 * DEPTH) ** 0.25
LN_EPS = 1e-5

LANES = 128
CHUNK = 128
PRE_TM = 256
PROJ_TM = 512
FFN_TM = 512
FFN_TF = 1408
ATT_T = 256
MASK_NEG = -1e30
VMEM_LIMIT = 56 * 1024 * 1024
INT_MIN = -2 ** 31


def _cparams(sem):
    return pltpu.CompilerParams(dimension_semantics=sem, vmem_limit_bytes=VMEM_LIMIT)


def _split2(x):
    hi = x.astype(BF16)
    lo = (x - hi.astype(F32)).astype(BF16)
    return hi, lo


def _split3(x):
    hi = x.astype(BF16)
    r1 = x - hi.astype(F32)
    mid = r1.astype(BF16)
    lo = (r1 - mid.astype(F32)).astype(BF16)
    return hi, mid, lo


_NN = (((1,), (0,)), ((), ()))
_NT = (((1,), (1,)), ((), ()))


def _mm(a, b, dims=_NN):
    return lax.dot_general(a, b, dims, preferred_element_type=F32)


def _dot3(a, b, dims=_NN):
    ah, al = _split2(a)
    bh, bl = _split2(b)
    return _mm(ah, bh, dims) + (_mm(ah, bl, dims) + _mm(al, bh, dims))


def _dot1(a, b, dims=_NN):
    return _mm(a.astype(BF16), b.astype(BF16), dims)


def _dot_hilo_rhs(a, b_bf16):
    h, l = _split2(a)
    return _mm(h, b_bf16) + _mm(l, b_bf16)


def _sigmoid(x):
    return 1.0 / (1.0 + jnp.exp(-x))


def _layernorm(xr, g, b):
    mu = jnp.mean(xr, axis=-1, keepdims=True)
    xc = xr - mu
    var = jnp.mean(xc * xc, axis=-1, keepdims=True)
    return xc * lax.rsqrt(var + LN_EPS) * g + b


def _adaln_body(c_ref, w_ref, b_ref, o_ref):
    c = c_ref[...]
    cond = c * _sigmoid(c)
    o_ref[0] = _dot3(cond, w_ref[0]) + b_ref[0]


def _adaln(c, ada_w, ada_b):
    depth, d, n = ada_w.shape
    bsz = c.shape[0]
    tn = n // 4
    return pl.pallas_call(
        _adaln_body,
        grid=(depth, n // tn),
        in_specs=[
            pl.BlockSpec((bsz, d), lambda i, j: (0, 0)),
            pl.BlockSpec((1, d, tn), lambda i, j: (i, 0, j)),
            pl.BlockSpec((1, 1, tn), lambda i, j: (i, 0, j)),
        ],
        out_specs=pl.BlockSpec((1, bsz, tn), lambda i, j: (i, 0, j)),
        out_shape=jax.ShapeDtypeStruct((depth, bsz, n), F32),
        compiler_params=_cparams(("parallel", "parallel")),
        name="adaln",
    )(c, ada_w, ada_b.reshape(depth, 1, n))


def _ffn_body(mixer_tail, *refs):
    if mixer_tail:
        (x_ref, mod_ref, wg_ref, wu_ref, wo_ref, lng_ref, lnb_ref,
         z_ref, g_ref, wm_ref, lng0_ref, lnb0_ref, o_ref, hin_ref, acc_ref, xm_ref) = refs
    else:
        x_ref, mod_ref, wg_ref, wu_ref, wo_ref, lng_ref, lnb_ref, o_ref, hin_ref, acc_ref = refs
    j = pl.program_id(2)

    @pl.when(j == 0)
    def _():
        x = x_ref[0]
        if mixer_tail:
            y = _mm((z_ref[0] * g_ref[0]).astype(BF16), wm_ref[...])
            x = _layernorm(DEEPNORM_ALPHA * x + (1.0 + mod_ref[0, 2:3, :]) * y, lng0_ref[...], lnb0_ref[...])
            xm_ref[...] = x
        hin_ref[...] = (x * (1.0 + mod_ref[0, 4:5, :]) + mod_ref[0, 3:4, :]).astype(BF16)
        acc_ref[...] = jnp.zeros_like(acc_ref)

    hin = hin_ref[...]
    gate = _mm(hin, wg_ref[...])
    up = _mm(hin, wu_ref[...])
    hid = (gate * _sigmoid(gate) * up).astype(BF16)
    acc_ref[...] += _mm(hid, wo_ref[...])

    @pl.when(j == pl.num_programs(2) - 1)
    def _():
        x = xm_ref[...] if mixer_tail else x_ref[0]
        res = DEEPNORM_ALPHA * x + (1.0 + mod_ref[0, 5:6, :]) * acc_ref[...]
        o_ref[0] = _layernorm(res, lng_ref[...], lnb_ref[...])


def _ffn(x, mod, w_in, w_out, ln_g, ln_b, mixer_tail=None):
    bsz, t, d = x.shape
    f = w_out.shape[0]
    nf = f // FFN_TF
    row = pl.BlockSpec((1, FFN_TM, d), lambda b, i, j: (b, i, 0))
    vec = pl.BlockSpec((1, d), lambda b, i, j: (0, 0))
    ins = [x, mod, w_in, w_in, w_out, ln_g.reshape(1, d), ln_b.reshape(1, d)]
    specs = [row,
             pl.BlockSpec((1, 6, d), lambda b, i, j: (b, 0, 0)),
             pl.BlockSpec((d, FFN_TF), lambda b, i, j: (0, j)),
             pl.BlockSpec((d, FFN_TF), lambda b, i, j: (0, nf + j)),
             pl.BlockSpec((FFN_TF, d), lambda b, i, j: (j, 0)),
             vec, vec]
    scratch = [pltpu.VMEM((FFN_TM, d), BF16), pltpu.VMEM((FFN_TM, d), F32)]
    if mixer_tail is not None:
        z, g, w_mix, ln_g0, ln_b0 = mixer_tail
        ins += [z, g, w_mix, ln_g0.reshape(1, d), ln_b0.reshape(1, d)]
        specs += [row, row, pl.BlockSpec((d, d), lambda b, i, j: (0, 0)), vec, vec]
        scratch += [pltpu.VMEM((FFN_TM, d), F32)]
    return pl.pallas_call(
        functools.partial(_ffn_body, mixer_tail is not None),
        grid=(bsz, t // FFN_TM, nf),
        in_specs=specs,
        out_specs=row,
        out_shape=jax.ShapeDtypeStruct((bsz, t, d), F32),
        scratch_shapes=scratch,
        compiler_params=_cparams(("parallel", "parallel", "arbitrary")),
        name="ffn",
    )(*ins)


def _rwkv_pre_body(has_vres, *refs):
    if has_vres:
        (x_ref, xp_ref, mod_ref, mu_ref, wr_ref, wk_ref, wv_ref, w1_ref, w2_ref, a1_ref, a2_ref,
         g1_ref, g2_ref, vec_ref, seg_ref, segt_ref, tri_ref, vf_ref, v1_ref, v2_ref,
         r_o, k_o, a_o, b_o, v_o, g_o, gl_o) = refs
    else:
        (x_ref, xp_ref, mod_ref, mu_ref, wr_ref, wk_ref, wv_ref, w1_ref, w2_ref, a1_ref, a2_ref,
         g1_ref, g2_ref, vec_ref, seg_ref, segt_ref, tri_ref,
         r_o, k_o, a_o, b_o, v_o, g_o, gl_o) = refs
    i = pl.program_id(1)
    sc = 1.0 + mod_ref[0, 1:2, :]
    sh = mod_ref[0, 0:1, :]
    hin = x_ref[0] * sc + sh
    tm = hin.shape[0]
    prev_row = xp_ref[0, 7:8, :] * sc + sh
    prev_row = jnp.where(i == 0, 0.0, prev_row)
    rows = lax.broadcasted_iota(jnp.int32, hin.shape, 0)
    hprev = jnp.where(rows == 0, prev_row, pltpu.roll(hin, 1, 0))
    xx = hprev - hin

    def mix(p):
        return hin + xx * mu_ref[p:p + 1, :]

    xr, xk, xv = mix(0).astype(BF16), mix(1).astype(BF16), mix(2).astype(BF16)
    xw, xa, xg = mix(3).astype(BF16), mix(4).astype(BF16), mix(5).astype(BF16)
    w0, a0, kkw, kaw = vec_ref[0:1, :], vec_ref[1:2, :], vec_ref[2:3, :], vec_ref[3:4, :]

    r = _mm(xr, wr_ref[...])
    k = _mm(xk, wk_ref[...])
    v = _mm(xv, wv_ref[...])

    wl = w0 + _mm(jnp.tanh(_mm(xw, w1_ref[...])).astype(BF16), w2_ref[...])
    nz = -wl
    softplus = jnp.maximum(nz, 0.0) + jnp.log(1.0 + jnp.exp(-jnp.abs(nz)))
    logdec = -jnp.exp(-softplus - 0.5)

    a = _sigmoid(a0 + _mm(_mm(xa, a1_ref[...]).astype(BF16), a2_ref[...]))
    if has_vres:
        v0 = vec_ref[4:5, :]
        vmix = _sigmoid(v0 + _mm(_mm(xv, v1_ref[...]).astype(BF16), v2_ref[...]))
        v = v + (vf_ref[0] - v) * vmix
    g = _mm(_sigmoid(_mm(xg, g1_ref[...])).astype(BF16), g2_ref[...])

    kk = k * kkw
    ss = _dot_hilo_rhs(kk * kk, seg_ref[...])
    nrm = jnp.maximum(jnp.sqrt(ss), 1e-12)
    inv = _dot_hilo_rhs(1.0 / nrm, segt_ref[...])
    kk = kk * inv
    k = k * (1.0 + (a - 1.0) * kaw)

    cum = _dot_exact_rhs_lhs(tri_ref[...], logdec)
    ginc = jnp.exp(cum)
    ginv = jnp.exp(-cum)
    gprev = jnp.exp(cum - logdec)

    r_o[0] = r * ginc
    k_o[0] = k * ginv
    a_o[0] = -kk * gprev
    b_o[0] = kk * a * ginv
    v_o[0] = v
    g_o[0] = g
    for cc in range(tm // CHUNK):
        first, last = cc * CHUNK, cc * CHUNK + CHUNK - 1
        gl_o[0, cc, 0:1] = jnp.exp(logdec[first:first + 1, :] - cum[first:first + 1, :])
        gl_o[0, cc, 1:2] = ginc[last:last + 1, :]


def _dot_exact_rhs_lhs(m_bf16, x):
    h, mid, l = _split3(x)
    return _mm(m_bf16, h) + (_mm(m_bf16, mid) + _mm(m_bf16, l))


def _rwkv_pre(x, mod, p, v_first):
    bsz, t, d = x.shape
    tm = PRE_TM
    has_vres = v_first is not None
    row = pl.BlockSpec((1, tm, d), lambda b, i: (b, i, 0))

    def full(shape):
        return pl.BlockSpec(shape, lambda b, i: (0,) * len(shape))

    heads = d // RWKV_HEAD
    seg = np.zeros((d, LANES), np.float32)
    seg[np.arange(d), np.arange(d) // RWKV_HEAD] = 1.0
    idx = np.arange(tm)
    same = idx[:, None] // CHUNK == idx[None, :] // CHUNK
    col = idx[None, :] % CHUNK
    tri = (same & (col <= idx[:, None] % CHUNK)).astype(np.float32) \
        - (same & (col <= CHUNK // 2 - 1)).astype(np.float32)
    vec_rows = [p['w0'], p['a0'], p['k_k'], p['k_a']] + ([p['v0']] if has_vres else [])
    vec = jnp.stack(vec_rows + [jnp.zeros_like(p['w0'])] * (8 - len(vec_rows)))
    dl, da, dg = p['w1'].shape[1], p['a1'].shape[1], p['g1'].shape[1]
    ins = [x, x, mod, p['mu'], p['wr'], p['wk'], p['wv'], p['w1'], p['w2'], p['a1'], p['a2'],
           p['g1'], p['g2'], vec, jnp.asarray(seg, BF16), jnp.asarray(seg.T, BF16),
           jnp.asarray(tri, BF16)]
    specs = [row,
             pl.BlockSpec((1, 8, d), lambda b, i: (b, jnp.maximum(i * (tm // 8) - 1, 0), 0)),
             pl.BlockSpec((1, 6, d), lambda b, i: (b, 0, 0)),
             full((6, d)), full((d, d)), full((d, d)), full((d, d)),
             full((d, dl)), full((dl, d)), full((d, da)), full((da, d)),
             full((d, dg)), full((dg, d)), full((8, d)),
             full((d, LANES)), full((LANES, d)), full((tm, tm))]
    if has_vres:
        dv = p['v1'].shape[1]
        ins += [v_first, p['v1'], p['v2']]
        specs += [row, full((d, dv)), full((dv, d))]
    act = jax.ShapeDtypeStruct((bsz, t, d), F32)
    nch = t // CHUNK
    outs = pl.pallas_call(
        functools.partial(_rwkv_pre_body, has_vres),
        grid=(bsz, t // tm),
        in_specs=specs,
        out_specs=[row] * 6 + [pl.BlockSpec((1, tm // CHUNK, 2, d), lambda b, i: (b, i, 0, 0))],
        out_shape=[act] * 6 + [jax.ShapeDtypeStruct((bsz, nch, 2, d), F32)],
        compiler_params=_cparams(("parallel", "parallel")),
        name="rwkv_pre",
    )(*ins)
    return outs


_BNN = (((2,), (1,)), ((0,), (0,)))
_BNT = (((2,), (2,)), ((0,), (0,)))


def _rwkv_scan_body(r_ref, k_ref, a_ref, b_ref, v_ref, gl_ref, vec_ref, o_ref, s_ref):
    c = pl.program_id(1)

    @pl.when(c == 0)
    def _():
        s_ref[...] = jnp.zeros_like(s_ref)

    L = CHUNK
    NP = RWKV_HEADS // 2
    shp = (NP, L, LANES)
    plane = lax.broadcasted_iota(jnp.int32, (1, L, 2 * L), 2)
    ph0 = plane < L
    tt = lax.broadcasted_iota(jnp.int32, (1, L, 2 * L), 1)
    ss = plane & (L - 1)
    strict = ss < tt
    incl = ss <= tt
    eye = jnp.where(ss == tt, 1.0, 0.0)
    base_bits = 4

    def same_block(bits):
        return (ss >> bits) == (tt >> bits)

    lvl0 = strict & same_block(base_bits)
    merges = [strict & same_block(bits + 1) & ((ss >> bits) != (tt >> bits))
              for bits in range(base_bits, (L - 1).bit_length())]
    dh0 = lax.broadcasted_iota(jnp.int32, (1, L, LANES), 2) < RWKV_HEAD
    ri = lax.broadcasted_iota(jnp.int32, (1, LANES, LANES), 1)
    ci = lax.broadcasted_iota(jnp.int32, (1, LANES, LANES), 2)
    blockdiag = (ri >> 6) == (ci >> 6)
    seg_mean = jnp.where(blockdiag[0], 1.0, 0.0).astype(BF16)

    def grp(ref):
        return jnp.stack([ref[0, :, hp * LANES:(hp + 1) * LANES] for hp in range(NP)])

    def bd(x):
        return jnp.concatenate([jnp.where(dh0, x, 0.0), jnp.where(dh0, 0.0, x)], axis=1)

    def bdp(y):
        return jnp.concatenate([jnp.where(ph0, y, 0.0), jnp.where(ph0, 0.0, y)], axis=1)

    def pmm(xp, y):
        return _dot1(xp, bd(y), _BNN)

    def ppm(xp, yp):
        return _dot1(xp, bdp(yp), _BNN)

    R, K, A, Bv, V = grp(r_ref), grp(k_ref), grp(a_ref), grp(b_ref), grp(v_ref)
    gvec = lambda n: jnp.stack([gl_ref[0, 0, n:n + 1, hp * LANES:(hp + 1) * LANES] for hp in range(NP)])
    S = s_ref[...] * gvec(0)

    AR = jnp.concatenate([A, R], axis=1)
    BK = jnp.concatenate([bd(Bv), bd(K)], axis=1)
    G = _dot1(AR, BK, _BNT)
    A_ab = jnp.where(strict, G[:, 0:L, 0:2 * L], 0.0)
    A_ak = jnp.where(strict, G[:, 0:L, 2 * L:4 * L], 0.0)
    A_rb = jnp.where(incl, G[:, L:2 * L, 0:2 * L], 0.0)
    A_rk = jnp.where(incl, G[:, L:2 * L, 2 * L:4 * L], 0.0)

    a0 = jnp.where(lvl0, A_ab, 0.0)
    Tm = eye + a0
    P = a0
    for _ in range(base_bits - 1):
        P = ppm(P, P)
        Tm = Tm + ppm(Tm, P)
    for lvl in merges:
        Tm = Tm + ppm(ppm(Tm, jnp.where(lvl, A_ab, 0.0)), Tm)

    PQ = _dot1(AR, S, _BNT)
    W = PQ[:, 0:L] + pmm(A_ak, V)
    U = pmm(Tm, W)
    Y = PQ[:, L:2 * L] + _dot1(jnp.concatenate([A_rb, A_rk], axis=2),
                               jnp.concatenate([bd(U), bd(V)], axis=1), _BNN)
    UV = jnp.concatenate([U, V], axis=1)
    UVt = jnp.stack([UV[hp].T for hp in range(NP)])
    BK2 = jnp.concatenate([Bv, K], axis=1)
    upd = _dot1(UVt, BK2, _BNN)
    s_ref[...] = (S + jnp.where(blockdiag, upd, 0.0)) * gvec(1)

    vrow = lambda n: jnp.stack([vec_ref[n:n + 1, hp * LANES:(hp + 1) * LANES] for hp in range(NP)])
    inv_n = 1.0 / RWKV_HEAD
    flat = lambda x: x.reshape(NP * L, LANES)
    mean = _dot_hilo_rhs(flat(Y), seg_mean).reshape(shp) * inv_n
    yc = Y - mean
    var = _dot_hilo_rhs(flat(yc * yc), seg_mean).reshape(shp) * inv_n
    yn = yc * lax.rsqrt(var + GN_EPS) * vrow(1) + vrow(2)
    bonus = _dot_hilo_rhs(flat(R * K * vrow(0)), seg_mean).reshape(shp)
    out = yn + bonus * V
    for hp in range(NP):
        o_ref[0, :, hp * LANES:(hp + 1) * LANES] = out[hp]


def _rwkv_scan(r, k, a, b, v, gl, r_k, lnx_g, lnx_b):
    bsz, t, d = r.shape
    row = pl.BlockSpec((1, CHUNK, d), lambda bb, c: (bb, c, 0))
    vec = jnp.stack([r_k.reshape(d), lnx_g, lnx_b] + [jnp.zeros((d,), F32)] * 5)
    return pl.pallas_call(
        _rwkv_scan_body,
        grid=(bsz, t // CHUNK),
        in_specs=[row] * 5 + [
            pl.BlockSpec((1, 1, 2, d), lambda bb, c: (bb, c, 0, 0)),
            pl.BlockSpec((8, d), lambda bb, c: (0, 0)),
        ],
        out_specs=row,
        out_shape=jax.ShapeDtypeStruct((bsz, t, d), F32),
        scratch_shapes=[pltpu.VMEM((RWKV_HEADS // 2, LANES, LANES), F32)],
        compiler_params=_cparams(("parallel", "arbitrary")),
        name="rwkv_scan",
    )(r, k, a, b, v, gl, vec)


IDX_COLS = 768


def _dsa_proj_body(x_ref, mod_ref, wq_ref, wc_ref, wi_ref, kvn_ref,
                   q_o, ckv_o, qi_o, ki_o, wi_o):
    hin = (x_ref[0] * (1.0 + mod_ref[0, 1:2, :]) + mod_ref[0, 0:1, :]).astype(BF16)
    q_o[0] = _mm(hin, wq_ref[...]).astype(BF16)
    ckv = _mm(hin, wc_ref[...])
    ms = jnp.mean(ckv * ckv, axis=-1, keepdims=True)
    ckv_o[0] = (ckv * lax.rsqrt(ms + 1e-6) * kvn_ref[...]).astype(BF16)
    idx = _mm(hin, wi_ref[...])
    nq = IDX_HEADS * IDX_DIM
    qi_o[0] = idx[:, 0:nq].astype(BF16)
    ki_o[0] = idx[:, nq:nq + LANES].astype(BF16)
    wi_o[0] = idx[:, nq + LANES:nq + 2 * LANES] * (IDX_HEADS ** -0.5 * IDX_DIM ** -0.5)


def _dsa_proj(x, mod, w_in, kv_norm):
    bsz, t, d = x.shape
    c1 = ATT_HEADS * ATT_HEAD_DIM
    c2 = c1 + KV_LATENT
    c3 = c2 + IDX_HEADS * IDX_DIM
    c4 = c3 + IDX_DIM
    wq = w_in[:, :c1].astype(BF16)
    wc = w_in[:, c1:c2].astype(BF16)
    pad = IDX_COLS - (c3 - c2) - 2 * IDX_DIM - IDX_HEADS
    widx = jnp.concatenate([w_in[:, c2:c3], w_in[:, c3:c4], w_in[:, c3:c4], w_in[:, c4:],
                            jnp.zeros((d, pad), F32)], axis=1).astype(BF16)
    tm = PROJ_TM

    def full(shape):
        return pl.BlockSpec(shape, lambda b, i: (0,) * len(shape))

    def row(n):
        return pl.BlockSpec((1, tm, n), lambda b, i: (b, i, 0))

    def act(n, dtype):
        return jax.ShapeDtypeStruct((bsz, t, n), dtype)

    nq = IDX_HEADS * IDX_DIM
    return pl.pallas_call(
        _dsa_proj_body,
        grid=(bsz, t // tm),
        in_specs=[row(d), pl.BlockSpec((1, 6, d), lambda b, i: (b, 0, 0)),
                  full((d, c1)), full((d, KV_LATENT)), full((d, IDX_COLS)),
                  full((1, KV_LATENT))],
        out_specs=[row(c1), row(KV_LATENT), row(nq), row(LANES), row(LANES)],
        out_shape=[act(c1, BF16), act(KV_LATENT, BF16), act(nq, BF16), act(LANES, BF16), act(LANES, F32)],
        compiler_params=_cparams(("parallel", "parallel")),
        name="dsa_proj",
    )(x, mod, wq, wc, widx, kv_norm.reshape(1, KV_LATENT))


def _dsa_index_body(k_sel, qi_ref, wi_ref, ki_ref, o_ref, key_ref):
    i = pl.program_id(1)
    tq = qi_ref.shape[1]
    t = ki_ref.shape[1]
    nchunk = t // tq
    wt = wi_ref[0].T
    lane = lax.broadcasted_iota(jnp.int32, (tq, LANES), 1)
    first = lane < IDX_DIM
    qheads = []
    for hp in range(IDX_HEADS // 2):
        qp = qi_ref[0, :, hp * LANES:(hp + 1) * LANES].astype(F32)
        qheads.append(jnp.where(first, qp, 0.0).astype(BF16))
        qheads.append(jnp.where(first, 0.0, qp).astype(BF16))
    q_all = jnp.concatenate(qheads, axis=0)
    krow = lax.broadcasted_iota(jnp.int32, (tq, tq), 0)
    qpos = i * tq + lax.broadcasted_iota(jnp.int32, (tq, tq), 1)

    def chunk_start(c):
        return pl.multiple_of(c * tq, tq)

    def score_chunk(c, carry):
        k0 = chunk_start(c)
        kk2 = ki_ref[0, pl.ds(k0, tq), :]
        score = jnp.zeros((tq, tq), F32)
        s_all = _mm(kk2, q_all, _NT)
        for h in range(IDX_HEADS):
            score = score + wt[h:h + 1, :] * jnp.maximum(s_all[:, h * tq:(h + 1) * tq], 0.0)
        score = jnp.where(score == 0.0, 0.0, score)
        bits = pltpu.bitcast(score, jnp.int32)
        skey = bits ^ ((bits >> 31) & 0x7FFFFFFF)
        key_ref[pl.ds(k0, tq), :] = jnp.where(k0 + krow <= qpos, skey, INT_MIN)
        return carry

    lax.fori_loop(0, i + 1, score_chunk, 0)

    def count(fn):
        def body(c, acc):
            k0 = chunk_start(c)
            ind = fn(key_ref[pl.ds(k0, tq), :], k0 + krow)
            part = jnp.sum(ind.reshape(tq // 32, 4, 8, tq), axis=0)
            return acc + jnp.sum(part, axis=0)
        acc = lax.fori_loop(0, i + 1, body, jnp.zeros((8, tq), F32))
        return jnp.sum(acc, axis=0, keepdims=True)

    kf = float(k_sel)

    def search(nc):
        def count_ge(cand):
            acc = jnp.zeros((8, tq), F32)
            for c in range(nc):
                ind = jnp.where(key_ref[c * tq:(c + 1) * tq, :] >= cand, 1.0, 0.0)
                acc = acc + jnp.sum(jnp.sum(ind.reshape(tq // 32, 4, 8, tq), axis=0), axis=0)
            return jnp.sum(acc, axis=0, keepdims=True)

        thr0 = jnp.where(count_ge(jnp.zeros((1, tq), jnp.int32)) >= kf, 0, INT_MIN).astype(jnp.int32)

        def thr_step(n, thr):
            cand = thr | jnp.left_shift(jnp.int32(1), 30 - n)
            return jnp.where(count_ge(cand) >= kf, cand, thr)

        return lax.fori_loop(0, 31, thr_step, thr0)

    thr = lax.switch(i, [functools.partial(search, nc + 1) for nc in range(nchunk)])
    n_gt = count(lambda keys, kpos: jnp.where(keys > thr, 1.0, 0.0))
    n_eq = count(lambda keys, kpos: jnp.where(keys == thr, 1.0, 0.0))
    need = kf - n_gt
    nbits = int(t - 1).bit_length()

    def tie_cut():
        def cut_step(n, cut):
            cand = cut | jnp.left_shift(jnp.int32(1), nbits - 1 - n)
            cnt = count(lambda keys, kpos: jnp.where(keys == thr, jnp.where(kpos < cand, 1.0, 0.0), 0.0))
            return jnp.where(cnt < need, cand, cut)
        return lax.fori_loop(0, nbits, cut_step, jnp.zeros((1, tq), jnp.int32))

    cut = lax.cond(jnp.max(n_eq - need) > 0.0, tie_cut, lambda: jnp.full((1, tq), t, jnp.int32))

    def write_chunk(c, carry):
        k0 = chunk_start(c)
        keys = key_ref[pl.ds(k0, tq), :]
        kpos = k0 + krow
        tie = jnp.where(keys == thr, jnp.where(kpos <= cut, 0.0, MASK_NEG), MASK_NEG)
        bias = jnp.where(keys > thr, 0.0, tie)
        o_ref[0, pl.ds(k0, tq), :] = jnp.where(kpos <= qpos, bias, MASK_NEG).astype(BF16)
        return carry

    lax.fori_loop(0, i + 1, write_chunk, 0)

    def masked_chunk(c, carry):
        o_ref[0, pl.ds(chunk_start(c), tq), :] = jnp.full((tq, tq), MASK_NEG, BF16)
        return carry

    lax.fori_loop(i + 1, nchunk, masked_chunk, 0)


def _dsa_index(qi, ki, wi, k_sel):
    bsz, t, nq = qi.shape
    tq = ATT_T
    return pl.pallas_call(
        functools.partial(_dsa_index_body, k_sel),
        grid=(bsz, t // tq),
        in_specs=[pl.BlockSpec((1, tq, nq), lambda b, i: (b, i, 0)),
                  pl.BlockSpec((1, tq, LANES), lambda b, i: (b, i, 0)),
                  pl.BlockSpec((1, t, LANES), lambda b, i: (b, 0, 0))],
        out_specs=pl.BlockSpec((1, t, tq), lambda b, i: (b, 0, i)),
        out_shape=jax.ShapeDtypeStruct((bsz, t, t), BF16),
        scratch_shapes=[pltpu.VMEM((t, tq), jnp.int32)],
        compiler_params=_cparams(("parallel", "parallel")),
        name="dsa_index",
    )(qi, wi, ki)


def _t5_bucket_np(n):
    n = np.maximum(n, 0)
    max_exact = REL_BUCKETS // 2
    nf = np.maximum(n, 1).astype(np.float32)
    large = max_exact + (np.log(nf / np.float32(max_exact)) / np.float32(math.log(REL_MAX_DIST / max_exact))
                         * np.float32(REL_BUCKETS - max_exact)).astype(np.int32)
    large = np.minimum(large, REL_BUCKETS - 1)
    return np.where(n < max_exact, n, large).astype(np.int32)


def _band_body(bkt_ref, rb_ref, o_ref):
    h = pl.program_id(1)
    bkt = bkt_ref[0]
    acc = jnp.zeros(bkt.shape, F32)
    for b in range(REL_BUCKETS):
        acc = jnp.where(bkt == b, rb_ref[b, h], acc)
    o_ref[0] = acc * LOG2E


def _band_bias(rel_bias):
    tt = ATT_T
    kc = np.arange(tt)[:, None]
    qr = np.arange(tt)[None, :]
    planes = [_t5_bucket_np(d * tt + qr - kc) for d in range(3)]
    assert (planes[2] == REL_BUCKETS - 1).all() and tt + 1 >= 113
    bkt = jnp.asarray(np.stack(planes))
    return pl.pallas_call(
        _band_body,
        grid=(3, ATT_HEADS),
        in_specs=[pl.BlockSpec((1, tt, tt), lambda d, h: (d, 0, 0)),
                  pl.BlockSpec(memory_space=pltpu.SMEM)],
        out_specs=pl.BlockSpec((1, tt, tt), lambda d, h: (d, 0, h)),
        out_shape=jax.ShapeDtypeStruct((3, tt, ATT_HEADS * tt), F32),
        compiler_params=_cparams(("parallel", "parallel")),
        name="band_bias",
    )(bkt, rel_bias)


LOG2E = 1.4426950408889634
ACC_ROWS = KV_LATENT + 16


def _dsa_attn_body(qt_ref, kt_ref, q_ref, ckv_ref, mask_ref, wuk_ref, wuv_ref, near_ref, far_ref,
                   x_ref, mod_ref, wout_ref, lng_ref, lnb_ref, o_ref,
                   ql_ref, m_ref, acc_ref, p_ref, ot_ref):
    i = qt_ref[pl.program_id(1)]
    j = kt_ref[pl.program_id(1)]
    nh = ATT_HEADS
    tq = q_ref.shape[1]
    qscale = ATT_HEAD_DIM ** -0.5 * LOG2E

    @pl.when(j == 0)
    def _():
        for hp in range(nh // 2):
            qp = q_ref[0, :, hp * LANES:(hp + 1) * LANES].astype(BF16)
            qlat = _mm(qp, wuk_ref[hp]) * qscale
            ql_ref[2 * hp * tq:(2 * hp + 1) * tq, :] = qlat[:, 0:KV_LATENT].astype(BF16)
            ql_ref[(2 * hp + 1) * tq:(2 * hp + 2) * tq, :] = qlat[:, KV_LATENT:2 * KV_LATENT].astype(BF16)
        m_ref[...] = jnp.full(m_ref.shape, MASK_NEG, F32)
        acc_ref[...] = jnp.zeros_like(acc_ref)

    def step(near):
        ckv = ckv_ref[0]
        tk = ckv.shape[0]
        s_all = _mm(ckv, ql_ref[...], _NT)
        ckv_aug = jnp.concatenate([ckv.astype(F32).T, jnp.ones((ACC_ROWS - KV_LATENT, tk), F32)],
                                  axis=0).astype(BF16)
        maskb = mask_ref[0].astype(F32)
        m_prev = m_ref[...]
        for h in range(nh):
            hs = slice(h * tq, (h + 1) * tq)
            s = s_all[:, hs] + maskb
            if near:
                s = s + near_ref[i - j, :, hs]
                m_new = jnp.maximum(m_prev[:, hs], jnp.max(s, axis=0, keepdims=True))
                shift = m_new
            else:
                cvec = far_ref[:, hs]
                m_new = jnp.maximum(m_prev[:, hs], jnp.max(s, axis=0, keepdims=True) + cvec)
                shift = m_new - cvec
            p_ref[:, hs] = jnp.exp2(s - shift).astype(BF16)
            m_ref[:, hs] = m_new
        alpha = jnp.exp2(m_prev - m_ref[...])
        acc_ref[...] = alpha * acc_ref[...] + _mm(ckv_aug, p_ref[...])

    @pl.when((j <= i) & (i - j < 2))
    def _():
        step(True)

    @pl.when(i - j >= 2)
    def _():
        step(False)

    @pl.when(j == i)
    def _():
        def norm(h):
            a = acc_ref[:, h * tq:(h + 1) * tq]
            return a[0:KV_LATENT] * (1.0 / a[KV_LATENT:KV_LATENT + 1])
        for hp in range(nh // 2):
            olat = jnp.concatenate([norm(2 * hp), norm(2 * hp + 1)], axis=0).astype(BF16)
            ot_ref[hp * LANES:(hp + 1) * LANES, :] = _mm(wuv_ref[hp], olat)
        y = _mm(ot_ref[...].T.astype(BF16), wout_ref[...])
        res = DEEPNORM_ALPHA * x_ref[0] + (1.0 + mod_ref[0, 2:3, :]) * y
        o_ref[0] = _layernorm(res, lng_ref[...], lnb_ref[...])


def _dsa_attn(q, ckv, maskt, w_uk, w_uv, band, x, mod, w_out, ln_g, ln_b):
    bsz, t, d = q.shape
    tt = ATT_T
    nt = t // tt
    nh = ATT_HEADS
    zk = jnp.zeros((nh // 2, ATT_HEAD_DIM, KV_LATENT), F32)
    wuk2 = jnp.concatenate([jnp.concatenate([w_uk[0::2], zk], axis=2),
                            jnp.concatenate([zk, w_uk[1::2]], axis=2)], axis=1).astype(BF16)
    wuv_t = jnp.swapaxes(w_uv, 1, 2)
    zv = jnp.zeros((nh // 2, ATT_HEAD_DIM, KV_LATENT), F32)
    wuv2 = jnp.concatenate([jnp.concatenate([wuv_t[0::2], zv], axis=2),
                            jnp.concatenate([zv, wuv_t[1::2]], axis=2)], axis=1).astype(BF16)
    pairs = [(i, j) for i in range(nt) for j in range(i + 1)]
    q_tab = jnp.asarray([p[0] for p in pairs], jnp.int32)
    k_tab = jnp.asarray([p[1] for p in pairs], jnp.int32)
    grid_spec = pltpu.PrefetchScalarGridSpec(
        num_scalar_prefetch=2,
        grid=(bsz, len(pairs)),
        in_specs=[pl.BlockSpec((1, tt, d), lambda b, s, qt, kt: (b, qt[s], 0)),
                  pl.BlockSpec((1, tt, KV_LATENT), lambda b, s, qt, kt: (b, kt[s], 0)),
                  pl.BlockSpec((1, tt, tt), lambda b, s, qt, kt: (b, kt[s], qt[s])),
                  pl.BlockSpec((nh // 2, LANES, 2 * KV_LATENT), lambda b, s, qt, kt: (0, 0, 0)),
                  pl.BlockSpec((nh // 2, LANES, 2 * KV_LATENT), lambda b, s, qt, kt: (0, 0, 0)),
                  pl.BlockSpec((2, tt, nh * tt), lambda b, s, qt, kt: (0, 0, 0)),
                  pl.BlockSpec((1, nh * tt), lambda b, s, qt, kt: (0, 0)),
                  pl.BlockSpec((1, tt, d), lambda b, s, qt, kt: (b, qt[s], 0)),
                  pl.BlockSpec((1, 6, d), lambda b, s, qt, kt: (b, 0, 0)),
                  pl.BlockSpec((d, d), lambda b, s, qt, kt: (0, 0)),
                  pl.BlockSpec((1, d), lambda b, s, qt, kt: (0, 0)),
                  pl.BlockSpec((1, d), lambda b, s, qt, kt: (0, 0))],
        out_specs=pl.BlockSpec((1, tt, d), lambda b, s, qt, kt: (b, qt[s], 0)),
        scratch_shapes=[pltpu.VMEM((nh * tt, KV_LATENT), BF16),
                        pltpu.VMEM((1, nh * tt), F32),
                        pltpu.VMEM((ACC_ROWS, nh * tt), F32),
                        pltpu.VMEM((tt, nh * tt), BF16),
                        pltpu.VMEM((d, tt), F32)])
    return pl.pallas_call(
        _dsa_attn_body,
        grid_spec=grid_spec,
        out_shape=jax.ShapeDtypeStruct((bsz, t, d), F32),
        compiler_params=_cparams(("parallel", "arbitrary")),
        name="dsa_attn",
    )(q_tab, k_tab, q, ckv, maskt, wuk2, wuv2, band[:2], band[2, 0:1, :],
      x, mod, w_out, ln_g.reshape(1, d), ln_b.reshape(1, d))


def kernel(x, c, ada_w, ada_b, ln_g, ln_b, ffn_w_in, ffn_w_out, rwkv_mu, rwkv_w_rkv, rwkv_w0, rwkv_w1, rwkv_w2, rwkv_a0, rwkv_a1, rwkv_a2, rwkv_v0, rwkv_v1, rwkv_v2, rwkv_g1, rwkv_g2, rwkv_k_k, rwkv_k_a, rwkv_r_k, rwkv_lnx_g, rwkv_lnx_b, rwkv_w_out, dsa_w_in, dsa_kv_norm, dsa_w_uk, dsa_w_uv, dsa_w_out, rel_bias):
    bsz, t, d = x.shape
    assert d == D_MODEL and t % ATT_T == 0 and t % FFN_TM == 0 and t % PRE_TM == 0
    mod_all = _adaln(c, ada_w, ada_b).reshape(DEPTH, bsz, 6, d)
    band = _band_bias(rel_bias)
    k_sel = min(TOPK_MAX, t // TOPK_DIV)
    bf = lambda w: w.astype(BF16)
    v_first = None
    for i in range(DEPTH):
        mod = mod_all[i]
        j = i // 2
        if i % 2 == 0:
            p = dict(mu=rwkv_mu[j], wr=bf(rwkv_w_rkv[j, 0]), wk=bf(rwkv_w_rkv[j, 1]), wv=bf(rwkv_w_rkv[j, 2]),
                     w0=rwkv_w0[j], w1=bf(rwkv_w1[j]), w2=bf(rwkv_w2[j]),
                     a0=rwkv_a0[j], a1=bf(rwkv_a1[j]), a2=bf(rwkv_a2[j]),
                     g1=bf(rwkv_g1[j]), g2=bf(rwkv_g2[j]), k_k=rwkv_k_k[j], k_a=rwkv_k_a[j])
            if j > 0:
                p.update(v0=rwkv_v0[j - 1], v1=bf(rwkv_v1[j - 1]), v2=bf(rwkv_v2[j - 1]))
            r_s, k_s, a_s, b_s, v, g, gl = _rwkv_pre(x, mod, p, v_first if j > 0 else None)
            if j == 0:
                v_first = v
            z = _rwkv_scan(r_s, k_s, a_s, b_s, v, gl, rwkv_r_k[j], rwkv_lnx_g[j], rwkv_lnx_b[j])
            tail = (z, g, bf(rwkv_w_out[j]), ln_g[i, 0], ln_b[i, 0])
        else:
            tail = None
            q, ckv, qi, ki, wi = _dsa_proj(x, mod, dsa_w_in[j], dsa_kv_norm[j])
            maskt = _dsa_index(qi, ki, wi, k_sel)
            x = _dsa_attn(q, ckv, maskt, dsa_w_uk[j], dsa_w_uv[j], band,
                          x, mod, bf(dsa_w_out[j]), ln_g[i, 0], ln_b[i, 0])
        x = _ffn(x, mod, bf(ffn_w_in[i]), bf(ffn_w_out[i]), ln_g[i, 1], ln_b[i, 1], tail)
    return x
```

```python
import functools
import math

import numpy as np
import jax
import jax.numpy as jnp
from jax import lax
from jax.experimental import pallas as pl
from jax.experimental.pallas import tpu as pltpu

F32 = jnp.float32
BF16 = jnp.bfloat16

D_MODEL = 1024
DEPTH = 4
RWKV_HEAD = 64
RWKV_HEADS = D_MODEL // RWKV_HEAD
GN_EPS = RWKV_HEAD * 1e-5
ATT_HEADS = 16
ATT_HEAD_DIM = 64
KV_LATENT = 128
IDX_HEADS = 8
IDX_DIM = 64
TOPK_MAX = 256
TOPK_DIV = 4
REL_BUCKETS = 32
REL_MAX_DIST = 128
FFN_HIDDEN = 2816
DEEPNORM_ALPHA = (2 * DEPTH) ** 0.25
LN_EPS = 1e-5

LANES = 128
CHUNK = 128
PRE_TM = 256
PROJ_TM = 512
FFN_TM = 512
FFN_TF = 1408
ATT_T = 256
MASK_NEG = -1e30
VMEM_LIMIT = 56 * 1024 * 1024
INT_MIN = -2 ** 31


def _cparams(sem):
    return pltpu.CompilerParams(dimension_semantics=sem, vmem_limit_bytes=VMEM_LIMIT)


def _split2(x):
    hi = x.astype(BF16)
    lo = (x - hi.astype(F32)).astype(BF16)
    return hi, lo


def _split3(x):
    hi = x.astype(BF16)
    r1 = x - hi.astype(F32)
    mid = r1.astype(BF16)
    lo = (r1 - mid.astype(F32)).astype(BF16)
    return hi, mid, lo


_NN = (((1,), (0,)), ((), ()))
_NT = (((1,), (1,)), ((), ()))


def _mm(a, b, dims=_NN):
    return lax.dot_general(a, b, dims, preferred_element_type=F32)


def _dot3(a, b, dims=_NN):
    ah, al = _split2(a)
    bh, bl = _split2(b)
    return _mm(ah, bh, dims) + (_mm(ah, bl, dims) + _mm(al, bh, dims))


def _dot1(a, b, dims=_NN):
    return _mm(a.astype(BF16), b.astype(BF16), dims)


def _dot_hilo_rhs(a, b_bf16):
    h, l = _split2(a)
    return _mm(h, b_bf16) + _mm(l, b_bf16)


def _sigmoid(x):
    return 1.0 / (1.0 + jnp.exp(-x))


def _layernorm(xr, g, b):
    mu = jnp.mean(xr, axis=-1, keepdims=True)
    xc = xr - mu
    var = jnp.mean(xc * xc, axis=-1, keepdims=True)
    return xc * lax.rsqrt(var + LN_EPS) * g + b


def _adaln_body(c_ref, w_ref, b_ref, o_ref):
    c = c_ref[...]
    cond = c * _sigmoid(c)
    o_ref[0] = _dot3(cond, w_ref[0]) + b_ref[0]


def _adaln(c, ada_w, ada_b):
    depth, d, n = ada_w.shape
    bsz = c.shape[0]
    tn = n // 4
    return pl.pallas_call(
        _adaln_body,
        grid=(depth, n // tn),
        in_specs=[
            pl.BlockSpec((bsz, d), lambda i, j: (0, 0)),
            pl.BlockSpec((1, d, tn), lambda i, j: (i, 0, j)),
            pl.BlockSpec((1, 1, tn), lambda i, j: (i, 0, j)),
        ],
        out_specs=pl.BlockSpec((1, bsz, tn), lambda i, j: (i, 0, j)),
        out_shape=jax.ShapeDtypeStruct((depth, bsz, n), F32),
        compiler_params=_cparams(("parallel", "parallel")),
        name="adaln",
    )(c, ada_w, ada_b.reshape(depth, 1, n))


def _ffn_body(mixer_tail, *refs):
    if mixer_tail:
        (x_ref, mod_ref, wg_ref, wu_ref, wo_ref, lng_ref, lnb_ref,
         z_ref, g_ref, wm_ref, lng0_ref, lnb0_ref, o_ref, hin_ref, acc_ref, xm_ref) = refs
    else:
        x_ref, mod_ref, wg_ref, wu_ref, wo_ref, lng_ref, lnb_ref, o_ref, hin_ref, acc_ref = refs
    j = pl.program_id(2)

    @pl.when(j == 0)
    def _():
        x = x_ref[0]
        if mixer_tail:
            y = _mm((z_ref[0] * g_ref[0]).astype(BF16), wm_ref[...])
            x = _layernorm(DEEPNORM_ALPHA * x + (1.0 + mod_ref[0, 2:3, :]) * y, lng0_ref[...], lnb0_ref[...])
            xm_ref[...] = x
        hin_ref[...] = (x * (1.0 + mod_ref[0, 4:5, :]) + mod_ref[0, 3:4, :]).astype(BF16)
        acc_ref[...] = jnp.zeros_like(acc_ref)

    hin = hin_ref[...]
    gate = _mm(hin, wg_ref[...])
    up = _mm(hin, wu_ref[...])
    hid = (gate * _sigmoid(gate) * up).astype(BF16)
    acc_ref[...] += _mm(hid, wo_ref[...])

    @pl.when(j == pl.num_programs(2) - 1)
    def _():
        x = xm_ref[...] if mixer_tail else x_ref[0]
        res = DEEPNORM_ALPHA * x + (1.0 + mod_ref[0, 5:6, :]) * acc_ref[...]
        o_ref[0] = _layernorm(res, lng_ref[...], lnb_ref[...])


def _ffn(x, mod, w_in, w_out, ln_g, ln_b, mixer_tail=None):
    bsz, t, d = x.shape
    f = w_out.shape[0]
    nf = f // FFN_TF
    row = pl.BlockSpec((1, FFN_TM, d), lambda b, i, j: (b, i, 0))
    vec = pl.BlockSpec((1, d), lambda b, i, j: (0, 0))
    ins = [x, mod, w_in, w_in, w_out, ln_g.reshape(1, d), ln_b.reshape(1, d)]
    specs = [row,
             pl.BlockSpec((1, 6, d), lambda b, i, j: (b, 0, 0)),
             pl.BlockSpec((d, FFN_TF), lambda b, i, j: (0, j)),
             pl.BlockSpec((d, FFN_TF), lambda b, i, j: (0, nf + j)),
             pl.BlockSpec((FFN_TF, d), lambda b, i, j: (j, 0)),
             vec, vec]
    scratch = [pltpu.VMEM((FFN_TM, d), BF16), pltpu.VMEM((FFN_TM, d), F32)]
    if mixer_tail is not None:
        z, g, w_mix, ln_g0, ln_b0 = mixer_tail
        ins += [z, g, w_mix, ln_g0.reshape(1, d), ln_b0.reshape(1, d)]
        specs += [row, row, pl.BlockSpec((d, d), lambda b, i, j: (0, 0)), vec, vec]
        scratch += [pltpu.VMEM((FFN_TM, d), F32)]
    return pl.pallas_call(
        functools.partial(_ffn_body, mixer_tail is not None),
        grid=(bsz, t // FFN_TM, nf),
        in_specs=specs,
        out_specs=row,
        out_shape=jax.ShapeDtypeStruct((bsz, t, d), F32),
        scratch_shapes=scratch,
        compiler_params=_cparams(("parallel", "parallel", "arbitrary")),
        name="ffn",
    )(*ins)


def _rwkv_pre_body(has_vres, *refs):
    if has_vres:
        (x_ref, xp_ref, mod_ref, mu_ref, wr_ref, wk_ref, wv_ref, w1_ref, w2_ref, a1_ref, a2_ref,
         g1_ref, g2_ref, vec_ref, seg_ref, segt_ref, tri_ref, vf_ref, v1_ref, v2_ref,
         r_o, k_o, a_o, b_o, v_o, g_o, gl_o) = refs
    else:
        (x_ref, xp_ref, mod_ref, mu_ref, wr_ref, wk_ref, wv_ref, w1_ref, w2_ref, a1_ref, a2_ref,
         g1_ref, g2_ref, vec_ref, seg_ref, segt_ref, tri_ref,
         r_o, k_o, a_o, b_o, v_o, g_o, gl_o) = refs
    i = pl.program_id(1)
    sc = 1.0 + mod_ref[0, 1:2, :]
    sh = mod_ref[0, 0:1, :]
    hin = x_ref[0] * sc + sh
    tm = hin.shape[0]
    prev_row = xp_ref[0, 7:8, :] * sc + sh
    prev_row = jnp.where(i == 0, 0.0, prev_row)
    rows = lax.broadcasted_iota(jnp.int32, hin.shape, 0)
    hprev = jnp.where(rows == 0, prev_row, pltpu.roll(hin, 1, 0))
    xx = hprev - hin

    def mix(p):
        return hin + xx * mu_ref[p:p + 1, :]

    xr, xk, xv = mix(0).astype(BF16), mix(1).astype(BF16), mix(2).astype(BF16)
    xw, xa, xg = mix(3).astype(BF16), mix(4).astype(BF16), mix(5).astype(BF16)
    w0, a0, kkw, kaw = vec_ref[0:1, :], vec_ref[1:2, :], vec_ref[2:3, :], vec_ref[3:4, :]

    r = _mm(xr, wr_ref[...])
    k = _mm(xk, wk_ref[...])
    v = _mm(xv, wv_ref[...])

    wl = w0 + _mm(jnp.tanh(_mm(xw, w1_ref[...])).astype(BF16), w2_ref[...])
    nz = -wl
    softplus = jnp.maximum(nz, 0.0) + jnp.log(1.0 + jnp.exp(-jnp.abs(nz)))
    logdec = -jnp.exp(-softplus - 0.5)

    a = _sigmoid(a0 + _mm(_mm(xa, a1_ref[...]).astype(BF16), a2_ref[...]))
    if has_vres:
        v0 = vec_ref[4:5, :]
        vmix = _sigmoid(v0 + _mm(_mm(xv, v1_ref[...]).astype(BF16), v2_ref[...]))
        v = v + (vf_ref[0] - v) * vmix
    g = _mm(_sigmoid(_mm(xg, g1_ref[...])).astype(BF16), g2_ref[...])

    kk = k * kkw
    ss = _dot_hilo_rhs(kk * kk, seg_ref[...])
    nrm = jnp.maximum(jnp.sqrt(ss), 1e-12)
    inv = _dot_hilo_rhs(1.0 / nrm, segt_ref[...])
    kk = kk * inv
    k = k * (1.0 + (a - 1.0) * kaw)

    cum = _dot_exact_rhs_lhs(tri_ref[...], logdec)
    ginc = jnp.exp(cum)
    ginv = jnp.exp(-cum)
    gprev = jnp.exp(cum - logdec)

    r_o[0] = r * ginc
    k_o[0] = k * ginv
    a_o[0] = -kk * gprev
    b_o[0] = kk * a * ginv
    v_o[0] = v
    g_o[0] = g
    for cc in range(tm // CHUNK):
        first, last = cc * CHUNK, cc * CHUNK + CHUNK - 1
        gl_o[0, cc, 0:1] = jnp.exp(logdec[first:first + 1, :] - cum[first:first + 1, :])
        gl_o[0, cc, 1:2] = ginc[last:last + 1, :]


def _dot_exact_rhs_lhs(m_bf16, x):
    h, mid, l = _split3(x)
    return _mm(m_bf16, h) + (_mm(m_bf16, mid) + _mm(m_bf16, l))


def _rwkv_pre(x, mod, p, v_first):
    bsz, t, d = x.shape
    tm = PRE_TM
    has_vres = v_first is not None
    row = pl.BlockSpec((1, tm, d), lambda b, i: (b, i, 0))

    def full(shape):
        return pl.BlockSpec(shape, lambda b, i: (0,) * len(shape))

    heads = d // RWKV_HEAD
    seg = np.zeros((d, LANES), np.float32)
    seg[np.arange(d), np.arange(d) // RWKV_HEAD] = 1.0
    idx = np.arange(tm)
    same = idx[:, None] // CHUNK == idx[None, :] // CHUNK
    col = idx[None, :] % CHUNK
    tri = (same & (col <= idx[:, None] % CHUNK)).astype(np.float32) \
        - (same & (col <= CHUNK // 2 - 1)).astype(np.float32)
    vec_rows = [p['w0'], p['a0'], p['k_k'], p['k_a']] + ([p['v0']] if has_vres else [])
    vec = jnp.stack(vec_rows + [jnp.zeros_like(p['w0'])] * (8 - len(vec_rows)))
    dl, da, dg = p['w1'].shape[1], p['a1'].shape[1], p['g1'].shape[1]
    ins = [x, x, mod, p['mu'], p['wr'], p['wk'], p['wv'], p['w1'], p['w2'], p['a1'], p['a2'],
           p['g1'], p['g2'], vec, jnp.asarray(seg, BF16), jnp.asarray(seg.T, BF16),
           jnp.asarray(tri, BF16)]
    specs = [row,
             pl.BlockSpec((1, 8, d), lambda b, i: (b, jnp.maximum(i * (tm // 8) - 1, 0), 0)),
             pl.BlockSpec((1, 6, d), lambda b, i: (b, 0, 0)),
             full((6, d)), full((d, d)), full((d, d)), full((d, d)),
             full((d, dl)), full((dl, d)), full((d, da)), full((da, d)),
             full((d, dg)), full((dg, d)), full((8, d)),
             full((d, LANES)), full((LANES, d)), full((tm, tm))]
    if has_vres:
        dv = p['v1'].shape[1]
        ins += [v_first, p['v1'], p['v2']]
        specs += [row, full((d, dv)), full((dv, d))]
    act = jax.ShapeDtypeStruct((bsz, t, d), F32)
    nch = t // CHUNK
    outs = pl.pallas_call(
        functools.partial(_rwkv_pre_body, has_vres),
        grid=(bsz, t // tm),
        in_specs=specs,
        out_specs=[row] * 6 + [pl.BlockSpec((1, tm // CHUNK, 2, d), lambda b, i: (b, i, 0, 0))],
        out_shape=[act] * 6 + [jax.ShapeDtypeStruct((bsz, nch, 2, d), F32)],
        compiler_params=_cparams(("parallel", "parallel")),
        name="rwkv_pre",
    )(*ins)
    return outs


_BNN = (((2,), (1,)), ((0,), (0,)))
_BNT = (((2,), (2,)), ((0,), (0,)))


def _rwkv_scan_body(r_ref, k_ref, a_ref, b_ref, v_ref, gl_ref, vec_ref, o_ref, s_ref):
    c = pl.program_id(1)

    @pl.when(c == 0)
    def _():
        s_ref[...] = jnp.zeros_like(s_ref)

    L = CHUNK
    NP = RWKV_HEADS // 2
    shp = (NP, L, LANES)
    plane = lax.broadcasted_iota(jnp.int32, (1, L, 2 * L), 2)
    ph0 = plane < L
    tt = lax.broadcasted_iota(jnp.int32, (1, L, 2 * L), 1)
    ss = plane & (L - 1)
    strict = ss < tt
    incl = ss <= tt
    eye = jnp.where(ss == tt, 1.0, 0.0)
    base_bits = 4

    def same_block(bits):
        return (ss >> bits) == (tt >> bits)

    lvl0 = strict & same_block(base_bits)
    merges = [strict & same_block(bits + 1) & ((ss >> bits) != (tt >> bits))
              for bits in range(base_bits, (L - 1).bit_length())]
    dh0 = lax.broadcasted_iota(jnp.int32, (1, L, LANES), 2) < RWKV_HEAD
    ri = lax.broadcasted_iota(jnp.int32, (1, LANES, LANES), 1)
    ci = lax.broadcasted_iota(jnp.int32, (1, LANES, LANES), 2)
    blockdiag = (ri >> 6) == (ci >> 6)
    seg_mean = jnp.where(blockdiag[0], 1.0, 0.0).astype(BF16)

    def grp(ref):
        return jnp.stack([ref[0, :, hp * LANES:(hp + 1) * LANES] for hp in range(NP)])

    def bd(x):
        return jnp.concatenate([jnp.where(dh0, x, 0.0), jnp.where(dh0, 0.0, x)], axis=1)

    def bdp(y):
        return jnp.concatenate([jnp.where(ph0, y, 0.0), jnp.where(ph0, 0.0, y)], axis=1)

    def pmm(xp, y):
        return _dot1(xp, bd(y), _BNN)

    def ppm(xp, yp):
        return _dot1(xp, bdp(yp), _BNN)

    R, K, A, Bv, V = grp(r_ref), grp(k_ref), grp(a_ref), grp(b_ref), grp(v_ref)
    gvec = lambda n: jnp.stack([gl_ref[0, 0, n:n + 1, hp * LANES:(hp + 1) * LANES] for hp in range(NP)])
    S = s_ref[...] * gvec(0)

    AR = jnp.concatenate([A, R], axis=1)
    BK = jnp.concatenate([bd(Bv), bd(K)], axis=1)
    G = _dot1(AR, BK, _BNT)
    A_ab = jnp.where(strict, G[:, 0:L, 0:2 * L], 0.0)
    A_ak = jnp.where(strict, G[:, 0:L, 2 * L:4 * L], 0.0)
    A_rb = jnp.where(incl, G[:, L:2 * L, 0:2 * L], 0.0)
    A_rk = jnp.where(incl, G[:, L:2 * L, 2 * L:4 * L], 0.0)

    a0 = jnp.where(lvl0, A_ab, 0.0)
    Tm = eye + a0
    P = a0
    for _ in range(base_bits - 1):
        P = ppm(P, P)
        Tm = Tm + ppm(Tm, P)
    for lvl in merges:
        Tm = Tm + ppm(ppm(Tm, jnp.where(lvl, A_ab, 0.0)), Tm)

    PQ = _dot1(AR, S, _BNT)
    W = PQ[:, 0:L] + pmm(A_ak, V)
    U = pmm(Tm, W)
    Y = PQ[:, L:2 * L] + _dot1(jnp.concatenate([A_rb, A_rk], axis=2),
                               jnp.concatenate([bd(U), bd(V)], axis=1), _BNN)
    UV = jnp.concatenate([U, V], axis=1)
    UVt = jnp.stack([UV[hp].T for hp in range(NP)])
    BK2 = jnp.concatenate([Bv, K], axis=1)
    upd = _dot1(UVt, BK2, _BNN)
    s_ref[...] = (S + jnp.where(blockdiag, upd, 0.0)) * gvec(1)

    vrow = lambda n: jnp.stack([vec_ref[n:n + 1, hp * LANES:(hp + 1) * LANES] for hp in range(NP)])
    inv_n = 1.0 / RWKV_HEAD
    flat = lambda x: x.reshape(NP * L, LANES)
    mean = _dot_hilo_rhs(flat(Y), seg_mean).reshape(shp) * inv_n
    yc = Y - mean
    var = _dot_hilo_rhs(flat(yc * yc), seg_mean).reshape(shp) * inv_n
    yn = yc * lax.rsqrt(var + GN_EPS) * vrow(1) + vrow(2)
    bonus = _dot_hilo_rhs(flat(R * K * vrow(0)), seg_mean).reshape(shp)
    out = yn + bonus * V
    for hp in range(NP):
        o_ref[0, :, hp * LANES:(hp + 1) * LANES] = out[hp]


def _rwkv_scan(r, k, a, b, v, gl, r_k, lnx_g, lnx_b):
    bsz, t, d = r.shape
    row = pl.BlockSpec((1, CHUNK, d), lambda bb, c: (bb, c, 0))
    vec = jnp.stack([r_k.reshape(d), lnx_g, lnx_b] + [jnp.zeros((d,), F32)] * 5)
    return pl.pallas_call(
        _rwkv_scan_body,
        grid=(bsz, t // CHUNK),
        in_specs=[row] * 5 + [
            pl.BlockSpec((1, 1, 2, d), lambda bb, c: (bb, c, 0, 0)),
            pl.BlockSpec((8, d), lambda bb, c: (0, 0)),
        ],
        out_specs=row,
        out_shape=jax.ShapeDtypeStruct((bsz, t, d), F32),
        scratch_shapes=[pltpu.VMEM((RWKV_HEADS // 2, LANES, LANES), F32)],
        compiler_params=_cparams(("parallel", "arbitrary")),
        name="rwkv_scan",
    )(r, k, a, b, v, gl, vec)


IDX_COLS = 768


def _dsa_proj_body(x_ref, mod_ref, wq_ref, wc_ref, wi_ref, kvn_ref,
                   q_o, ckv_o, qi_o, ki_o, wi_o):
    hin = (x_ref[0] * (1.0 + mod_ref[0, 1:2, :]) + mod_ref[0, 0:1, :]).astype(BF16)
    q_o[0] = _mm(hin, wq_ref[...]).astype(BF16)
    ckv = _mm(hin, wc_ref[...])
    ms = jnp.mean(ckv * ckv, axis=-1, keepdims=True)
    ckv_o[0] = (ckv * lax.rsqrt(ms + 1e-6) * kvn_ref[...]).astype(BF16)
    idx = _mm(hin, wi_ref[...])
    nq = IDX_HEADS * IDX_DIM
    qi_o[0] = idx[:, 0:nq].astype(BF16)
    ki_o[0] = idx[:, nq:nq + LANES].astype(BF16)
    wi_o[0] = idx[:, nq + LANES:nq + 2 * LANES] * (IDX_HEADS ** -0.5 * IDX_DIM ** -0.5)


def _dsa_proj(x, mod, w_in, kv_norm):
    bsz, t, d = x.shape
    c1 = ATT_HEADS * ATT_HEAD_DIM
    c2 = c1 + KV_LATENT
    c3 = c2 + IDX_HEADS * IDX_DIM
    c4 = c3 + IDX_DIM
    wq = w_in[:, :c1].astype(BF16)
    wc = w_in[:, c1:c2].astype(BF16)
    pad = IDX_COLS - (c3 - c2) - 2 * IDX_DIM - IDX_HEADS
    widx = jnp.concatenate([w_in[:, c2:c3], w_in[:, c3:c4], w_in[:, c3:c4], w_in[:, c4:],
                            jnp.zeros((d, pad), F32)], axis=1).astype(BF16)
    tm = PROJ_TM

    def full(shape):
        return pl.BlockSpec(shape, lambda b, i: (0,) * len(shape))

    def row(n):
        return pl.BlockSpec((1, tm, n), lambda b, i: (b, i, 0))

    def act(n, dtype):
        return jax.ShapeDtypeStruct((bsz, t, n), dtype)

    nq = IDX_HEADS * IDX_DIM
    return pl.pallas_call(
        _dsa_proj_body,
        grid=(bsz, t // tm),
        in_specs=[row(d), pl.BlockSpec((1, 6, d), lambda b, i: (b, 0, 0)),
                  full((d, c1)), full((d, KV_LATENT)), full((d, IDX_COLS)),
                  full((1, KV_LATENT))],
        out_specs=[row(c1), row(KV_LATENT), row(nq), row(LANES), row(LANES)],
        out_shape=[act(c1, BF16), act(KV_LATENT, BF16), act(nq, BF16), act(LANES, BF16), act(LANES, F32)],
        compiler_params=_cparams(("parallel", "parallel")),
        name="dsa_proj",
    )(x, mod, wq, wc, widx, kv_norm.reshape(1, KV_LATENT))


def _dsa_index_body(k_sel, qi_ref, wi_ref, ki_ref, o_ref, key_ref):
    i = pl.program_id(1)
    tq = qi_ref.shape[1]
    t = ki_ref.shape[1]
    nchunk = t // tq
    wt = wi_ref[0].T
    lane = lax.broadcasted_iota(jnp.int32, (tq, LANES), 1)
    first = lane < IDX_DIM
    qheads = []
    for hp in range(IDX_HEADS // 2):
        qp = qi_ref[0, :, hp * LANES:(hp + 1) * LANES].astype(F32)
        qheads.append(jnp.where(first, qp, 0.0).astype(BF16))
        qheads.append(jnp.where(first, 0.0, qp).astype(BF16))
    q_all = jnp.concatenate(qheads, axis=0)
    krow = lax.broadcasted_iota(jnp.int32, (tq, tq), 0)
    qpos = i * tq + lax.broadcasted_iota(jnp.int32, (tq, tq), 1)

    def chunk_start(c):
        return pl.multiple_of(c * tq, tq)

    def score_chunk(c, carry):
        k0 = chunk_start(c)
        kk2 = ki_ref[0, pl.ds(k0, tq), :]
        score = jnp.zeros((tq, tq), F32)
        s_all = _mm(kk2, q_all, _NT)
        for h in range(IDX_HEADS):
            score = score + wt[h:h + 1, :] * jnp.maximum(s_all[:, h * tq:(h + 1) * tq], 0.0)
        score = jnp.where(score == 0.0, 0.0, score)
        bits = pltpu.bitcast(score, jnp.int32)
        skey = bits ^ ((bits >> 31) & 0x7FFFFFFF)
        key_ref[pl.ds(k0, tq), :] = jnp.where(k0 + krow <= qpos, skey, INT_MIN)
        return carry

    lax.fori_loop(0, i + 1, score_chunk, 0)

    def count(fn):
        def body(c, acc):
            k0 = chunk_start(c)
            ind = fn(key_ref[pl.ds(k0, tq), :], k0 + krow)
            part = jnp.sum(ind.reshape(tq // 32, 4, 8, tq), axis=0)
            return acc + jnp.sum(part, axis=0)
        acc = lax.fori_loop(0, i + 1, body, jnp.zeros((8, tq), F32))
        return jnp.sum(acc, axis=0, keepdims=True)

    kf = float(k_sel)

    def search(nc):
        def count_ge(cand):
            acc = jnp.zeros((8, tq), F32)
            for c in range(nc):
                ind = jnp.where(key_ref[c * tq:(c + 1) * tq, :] >= cand, 1.0, 0.0)
                acc = acc + jnp.sum(jnp.sum(ind.reshape(tq // 32, 4, 8, tq), axis=0), axis=0)
            return jnp.sum(acc, axis=0, keepdims=True)

        thr0 = jnp.where(count_ge(jnp.zeros((1, tq), jnp.int32)) >= kf, 0, INT_MIN).astype(jnp.int32)

        def thr_step(n, thr):
            cand = thr | jnp.left_shift(jnp.int32(1), 30 - n)
            return jnp.where(count_ge(cand) >= kf, cand, thr)

        return lax.fori_loop(0, 31, thr_step, thr0)

    thr = lax.switch(i, [functools.partial(search, nc + 1) for nc in range(nchunk)])
    n_gt = count(lambda keys, kpos: jnp.where(keys > thr, 1.0, 0.0))
    n_eq = count(lambda keys, kpos: jnp.where(keys == thr, 1.0, 0.0))
    need = kf - n_gt
    nbits = int(t - 1).bit_length()

    def tie_cut():
        def cut_step(n, cut):
            cand = cut | jnp.left_shift(jnp.int32(1), nbits - 1 - n)
            cnt = count(lambda keys, kpos: jnp.where(keys == thr, jnp.where(kpos < cand, 1.0, 0.0), 0.0))
            return jnp.where(cnt < need, cand, cut)
        return lax.fori_loop(0, nbits, cut_step, jnp.zeros((1, tq), jnp.int32))

    cut = lax.cond(jnp.max(n_eq - need) > 0.0, tie_cut, lambda: jnp.full((1, tq), t, jnp.int32))

    def write_chunk(c, carry):
        k0 = chunk_start(c)
        keys = key_ref[pl.ds(k0, tq), :]
        kpos = k0 + krow
        tie = jnp.where(keys == thr, jnp.where(kpos <= cut, 0.0, MASK_NEG), MASK_NEG)
        bias = jnp.where(keys > thr, 0.0, tie)
        o_ref[0, pl.ds(k0, tq), :] = jnp.where(kpos <= qpos, bias, MASK_NEG).astype(BF16)
        return carry

    lax.fori_loop(0, i + 1, write_chunk, 0)

    def masked_chunk(c, carry):
        o_ref[0, pl.ds(chunk_start(c), tq), :] = jnp.full((tq, tq), MASK_NEG, BF16)
        return carry

    lax.fori_loop(i + 1, nchunk, masked_chunk, 0)


def _dsa_index(qi, ki, wi, k_sel):
    bsz, t, nq = qi.shape
    tq = ATT_T
    return pl.pallas_call(
        functools.partial(_dsa_index_body, k_sel),
        grid=(bsz, t // tq),
        in_specs=[pl.BlockSpec((1, tq, nq), lambda b, i: (b, i, 0)),
                  pl.BlockSpec((1, tq, LANES), lambda b, i: (b, i, 0)),
                  pl.BlockSpec((1, t, LANES), lambda b, i: (b, 0, 0))],
        out_specs=pl.BlockSpec((1, t, tq), lambda b, i: (b, 0, i)),
        out_shape=jax.ShapeDtypeStruct((bsz, t, t), BF16),
        scratch_shapes=[pltpu.VMEM((t, tq), jnp.int32)],
        compiler_params=_cparams(("parallel", "parallel")),
        name="dsa_index",
    )(qi, wi, ki)


def _t5_bucket_np(n):
    n = np.maximum(n, 0)
    max_exact = REL_BUCKETS // 2
    nf = np.maximum(n, 1).astype(np.float32)
    large = max_exact + (np.log(nf / np.float32(max_exact)) / np.float32(math.log(REL_MAX_DIST / max_exact))
                         * np.float32(REL_BUCKETS - max_exact)).astype(np.int32)
    large = np.minimum(large, REL_BUCKETS - 1)
    return np.where(n < max_exact, n, large).astype(np.int32)


def _band_body(bkt_ref, rb_ref, o_ref):
    h = pl.program_id(1)
    bkt = bkt_ref[0]
    acc = jnp.zeros(bkt.shape, F32)
    for b in range(REL_BUCKETS):
        acc = jnp.where(bkt == b, rb_ref[b, h], acc)
    o_ref[0] = acc * LOG2E


def _band_bias(rel_bias):
    tt = ATT_T
    kc = np.arange(tt)[:, None]
    qr = np.arange(tt)[None, :]
    planes = [_t5_bucket_np(d * tt + qr - kc) for d in range(3)]
    assert (planes[2] == REL_BUCKETS - 1).all() and tt + 1 >= 113
    bkt = jnp.asarray(np.stack(planes))
    return pl.pallas_call(
        _band_body,
        grid=(3, ATT_HEADS),
        in_specs=[pl.BlockSpec((1, tt, tt), lambda d, h: (d, 0, 0)),
                  pl.BlockSpec(memory_space=pltpu.SMEM)],
        out_specs=pl.BlockSpec((1, tt, tt), lambda d, h: (d, 0, h)),
        out_shape=jax.ShapeDtypeStruct((3, tt, ATT_HEADS * tt), F32),
        compiler_params=_cparams(("parallel", "parallel")),
        name="band_bias",
    )(bkt, rel_bias)


LOG2E = 1.4426950408889634
ACC_ROWS = KV_LATENT + 16


def _dsa_attn_body(qt_ref, kt_ref, q_ref, ckv_ref, mask_ref, wuk_ref, wuv_ref, near_ref, far_ref,
                   x_ref, mod_ref, wout_ref, lng_ref, lnb_ref, o_ref,
                   ql_ref, m_ref, sm_ref, ot_ref, *acc_refs):
    i = qt_ref[pl.program_id(1)]
    kp = kt_ref[pl.program_id(1)]
    gap = i - 2 * kp
    nh = ATT_HEADS
    tq = q_ref.shape[1]
    tk = tq
    qscale = ATT_HEAD_DIM ** -0.5 * LOG2E

    @pl.when(kp == 0)
    def _():
        for hp in range(nh // 2):
            qp = q_ref[0, :, hp * LANES:(hp + 1) * LANES].astype(BF16)
            qlat = _mm(qp, wuk_ref[hp]) * qscale
            ql_ref[2 * hp * tq:(2 * hp + 1) * tq, :] = qlat[:, 0:KV_LATENT].astype(BF16)
            ql_ref[(2 * hp + 1) * tq:(2 * hp + 2) * tq, :] = qlat[:, KV_LATENT:2 * KV_LATENT].astype(BF16)
        m_ref[...] = jnp.full(m_ref.shape, MASK_NEG, F32)
        for acc_ref in acc_refs:
            acc_ref[...] = jnp.zeros_like(acc_ref)

    def step(*kinds):
        nk = len(kinds) * tk
        ckv = ckv_ref[0, 0:nk, :]
        ckv_aug = jnp.concatenate([ckv.astype(F32).T, jnp.ones((ACC_ROWS - KV_LATENT, nk), F32)],
                                  axis=0).astype(BF16)
        maskb = mask_ref[0, 0:nk, :].astype(F32)
        m_prev = m_ref[...]
        m_news = []
        for h in range(nh):
            hs = slice(h * tq, (h + 1) * tq)
            s = _mm(ckv, ql_ref[hs, :], _NT)
            m_new = m_prev[:, hs]
            for n, plane in enumerate(kinds):
                rows = slice(n * tk, (n + 1) * tk)
                sm = s[rows] + maskb[rows]
                if plane is None:
                    m_new = jnp.maximum(m_new, jnp.max(sm, axis=0, keepdims=True) + far_ref[:, hs])
                else:
                    sm = sm + near_ref[plane, :, hs]
                    m_new = jnp.maximum(m_new, jnp.max(sm, axis=0, keepdims=True))
                sm_ref[rows, hs] = sm
            m_news.append(m_new)
        for h in range(nh):
            hs = slice(h * tq, (h + 1) * tq)
            m_new = m_news[h]
            p = jnp.concatenate(
                [jnp.exp2(sm_ref[n * tk:(n + 1) * tk, hs]
                          - (m_new - far_ref[:, hs] if plane is None else m_new)).astype(BF16)
                 for n, plane in enumerate(kinds)], axis=0)
            alpha = jnp.exp2(m_prev[:, hs] - m_new)
            acc_refs[h][...] = alpha * acc_refs[h][...] + _mm(ckv_aug, p)
        m_ref[...] = jnp.concatenate(m_news, axis=1)

    @pl.when(gap >= 3)
    def _():
        step(None, None)

    @pl.when(gap == 2)
    def _():
        step(None, 1)

    @pl.when(gap == 1)
    def _():
        step(1, 0)

    @pl.when(gap == 0)
    def _():
        step(0)

    @pl.when(gap <= 1)
    def _():
        def norm(h):
            a = acc_refs[h][...]
            return a[0:KV_LATENT] * (1.0 / a[KV_LATENT:KV_LATENT + 1])
        for hp in range(nh // 2):
            olat = jnp.concatenate([norm(2 * hp), norm(2 * hp + 1)], axis=0).astype(BF16)
            ot_ref[hp * LANES:(hp + 1) * LANES, :] = _mm(wuv_ref[hp], olat)
        y = _mm(ot_ref[...].T.astype(BF16), wout_ref[...])
        res = DEEPNORM_ALPHA * x_ref[0] + (1.0 + mod_ref[0, 2:3, :]) * y
        o_ref[0] = _layernorm(res, lng_ref[...], lnb_ref[...])


def _dsa_attn(q, ckv, maskt, w_uk, w_uv, band, x, mod, w_out, ln_g, ln_b):
    bsz, t, d = q.shape
    tt = ATT_T
    nt = t // tt
    nh = ATT_HEADS
    zk = jnp.zeros((nh // 2, ATT_HEAD_DIM, KV_LATENT), F32)
    wuk2 = jnp.concatenate([jnp.concatenate([w_uk[0::2], zk], axis=2),
                            jnp.concatenate([zk, w_uk[1::2]], axis=2)], axis=1).astype(BF16)
    wuv_t = jnp.swapaxes(w_uv, 1, 2)
    zv = jnp.zeros((nh // 2, ATT_HEAD_DIM, KV_LATENT), F32)
    wuv2 = jnp.concatenate([jnp.concatenate([wuv_t[0::2], zv], axis=2),
                            jnp.concatenate([zv, wuv_t[1::2]], axis=2)], axis=1).astype(BF16)
    assert nt % 2 == 0
    steps = [(i, kp) for i in range(nt) for kp in range(i // 2 + 1)]
    q_tab = jnp.asarray([p[0] for p in steps], jnp.int32)
    k_tab = jnp.asarray([p[1] for p in steps], jnp.int32)
    grid_spec = pltpu.PrefetchScalarGridSpec(
        num_scalar_prefetch=2,
        grid=(bsz, len(steps)),
        in_specs=[pl.BlockSpec((1, tt, d), lambda b, s, qt, kt: (b, qt[s], 0)),
                  pl.BlockSpec((1, 2 * tt, KV_LATENT), lambda b, s, qt, kt: (b, kt[s], 0)),
                  pl.BlockSpec((1, 2 * tt, tt), lambda b, s, qt, kt: (b, kt[s], qt[s])),
                  pl.BlockSpec((nh // 2, LANES, 2 * KV_LATENT), lambda b, s, qt, kt: (0, 0, 0)),
                  pl.BlockSpec((nh // 2, LANES, 2 * KV_LATENT), lambda b, s, qt, kt: (0, 0, 0)),
                  pl.BlockSpec((2, tt, nh * tt), lambda b, s, qt, kt: (0, 0, 0)),
                  pl.BlockSpec((1, nh * tt), lambda b, s, qt, kt: (0, 0)),
                  pl.BlockSpec((1, tt, d), lambda b, s, qt, kt: (b, qt[s], 0)),
                  pl.BlockSpec((1, 6, d), lambda b, s, qt, kt: (b, 0, 0)),
                  pl.BlockSpec((d, d), lambda b, s, qt, kt: (0, 0)),
                  pl.BlockSpec((1, d), lambda b, s, qt, kt: (0, 0)),
                  pl.BlockSpec((1, d), lambda b, s, qt, kt: (0, 0))],
        out_specs=pl.BlockSpec((1, tt, d), lambda b, s, qt, kt: (b, qt[s], 0)),
        scratch_shapes=[pltpu.VMEM((nh * tt, KV_LATENT), BF16),
                        pltpu.VMEM((1, nh * tt), F32),
                        pltpu.VMEM((2 * tt, nh * tt), F32),
                        pltpu.VMEM((d, tt), F32)]
        + [pltpu.VMEM((ACC_ROWS, tt), F32)] * nh)
    return pl.pallas_call(
        _dsa_attn_body,
        grid_spec=grid_spec,
        out_shape=jax.ShapeDtypeStruct((bsz, t, d), F32),
        compiler_params=_cparams(("parallel", "arbitrary")),
        name="dsa_attn",
    )(q_tab, k_tab, q, ckv, maskt, wuk2, wuv2, band[:2], band[2, 0:1, :],
      x, mod, w_out, ln_g.reshape(1, d), ln_b.reshape(1, d))


def kernel(x, c, ada_w, ada_b, ln_g, ln_b, ffn_w_in, ffn_w_out, rwkv_mu, rwkv_w_rkv, rwkv_w0, rwkv_w1, rwkv_w2, rwkv_a0, rwkv_a1, rwkv_a2, rwkv_v0, rwkv_v1, rwkv_v2, rwkv_g1, rwkv_g2, rwkv_k_k, rwkv_k_a, rwkv_r_k, rwkv_lnx_g, rwkv_lnx_b, rwkv_w_out, dsa_w_in, dsa_kv_norm, dsa_w_uk, dsa_w_uv, dsa_w_out, rel_bias):
    bsz, t, d = x.shape
    assert d == D_MODEL and t % ATT_T == 0 and t % FFN_TM == 0 and t % PRE_TM == 0
    mod_all = _adaln(c, ada_w, ada_b).reshape(DEPTH, bsz, 6, d)
    band = _band_bias(rel_bias)
    k_sel = min(TOPK_MAX, t // TOPK_DIV)
    bf = lambda w: w.astype(BF16)
    v_first = None
    for i in range(DEPTH):
        mod = mod_all[i]
        j = i // 2
        if i % 2 == 0:
            p = dict(mu=rwkv_mu[j], wr=bf(rwkv_w_rkv[j, 0]), wk=bf(rwkv_w_rkv[j, 1]), wv=bf(rwkv_w_rkv[j, 2]),
                     w0=rwkv_w0[j], w1=bf(rwkv_w1[j]), w2=bf(rwkv_w2[j]),
                     a0=rwkv_a0[j], a1=bf(rwkv_a1[j]), a2=bf(rwkv_a2[j]),
                     g1=bf(rwkv_g1[j]), g2=bf(rwkv_g2[j]), k_k=rwkv_k_k[j], k_a=rwkv_k_a[j])
            if j > 0:
                p.update(v0=rwkv_v0[j - 1], v1=bf(rwkv_v1[j - 1]), v2=bf(rwkv_v2[j - 1]))
            r_s, k_s, a_s, b_s, v, g, gl = _rwkv_pre(x, mod, p, v_first if j > 0 else None)
            if j == 0:
                v_first = v
            z = _rwkv_scan(r_s, k_s, a_s, b_s, v, gl, rwkv_r_k[j], rwkv_lnx_g[j], rwkv_lnx_b[j])
            tail = (z, g, bf(rwkv_w_out[j]), ln_g[i, 0], ln_b[i, 0])
        else:
            tail = None
            q, ckv, qi, ki, wi = _dsa_proj(x, mod, dsa_w_in[j], dsa_kv_norm[j])
            maskt = _dsa_index(qi, ki, wi, k_sel)
            x = _dsa_attn(q, ckv, maskt, dsa_w_uk[j], dsa_w_uv[j], band,
                          x, mod, bf(dsa_w_out[j]), ln_g[i, 0], ln_b[i, 0])
        x = _ffn(x, mod, bf(ffn_w_in[i]), bf(ffn_w_out[i]), ln_g[i, 1], ln_b[i, 1], tail)
    return x
```

```python
import functools
import math

import numpy as np
import jax
import jax.numpy as jnp
from jax import lax
from jax.experimental import pallas as pl
from jax.experimental.pallas import tpu as pltpu

F32 = jnp.float32
BF16 = jnp.bfloat16

D_MODEL = 1024
DEPTH = 4
RWKV_HEAD = 64
RWKV_HEADS = D_MODEL // RWKV_HEAD
GN_EPS = RWKV_HEAD * 1e-5
ATT_HEADS = 16
ATT_HEAD_DIM = 64
KV_LATENT = 128
IDX_HEADS = 8
IDX_DIM = 64
TOPK_MAX = 256
TOPK_DIV = 4
REL_BUCKETS = 32
REL_MAX_DIST = 128
FFN_HIDDEN = 2816
DEEPNORM_ALPHA = (2 * DEPTH) ** 0.25
LN_EPS = 1e-5

LANES = 128
CHUNK = 128
PRE_TM = 256
PROJ_TM = 512
FFN_TM = 512
FFN_TF = 1408
ATT_T = 256
MASK_NEG = -1e30
VMEM_LIMIT = 56 * 1024 * 1024
INT_MIN = -2 ** 31


def _cparams(sem):
    return pltpu.CompilerParams(dimension_semantics=sem, vmem_limit_bytes=VMEM_LIMIT)


def _split2(x):
    hi = x.astype(BF16)
    lo = (x - hi.astype(F32)).astype(BF16)
    return hi, lo


def _split3(x):
    hi = x.astype(BF16)
    r1 = x - hi.astype(F32)
    mid = r1.astype(BF16)
    lo = (r1 - mid.astype(F32)).astype(BF16)
    return hi, mid, lo


_NN = (((1,), (0,)), ((), ()))
_NT = (((1,), (1,)), ((), ()))


def _mm(a, b, dims=_NN):
    return lax.dot_general(a, b, dims, preferred_element_type=F32)


def _dot3(a, b, dims=_NN):
    ah, al = _split2(a)
    bh, bl = _split2(b)
    return _mm(ah, bh, dims) + (_mm(ah, bl, dims) + _mm(al, bh, dims))


def _dot1(a, b, dims=_NN):
    return _mm(a.astype(BF16), b.astype(BF16), dims)


def _dot_hilo_rhs(a, b_bf16):
    h, l = _split2(a)
    return _mm(h, b_bf16) + _mm(l, b_bf16)


def _sigmoid(x):
    return 1.0 / (1.0 + jnp.exp(-x))


def _layernorm(xr, g, b):
    mu = jnp.mean(xr, axis=-1, keepdims=True)
    xc = xr - mu
    var = jnp.mean(xc * xc, axis=-1, keepdims=True)
    return xc * lax.rsqrt(var + LN_EPS) * g + b


def _adaln_body(c_ref, w_ref, b_ref, o_ref):
    c = c_ref[...]
    cond = c * _sigmoid(c)
    o_ref[0] = _dot3(cond, w_ref[0]) + b_ref[0]


def _adaln(c, ada_w, ada_b):
    depth, d, n = ada_w.shape
    bsz = c.shape[0]
    tn = n // 4
    return pl.pallas_call(
        _adaln_body,
        grid=(depth, n // tn),
        in_specs=[
            pl.BlockSpec((bsz, d), lambda i, j: (0, 0)),
            pl.BlockSpec((1, d, tn), lambda i, j: (i, 0, j)),
            pl.BlockSpec((1, 1, tn), lambda i, j: (i, 0, j)),
        ],
        out_specs=pl.BlockSpec((1, bsz, tn), lambda i, j: (i, 0, j)),
        out_shape=jax.ShapeDtypeStruct((depth, bsz, n), F32),
        compiler_params=_cparams(("parallel", "parallel")),
        name="adaln",
    )(c, ada_w, ada_b.reshape(depth, 1, n))


def _ffn_body(mixer_tail, *refs):
    if mixer_tail:
        (x_ref, mod_ref, wg_ref, wu_ref, wo_ref, lng_ref, lnb_ref,
         z_ref, g_ref, wm_ref, lng0_ref, lnb0_ref, o_ref, hin_ref, acc_ref, xm_ref) = refs
    else:
        x_ref, mod_ref, wg_ref, wu_ref, wo_ref, lng_ref, lnb_ref, o_ref, hin_ref, acc_ref = refs
    j = pl.program_id(2)

    @pl.when(j == 0)
    def _():
        x = x_ref[0]
        if mixer_tail:
            y = _mm((z_ref[0] * g_ref[0]).astype(BF16), wm_ref[...])
            x = _layernorm(DEEPNORM_ALPHA * x + (1.0 + mod_ref[0, 2:3, :]) * y, lng0_ref[...], lnb0_ref[...])
            xm_ref[...] = x
        hin_ref[...] = (x * (1.0 + mod_ref[0, 4:5, :]) + mod_ref[0, 3:4, :]).astype(BF16)
        acc_ref[...] = jnp.zeros_like(acc_ref)

    hin = hin_ref[...]
    gate = _mm(hin, wg_ref[...])
    up = _mm(hin, wu_ref[...])
    hid = (gate * _sigmoid(gate) * up).astype(BF16)
    acc_ref[...] += _mm(hid, wo_ref[...])

    @pl.when(j == pl.num_programs(2) - 1)
    def _():
        x = xm_ref[...] if mixer_tail else x_ref[0]
        res = DEEPNORM_ALPHA * x + (1.0 + mod_ref[0, 5:6, :]) * acc_ref[...]
        o_ref[0] = _layernorm(res, lng_ref[...], lnb_ref[...])


def _ffn(x, mod, w_in, w_out, ln_g, ln_b, mixer_tail=None):
    bsz, t, d = x.shape
    f = w_out.shape[0]
    nf = f // FFN_TF
    row = pl.BlockSpec((1, FFN_TM, d), lambda b, i, j: (b, i, 0))
    vec = pl.BlockSpec((1, d), lambda b, i, j: (0, 0))
    ins = [x, mod, w_in, w_in, w_out, ln_g.reshape(1, d), ln_b.reshape(1, d)]
    specs = [row,
             pl.BlockSpec((1, 6, d), lambda b, i, j: (b, 0, 0)),
             pl.BlockSpec((d, FFN_TF), lambda b, i, j: (0, j)),
             pl.BlockSpec((d, FFN_TF), lambda b, i, j: (0, nf + j)),
             pl.BlockSpec((FFN_TF, d), lambda b, i, j: (j, 0)),
             vec, vec]
    scratch = [pltpu.VMEM((FFN_TM, d), BF16), pltpu.VMEM((FFN_TM, d), F32)]
    if mixer_tail is not None:
        z, g, w_mix, ln_g0, ln_b0 = mixer_tail
        ins += [z, g, w_mix, ln_g0.reshape(1, d), ln_b0.reshape(1, d)]
        specs += [row, row, pl.BlockSpec((d, d), lambda b, i, j: (0, 0)), vec, vec]
        scratch += [pltpu.VMEM((FFN_TM, d), F32)]
    return pl.pallas_call(
        functools.partial(_ffn_body, mixer_tail is not None),
        grid=(bsz, t // FFN_TM, nf),
        in_specs=specs,
        out_specs=row,
        out_shape=jax.ShapeDtypeStruct((bsz, t, d), F32),
        scratch_shapes=scratch,
        compiler_params=_cparams(("parallel", "parallel", "arbitrary")),
        name="ffn",
    )(*ins)


def _rwkv_pre_body(has_vres, *refs):
    if has_vres:
        (x_ref, xp_ref, mod_ref, mu_ref, wr_ref, wk_ref, wv_ref, w1_ref, w2_ref, a1_ref, a2_ref,
         g1_ref, g2_ref, vec_ref, seg_ref, segt_ref, tri_ref, vf_ref, v1_ref, v2_ref,
         r_o, k_o, a_o, b_o, v_o, g_o, gl_o) = refs
    else:
        (x_ref, xp_ref, mod_ref, mu_ref, wr_ref, wk_ref, wv_ref, w1_ref, w2_ref, a1_ref, a2_ref,
         g1_ref, g2_ref, vec_ref, seg_ref, segt_ref, tri_ref,
         r_o, k_o, a_o, b_o, v_o, g_o, gl_o) = refs
    i = pl.program_id(1)
    sc = 1.0 + mod_ref[0, 1:2, :]
    sh = mod_ref[0, 0:1, :]
    hin = x_ref[0] * sc + sh
    tm = hin.shape[0]
    prev_row = xp_ref[0, 7:8, :] * sc + sh
    prev_row = jnp.where(i == 0, 0.0, prev_row)
    rows = lax.broadcasted_iota(jnp.int32, hin.shape, 0)
    hprev = jnp.where(rows == 0, prev_row, pltpu.roll(hin, 1, 0))
    xx = hprev - hin

    def mix(p):
        return hin + xx * mu_ref[p:p + 1, :]

    xr, xk, xv = mix(0).astype(BF16), mix(1).astype(BF16), mix(2).astype(BF16)
    xw, xa, xg = mix(3).astype(BF16), mix(4).astype(BF16), mix(5).astype(BF16)
    w0, a0, kkw, kaw = vec_ref[0:1, :], vec_ref[1:2, :], vec_ref[2:3, :], vec_ref[3:4, :]

    r = _mm(xr, wr_ref[...])
    k = _mm(xk, wk_ref[...])
    v = _mm(xv, wv_ref[...])

    wl = w0 + _mm(jnp.tanh(_mm(xw, w1_ref[...])).astype(BF16), w2_ref[...])
    nz = -wl
    softplus = jnp.maximum(nz, 0.0) + jnp.log(1.0 + jnp.exp(-jnp.abs(nz)))
    logdec = -jnp.exp(-softplus - 0.5)

    a = _sigmoid(a0 + _mm(_mm(xa, a1_ref[...]).astype(BF16), a2_ref[...]))
    if has_vres:
        v0 = vec_ref[4:5, :]
        vmix = _sigmoid(v0 + _mm(_mm(xv, v1_ref[...]).astype(BF16), v2_ref[...]))
        v = v + (vf_ref[0] - v) * vmix
    g = _mm(_sigmoid(_mm(xg, g1_ref[...])).astype(BF16), g2_ref[...])

    kk = k * kkw
    ss = _dot_hilo_rhs(kk * kk, seg_ref[...])
    nrm = jnp.maximum(jnp.sqrt(ss), 1e-12)
    inv = _dot_hilo_rhs(1.0 / nrm, segt_ref[...])
    kk = kk * inv
    k = k * (1.0 + (a - 1.0) * kaw)

    cum = _dot_exact_rhs_lhs(tri_ref[...], logdec)
    ginc = jnp.exp(cum)
    ginv = jnp.exp(-cum)
    gprev = jnp.exp(cum - logdec)

    r_o[0] = r * ginc
    k_o[0] = k * ginv
    a_o[0] = -kk * gprev
    b_o[0] = kk * a * ginv
    v_o[0] = v
    g_o[0] = g
    for cc in range(tm // CHUNK):
        first, last = cc * CHUNK, cc * CHUNK + CHUNK - 1
        gl_o[0, cc, 0:1] = jnp.exp(logdec[first:first + 1, :] - cum[first:first + 1, :])
        gl_o[0, cc, 1:2] = ginc[last:last + 1, :]


def _dot_exact_rhs_lhs(m_bf16, x):
    h, mid, l = _split3(x)
    return _mm(m_bf16, h) + (_mm(m_bf16, mid) + _mm(m_bf16, l))


def _rwkv_pre(x, mod, p, v_first):
    bsz, t, d = x.shape
    tm = PRE_TM
    has_vres = v_first is not None
    row = pl.BlockSpec((1, tm, d), lambda b, i: (b, i, 0))

    def full(shape):
        return pl.BlockSpec(shape, lambda b, i: (0,) * len(shape))

    heads = d // RWKV_HEAD
    seg = np.zeros((d, LANES), np.float32)
    seg[np.arange(d), np.arange(d) // RWKV_HEAD] = 1.0
    idx = np.arange(tm)
    same = idx[:, None] // CHUNK == idx[None, :] // CHUNK
    col = idx[None, :] % CHUNK
    tri = (same & (col <= idx[:, None] % CHUNK)).astype(np.float32) \
        - (same & (col <= CHUNK // 2 - 1)).astype(np.float32)
    vec_rows = [p['w0'], p['a0'], p['k_k'], p['k_a']] + ([p['v0']] if has_vres else [])
    vec = jnp.stack(vec_rows + [jnp.zeros_like(p['w0'])] * (8 - len(vec_rows)))
    dl, da, dg = p['w1'].shape[1], p['a1'].shape[1], p['g1'].shape[1]
    ins = [x, x, mod, p['mu'], p['wr'], p['wk'], p['wv'], p['w1'], p['w2'], p['a1'], p['a2'],
           p['g1'], p['g2'], vec, jnp.asarray(seg, BF16), jnp.asarray(seg.T, BF16),
           jnp.asarray(tri, BF16)]
    specs = [row,
             pl.BlockSpec((1, 8, d), lambda b, i: (b, jnp.maximum(i * (tm // 8) - 1, 0), 0)),
             pl.BlockSpec((1, 6, d), lambda b, i: (b, 0, 0)),
             full((6, d)), full((d, d)), full((d, d)), full((d, d)),
             full((d, dl)), full((dl, d)), full((d, da)), full((da, d)),
             full((d, dg)), full((dg, d)), full((8, d)),
             full((d, LANES)), full((LANES, d)), full((tm, tm))]
    if has_vres:
        dv = p['v1'].shape[1]
        ins += [v_first, p['v1'], p['v2']]
        specs += [row, full((d, dv)), full((dv, d))]
    act = jax.ShapeDtypeStruct((bsz, t, d), F32)
    nch = t // CHUNK
    outs = pl.pallas_call(
        functools.partial(_rwkv_pre_body, has_vres),
        grid=(bsz, t // tm),
        in_specs=specs,
        out_specs=[row] * 6 + [pl.BlockSpec((1, tm // CHUNK, 2, d), lambda b, i: (b, i, 0, 0))],
        out_shape=[act] * 6 + [jax.ShapeDtypeStruct((bsz, nch, 2, d), F32)],
        compiler_params=_cparams(("parallel", "parallel")),
        name="rwkv_pre",
    )(*ins)
    return outs


_BNN = (((2,), (1,)), ((0,), (0,)))
_BNT = (((2,), (2,)), ((0,), (0,)))


def _rwkv_scan_body(r_ref, k_ref, a_ref, b_ref, v_ref, gl_ref, vec_ref, o_ref, s_ref):
    c = pl.program_id(1)

    @pl.when(c == 0)
    def _():
        s_ref[...] = jnp.zeros_like(s_ref)

    L = CHUNK
    NP = RWKV_HEADS // 2
    shp = (NP, L, LANES)
    plane = lax.broadcasted_iota(jnp.int32, (1, L, 2 * L), 2)
    ph0 = plane < L
    tt = lax.broadcasted_iota(jnp.int32, (1, L, 2 * L), 1)
    ss = plane & (L - 1)
    strict = ss < tt
    incl = ss <= tt
    eye = jnp.where(ss == tt, 1.0, 0.0)
    base_bits = 4

    def same_block(bits):
        return (ss >> bits) == (tt >> bits)

    lvl0 = strict & same_block(base_bits)
    merges = [strict & same_block(bits + 1) & ((ss >> bits) != (tt >> bits))
              for bits in range(base_bits, (L - 1).bit_length())]
    dh0 = lax.broadcasted_iota(jnp.int32, (1, L, LANES), 2) < RWKV_HEAD
    ri = lax.broadcasted_iota(jnp.int32, (1, LANES, LANES), 1)
    ci = lax.broadcasted_iota(jnp.int32, (1, LANES, LANES), 2)
    blockdiag = (ri >> 6) == (ci >> 6)
    seg_mean = jnp.where(blockdiag[0], 1.0, 0.0).astype(BF16)

    def grp(ref):
        return jnp.stack([ref[0, :, hp * LANES:(hp + 1) * LANES] for hp in range(NP)])

    def bd(x):
        return jnp.concatenate([jnp.where(dh0, x, 0.0), jnp.where(dh0, 0.0, x)], axis=1)

    def bdp(y):
        return jnp.concatenate([jnp.where(ph0, y, 0.0), jnp.where(ph0, 0.0, y)], axis=1)

    def pmm(xp, y):
        return _dot1(xp, bd(y), _BNN)

    def ppm(xp, yp):
        return _dot1(xp, bdp(yp), _BNN)

    R, K, A, Bv, V = grp(r_ref), grp(k_ref), grp(a_ref), grp(b_ref), grp(v_ref)
    gvec = lambda n: jnp.stack([gl_ref[0, 0, n:n + 1, hp * LANES:(hp + 1) * LANES] for hp in range(NP)])
    S = s_ref[...] * gvec(0)

    AR = jnp.concatenate([A, R], axis=1)
    BK = jnp.concatenate([bd(Bv), bd(K)], axis=1)
    G = _dot1(AR, BK, _BNT)
    A_ab = jnp.where(strict, G[:, 0:L, 0:2 * L], 0.0)
    A_ak = jnp.where(strict, G[:, 0:L, 2 * L:4 * L], 0.0)
    A_rb = jnp.where(incl, G[:, L:2 * L, 0:2 * L], 0.0)
    A_rk = jnp.where(incl, G[:, L:2 * L, 2 * L:4 * L], 0.0)

    a0 = jnp.where(lvl0, A_ab, 0.0)
    Tm = eye + a0
    P = a0
    for _ in range(base_bits - 1):
        P = ppm(P, P)
        Tm = Tm + ppm(Tm, P)
    for lvl in merges:
        Tm = Tm + ppm(ppm(Tm, jnp.where(lvl, A_ab, 0.0)), Tm)

    PQ = _dot1(AR, S, _BNT)
    W = PQ[:, 0:L] + pmm(A_ak, V)
    U = pmm(Tm, W)
    Y = PQ[:, L:2 * L] + _dot1(jnp.concatenate([A_rb, A_rk], axis=2),
                               jnp.concatenate([bd(U), bd(V)], axis=1), _BNN)
    UV = jnp.concatenate([U, V], axis=1)
    UVt = jnp.stack([UV[hp].T for hp in range(NP)])
    BK2 = jnp.concatenate([Bv, K], axis=1)
    upd = _dot1(UVt, BK2, _BNN)
    s_ref[...] = (S + jnp.where(blockdiag, upd, 0.0)) * gvec(1)

    vrow = lambda n: jnp.stack([vec_ref[n:n + 1, hp * LANES:(hp + 1) * LANES] for hp in range(NP)])
    inv_n = 1.0 / RWKV_HEAD
    flat = lambda x: x.reshape(NP * L, LANES)
    mean = _dot_hilo_rhs(flat(Y), seg_mean).reshape(shp) * inv_n
    yc = Y - mean
    var = _dot_hilo_rhs(flat(yc * yc), seg_mean).reshape(shp) * inv_n
    yn = yc * lax.rsqrt(var + GN_EPS) * vrow(1) + vrow(2)
    bonus = _dot_hilo_rhs(flat(R * K * vrow(0)), seg_mean).reshape(shp)
    out = yn + bonus * V
    for hp in range(NP):
        o_ref[0, :, hp * LANES:(hp + 1) * LANES] = out[hp]


def _rwkv_scan(r, k, a, b, v, gl, r_k, lnx_g, lnx_b):
    bsz, t, d = r.shape
    row = pl.BlockSpec((1, CHUNK, d), lambda bb, c: (bb, c, 0))
    vec = jnp.stack([r_k.reshape(d), lnx_g, lnx_b] + [jnp.zeros((d,), F32)] * 5)
    return pl.pallas_call(
        _rwkv_scan_body,
        grid=(bsz, t // CHUNK),
        in_specs=[row] * 5 + [
            pl.BlockSpec((1, 1, 2, d), lambda bb, c: (bb, c, 0, 0)),
            pl.BlockSpec((8, d), lambda bb, c: (0, 0)),
        ],
        out_specs=row,
        out_shape=jax.ShapeDtypeStruct((bsz, t, d), F32),
        scratch_shapes=[pltpu.VMEM((RWKV_HEADS // 2, LANES, LANES), F32)],
        compiler_params=_cparams(("parallel", "arbitrary")),
        name="rwkv_scan",
    )(r, k, a, b, v, gl, vec)


IDX_COLS = 768


def _dsa_proj_body(x_ref, mod_ref, wq_ref, wc_ref, wi_ref, kvn_ref,
                   q_o, ckv_o, qi_o, ki_o, wi_o):
    hin = (x_ref[0] * (1.0 + mod_ref[0, 1:2, :]) + mod_ref[0, 0:1, :]).astype(BF16)
    q_o[0] = _mm(hin, wq_ref[...]).astype(BF16)
    ckv = _mm(hin, wc_ref[...])
    ms = jnp.mean(ckv * ckv, axis=-1, keepdims=True)
    ckv_o[0] = (ckv * lax.rsqrt(ms + 1e-6) * kvn_ref[...]).astype(BF16)
    idx = _mm(hin, wi_ref[...])
    nq = IDX_HEADS * IDX_DIM
    qi_o[0] = idx[:, 0:nq].astype(BF16)
    ki_o[0] = idx[:, nq:nq + LANES].astype(BF16)
    wi_o[0] = idx[:, nq + LANES:nq + 2 * LANES] * (IDX_HEADS ** -0.5 * IDX_DIM ** -0.5)


def _dsa_proj(x, mod, w_in, kv_norm):
    bsz, t, d = x.shape
    c1 = ATT_HEADS * ATT_HEAD_DIM
    c2 = c1 + KV_LATENT
    c3 = c2 + IDX_HEADS * IDX_DIM
    c4 = c3 + IDX_DIM
    wq = w_in[:, :c1].astype(BF16)
    wc = w_in[:, c1:c2].astype(BF16)
    pad = IDX_COLS - (c3 - c2) - 2 * IDX_DIM - IDX_HEADS
    widx = jnp.concatenate([w_in[:, c2:c3], w_in[:, c3:c4], w_in[:, c3:c4], w_in[:, c4:],
                            jnp.zeros((d, pad), F32)], axis=1).astype(BF16)
    tm = PROJ_TM

    def full(shape):
        return pl.BlockSpec(shape, lambda b, i: (0,) * len(shape))

    def row(n):
        return pl.BlockSpec((1, tm, n), lambda b, i: (b, i, 0))

    def act(n, dtype):
        return jax.ShapeDtypeStruct((bsz, t, n), dtype)

    nq = IDX_HEADS * IDX_DIM
    return pl.pallas_call(
        _dsa_proj_body,
        grid=(bsz, t // tm),
        in_specs=[row(d), pl.BlockSpec((1, 6, d), lambda b, i: (b, 0, 0)),
                  full((d, c1)), full((d, KV_LATENT)), full((d, IDX_COLS)),
                  full((1, KV_LATENT))],
        out_specs=[row(c1), row(KV_LATENT), row(nq), row(LANES), row(LANES)],
        out_shape=[act(c1, BF16), act(KV_LATENT, BF16), act(nq, BF16), act(LANES, BF16), act(LANES, F32)],
        compiler_params=_cparams(("parallel", "parallel")),
        name="dsa_proj",
    )(x, mod, wq, wc, widx, kv_norm.reshape(1, KV_LATENT))


def _dsa_index_body(k_sel, qi_ref, wi_ref, ki_ref, o_ref, key_ref):
    i = pl.program_id(1)
    tq = qi_ref.shape[1]
    t = ki_ref.shape[1]
    nchunk = t // tq
    wt = wi_ref[0].T
    lane = lax.broadcasted_iota(jnp.int32, (tq, LANES), 1)
    first = lane < IDX_DIM
    qheads = []
    for hp in range(IDX_HEADS // 2):
        qp = qi_ref[0, :, hp * LANES:(hp + 1) * LANES].astype(F32)
        qheads.append(jnp.where(first, qp, 0.0).astype(BF16))
        qheads.append(jnp.where(first, 0.0, qp).astype(BF16))
    q_all = jnp.concatenate(qheads, axis=0)
    krow = lax.broadcasted_iota(jnp.int32, (tq, tq), 0)
    qlane = lax.broadcasted_iota(jnp.int32, (tq, tq), 1)

    kf = float(k_sel)
    nbits = int(t - 1).bit_length()

    def colsum(ind):
        return jnp.sum(jnp.sum(ind.reshape(tq // 32, 4, 8, tq), axis=0), axis=0)

    def tile(nc):
        chunks = [slice(c * tq, (c + 1) * tq) for c in range(nc)]
        diag = krow <= qlane

        for c, rows in enumerate(chunks):
            kk2 = ki_ref[0, rows, :]
            s_all = _mm(kk2, q_all, _NT)
            score = jnp.zeros((tq, tq), F32)
            for h in range(IDX_HEADS):
                score = score + wt[h:h + 1, :] * jnp.maximum(s_all[:, h * tq:(h + 1) * tq], 0.0)
            score = jnp.where(score == 0.0, 0.0, score)
            bits = pltpu.bitcast(score, jnp.int32)
            skey = bits ^ ((bits >> 31) & 0x7FFFFFFF)
            key_ref[rows, :] = jnp.where(diag, skey, INT_MIN) if c == nc - 1 else skey

        def count(fn):
            acc = jnp.zeros((8, tq), F32)
            for c, rows in enumerate(chunks):
                acc = acc + colsum(fn(key_ref[rows, :], c * tq + krow))
            return jnp.sum(acc, axis=0, keepdims=True)

        def count_ge(cand):
            return count(lambda keys, kpos: jnp.where(keys >= cand, 1.0, 0.0))

        thr0 = jnp.where(count_ge(jnp.zeros((1, tq), jnp.int32)) >= kf, 0, INT_MIN).astype(jnp.int32)

        def thr_step(n, thr):
            cand = thr | jnp.left_shift(jnp.int32(1), 30 - n)
            return jnp.where(count_ge(cand) >= kf, cand, thr)

        thr = lax.fori_loop(0, 31, thr_step, thr0)
        n_gt = count(lambda keys, kpos: jnp.where(keys > thr, 1.0, 0.0))
        n_eq = count(lambda keys, kpos: jnp.where(keys == thr, 1.0, 0.0))
        need = kf - n_gt

        def tie_cut():
            def cut_step(n, cut):
                cand = cut | jnp.left_shift(jnp.int32(1), nbits - 1 - n)
                cnt = count(lambda keys, kpos: jnp.where(keys == thr, jnp.where(kpos < cand, 1.0, 0.0), 0.0))
                return jnp.where(cnt < need, cand, cut)
            return lax.fori_loop(0, nbits, cut_step, jnp.zeros((1, tq), jnp.int32))

        cut = lax.cond(jnp.max(n_eq - need) > 0.0, tie_cut, lambda: jnp.full((1, tq), t, jnp.int32))

        for c, rows in enumerate(chunks):
            keys = key_ref[rows, :]
            tie = jnp.where(keys == thr, jnp.where(c * tq + krow <= cut, 0.0, MASK_NEG), MASK_NEG)
            bias = jnp.where(keys > thr, 0.0, tie)
            if c == nc - 1:
                bias = jnp.where(diag, bias, MASK_NEG)
            o_ref[0, rows, :] = bias.astype(BF16)
        for c in range(nc, nchunk):
            o_ref[0, c * tq:(c + 1) * tq, :] = jnp.full((tq, tq), MASK_NEG, BF16)

    for nc in range(1, nchunk + 1):
        pl.when(i == nc - 1)(functools.partial(tile, nc))


def _dsa_index(qi, ki, wi, k_sel):
    bsz, t, nq = qi.shape
    tq = ATT_T
    return pl.pallas_call(
        functools.partial(_dsa_index_body, k_sel),
        grid=(bsz, t // tq),
        in_specs=[pl.BlockSpec((1, tq, nq), lambda b, i: (b, i, 0)),
                  pl.BlockSpec((1, tq, LANES), lambda b, i: (b, i, 0)),
                  pl.BlockSpec((1, t, LANES), lambda b, i: (b, 0, 0))],
        out_specs=pl.BlockSpec((1, t, tq), lambda b, i: (b, 0, i)),
        out_shape=jax.ShapeDtypeStruct((bsz, t, t), BF16),
        scratch_shapes=[pltpu.VMEM((t, tq), jnp.int32)],
        compiler_params=_cparams(("parallel", "parallel")),
        name="dsa_index",
    )(qi, wi, ki)


def _t5_bucket_np(n):
    n = np.maximum(n, 0)
    max_exact = REL_BUCKETS // 2
    nf = np.maximum(n, 1).astype(np.float32)
    large = max_exact + (np.log(nf / np.float32(max_exact)) / np.float32(math.log(REL_MAX_DIST / max_exact))
                         * np.float32(REL_BUCKETS - max_exact)).astype(np.int32)
    large = np.minimum(large, REL_BUCKETS - 1)
    return np.where(n < max_exact, n, large).astype(np.int32)


def _band_body(bkt_ref, rb_ref, o_ref):
    h = pl.program_id(1)
    bkt = bkt_ref[0]
    acc = jnp.zeros(bkt.shape, F32)
    for b in range(REL_BUCKETS):
        acc = jnp.where(bkt == b, rb_ref[b, h], acc)
    o_ref[0] = acc * LOG2E


def _band_bias(rel_bias):
    tt = ATT_T
    kc = np.arange(tt)[:, None]
    qr = np.arange(tt)[None, :]
    planes = [_t5_bucket_np(d * tt + qr - kc) for d in range(3)]
    assert (planes[2] == REL_BUCKETS - 1).all() and tt + 1 >= 113
    bkt = jnp.asarray(np.stack(planes))
    return pl.pallas_call(
        _band_body,
        grid=(3, ATT_HEADS),
        in_specs=[pl.BlockSpec((1, tt, tt), lambda d, h: (d, 0, 0)),
                  pl.BlockSpec(memory_space=pltpu.SMEM)],
        out_specs=pl.BlockSpec((1, tt, tt), lambda d, h: (d, 0, h)),
        out_shape=jax.ShapeDtypeStruct((3, tt, ATT_HEADS * tt), F32),
        compiler_params=_cparams(("parallel", "parallel")),
        name="band_bias",
    )(bkt, rel_bias)


LOG2E = 1.4426950408889634
ACC_ROWS = KV_LATENT + 16


def _dsa_attn_body(qt_ref, kt_ref, q_ref, ckv_ref, mask_ref, wuk_ref, wuv_ref, near_ref, far_ref,
                   x_ref, mod_ref, wout_ref, lng_ref, lnb_ref, o_ref,
                   ql_ref, m_ref, sm_ref, ot_ref, *acc_refs):
    i = qt_ref[pl.program_id(1)]
    kp = kt_ref[pl.program_id(1)]
    gap = i - 2 * kp
    nh = ATT_HEADS
    tq = q_ref.shape[1]
    tk = tq
    qscale = ATT_HEAD_DIM ** -0.5 * LOG2E

    @pl.when(kp == 0)
    def _():
        for hp in range(nh // 2):
            qp = q_ref[0, :, hp * LANES:(hp + 1) * LANES].astype(BF16)
            qlat = _mm(qp, wuk_ref[hp]) * qscale
            ql_ref[2 * hp * tq:(2 * hp + 1) * tq, :] = qlat[:, 0:KV_LATENT].astype(BF16)
            ql_ref[(2 * hp + 1) * tq:(2 * hp + 2) * tq, :] = qlat[:, KV_LATENT:2 * KV_LATENT].astype(BF16)
        m_ref[...] = jnp.full(m_ref.shape, MASK_NEG, F32)
        for acc_ref in acc_refs:
            acc_ref[...] = jnp.zeros_like(acc_ref)

    def step(*kinds):
        nk = len(kinds) * tk
        ckv = ckv_ref[0, 0:nk, :]
        ckv_aug = jnp.concatenate([ckv.astype(F32).T, jnp.ones((ACC_ROWS - KV_LATENT, nk), F32)],
                                  axis=0).astype(BF16)
        maskb = mask_ref[0, 0:nk, :].astype(F32)
        m_prev = m_ref[...]
        m_news = []
        for h in range(nh):
            hs = slice(h * tq, (h + 1) * tq)
            s = _mm(ckv, ql_ref[hs, :], _NT)
            m_new = m_prev[:, hs]
            for n, plane in enumerate(kinds):
                rows = slice(n * tk, (n + 1) * tk)
                sm = s[rows] + maskb[rows]
                if plane is None:
                    m_new = jnp.maximum(m_new, jnp.max(sm, axis=0, keepdims=True) + far_ref[:, hs])
                else:
                    sm = sm + near_ref[plane, :, hs]
                    m_new = jnp.maximum(m_new, jnp.max(sm, axis=0, keepdims=True))
                sm_ref[rows, hs] = sm
            m_news.append(m_new)
        for h in range(nh):
            hs = slice(h * tq, (h + 1) * tq)
            m_new = m_news[h]
            p = jnp.concatenate(
                [jnp.exp2(sm_ref[n * tk:(n + 1) * tk, hs]
                          - (m_new - far_ref[:, hs] if plane is None else m_new)).astype(BF16)
                 for n, plane in enumerate(kinds)], axis=0)
            alpha = jnp.exp2(m_prev[:, hs] - m_new)
            acc_refs[h][...] = alpha * acc_refs[h][...] + _mm(ckv_aug, p)
        m_ref[...] = jnp.concatenate(m_news, axis=1)

    @pl.when(gap >= 3)
    def _():
        step(None, None)

    @pl.when(gap == 2)
    def _():
        step(None, 1)

    @pl.when(gap == 1)
    def _():
        step(1, 0)

    @pl.when(gap == 0)
    def _():
        step(0)

    @pl.when(gap <= 1)
    def _():
        def norm(h):
            a = acc_refs[h][...]
            return a[0:KV_LATENT] * (1.0 / a[KV_LATENT:KV_LATENT + 1])
        for hp in range(nh // 2):
            olat = jnp.concatenate([norm(2 * hp), norm(2 * hp + 1)], axis=0).astype(BF16)
            ot_ref[hp * LANES:(hp + 1) * LANES, :] = _mm(wuv_ref[hp], olat)
        y = _mm(ot_ref[...].T.astype(BF16), wout_ref[...])
        res = DEEPNORM_ALPHA * x_ref[0] + (1.0 + mod_ref[0, 2:3, :]) * y
        o_ref[0] = _layernorm(res, lng_ref[...], lnb_ref[...])


def _dsa_attn(q, ckv, maskt, w_uk, w_uv, band, x, mod, w_out, ln_g, ln_b):
    bsz, t, d = q.shape
    tt = ATT_T
    nt = t // tt
    nh = ATT_HEADS
    zk = jnp.zeros((nh // 2, ATT_HEAD_DIM, KV_LATENT), F32)
    wuk2 = jnp.concatenate([jnp.concatenate([w_uk[0::2], zk], axis=2),
                            jnp.concatenate([zk, w_uk[1::2]], axis=2)], axis=1).astype(BF16)
    wuv_t = jnp.swapaxes(w_uv, 1, 2)
    zv = jnp.zeros((nh // 2, ATT_HEAD_DIM, KV_LATENT), F32)
    wuv2 = jnp.concatenate([jnp.concatenate([wuv_t[0::2], zv], axis=2),
                            jnp.concatenate([zv, wuv_t[1::2]], axis=2)], axis=1).astype(BF16)
    assert nt % 2 == 0
    steps = [(i, kp) for i in range(nt) for kp in range(i // 2 + 1)]
    q_tab = jnp.asarray([p[0] for p in steps], jnp.int32)
    k_tab = jnp.asarray([p[1] for p in steps], jnp.int32)
    grid_spec = pltpu.PrefetchScalarGridSpec(
        num_scalar_prefetch=2,
        grid=(bsz, len(steps)),
        in_specs=[pl.BlockSpec((1, tt, d), lambda b, s, qt, kt: (b, qt[s], 0)),
                  pl.BlockSpec((1, 2 * tt, KV_LATENT), lambda b, s, qt, kt: (b, kt[s], 0)),
                  pl.BlockSpec((1, 2 * tt, tt), lambda b, s, qt, kt: (b, kt[s], qt[s])),
                  pl.BlockSpec((nh // 2, LANES, 2 * KV_LATENT), lambda b, s, qt, kt: (0, 0, 0)),
                  pl.BlockSpec((nh // 2, LANES, 2 * KV_LATENT), lambda b, s, qt, kt: (0, 0, 0)),
                  pl.BlockSpec((2, tt, nh * tt), lambda b, s, qt, kt: (0, 0, 0)),
                  pl.BlockSpec((1, nh * tt), lambda b, s, qt, kt: (0, 0)),
                  pl.BlockSpec((1, tt, d), lambda b, s, qt, kt: (b, qt[s], 0)),
                  pl.BlockSpec((1, 6, d), lambda b, s, qt, kt: (b, 0, 0)),
                  pl.BlockSpec((d, d), lambda b, s, qt, kt: (0, 0)),
                  pl.BlockSpec((1, d), lambda b, s, qt, kt: (0, 0)),
                  pl.BlockSpec((1, d), lambda b, s, qt, kt: (0, 0))],
        out_specs=pl.BlockSpec((1, tt, d), lambda b, s, qt, kt: (b, qt[s], 0)),
        scratch_shapes=[pltpu.VMEM((nh * tt, KV_LATENT), BF16),
                        pltpu.VMEM((1, nh * tt), F32),
                        pltpu.VMEM((2 * tt, nh * tt), F32),
                        pltpu.VMEM((d, tt), F32)]
        + [pltpu.VMEM((ACC_ROWS, tt), F32)] * nh)
    return pl.pallas_call(
        _dsa_attn_body,
        grid_spec=grid_spec,
        out_shape=jax.ShapeDtypeStruct((bsz, t, d), F32),
        compiler_params=_cparams(("parallel", "arbitrary")),
        name="dsa_attn",
    )(q_tab, k_tab, q, ckv, maskt, wuk2, wuv2, band[:2], band[2, 0:1, :],
      x, mod, w_out, ln_g.reshape(1, d), ln_b.reshape(1, d))


def kernel(x, c, ada_w, ada_b, ln_g, ln_b, ffn_w_in, ffn_w_out, rwkv_mu, rwkv_w_rkv, rwkv_w0, rwkv_w1, rwkv_w2, rwkv_a0, rwkv_a1, rwkv_a2, rwkv_v0, rwkv_v1, rwkv_v2, rwkv_g1, rwkv_g2, rwkv_k_k, rwkv_k_a, rwkv_r_k, rwkv_lnx_g, rwkv_lnx_b, rwkv_w_out, dsa_w_in, dsa_kv_norm, dsa_w_uk, dsa_w_uv, dsa_w_out, rel_bias):
    bsz, t, d = x.shape
    assert d == D_MODEL and t % ATT_T == 0 and t % FFN_TM == 0 and t % PRE_TM == 0
    mod_all = _adaln(c, ada_w, ada_b).reshape(DEPTH, bsz, 6, d)
    band = _band_bias(rel_bias)
    k_sel = min(TOPK_MAX, t // TOPK_DIV)
    bf = lambda w: w.astype(BF16)
    v_first = None
    for i in range(DEPTH):
        mod = mod_all[i]
        j = i // 2
        if i % 2 == 0:
            p = dict(mu=rwkv_mu[j], wr=bf(rwkv_w_rkv[j, 0]), wk=bf(rwkv_w_rkv[j, 1]), wv=bf(rwkv_w_rkv[j, 2]),
                     w0=rwkv_w0[j], w1=bf(rwkv_w1[j]), w2=bf(rwkv_w2[j]),
                     a0=rwkv_a0[j], a1=bf(rwkv_a1[j]), a2=bf(rwkv_a2[j]),
                     g1=bf(rwkv_g1[j]), g2=bf(rwkv_g2[j]), k_k=rwkv_k_k[j], k_a=rwkv_k_a[j])
            if j > 0:
                p.update(v0=rwkv_v0[j - 1], v1=bf(rwkv_v1[j - 1]), v2=bf(rwkv_v2[j - 1]))
            r_s, k_s, a_s, b_s, v, g, gl = _rwkv_pre(x, mod, p, v_first if j > 0 else None)
            if j == 0:
                v_first = v
            z = _rwkv_scan(r_s, k_s, a_s, b_s, v, gl, rwkv_r_k[j], rwkv_lnx_g[j], rwkv_lnx_b[j])
            tail = (z, g, bf(rwkv_w_out[j]), ln_g[i, 0], ln_b[i, 0])
        else:
            tail = None
            q, ckv, qi, ki, wi = _dsa_proj(x, mod, dsa_w_in[j], dsa_kv_norm[j])
            maskt = _dsa_index(qi, ki, wi, k_sel)
            x = _dsa_attn(q, ckv, maskt, dsa_w_uk[j], dsa_w_uv[j], band,
                          x, mod, bf(dsa_w_out[j]), ln_g[i, 0], ln_b[i, 0])
        x = _ffn(x, mod, bf(ffn_w_in[i]), bf(ffn_w_out[i]), ln_g[i, 1], ln_b[i, 1], tail)
    return x
```

```python
import functools
import math

import numpy as np
import jax
import jax.numpy as jnp
from jax import lax
from jax.experimental import pallas as pl
from jax.experimental.pallas import tpu as pltpu

F32 = jnp.float32
BF16 = jnp.bfloat16

D_MODEL = 1024
DEPTH = 4
RWKV_HEAD = 64
RWKV_HEADS = D_MODEL // RWKV_HEAD
GN_EPS = RWKV_HEAD * 1e-5
ATT_HEADS = 16
ATT_HEAD_DIM = 64
KV_LATENT = 128
IDX_HEADS = 8
IDX_DIM = 64
TOPK_MAX = 256
TOPK_DIV = 4
REL_BUCKETS = 32
REL_MAX_DIST = 128
FFN_HIDDEN = 2816
DEEPNORM_ALPHA = (2 * DEPTH) ** 0.25
LN_EPS = 1e-5

LANES = 128
CHUNK = 128
PRE_TM = 256
PROJ_TM = 512
FFN_TM = 512
FFN_TF = 1408
ATT_T = 256
MASK_NEG = -1e30
VMEM_LIMIT = 56 * 1024 * 1024
INT_MIN = -2 ** 31


def _cparams(sem):
    return pltpu.CompilerParams(dimension_semantics=sem, vmem_limit_bytes=VMEM_LIMIT)


def _split2(x):
    hi = x.astype(BF16)
    lo = (x - hi.astype(F32)).astype(BF16)
    return hi, lo


def _split3(x):
    hi = x.astype(BF16)
    r1 = x - hi.astype(F32)
    mid = r1.astype(BF16)
    lo = (r1 - mid.astype(F32)).astype(BF16)
    return hi, mid, lo


_NN = (((1,), (0,)), ((), ()))
_NT = (((1,), (1,)), ((), ()))


def _mm(a, b, dims=_NN):
    return lax.dot_general(a, b, dims, preferred_element_type=F32)


def _dot3(a, b, dims=_NN):
    ah, al = _split2(a)
    bh, bl = _split2(b)
    return _mm(ah, bh, dims) + (_mm(ah, bl, dims) + _mm(al, bh, dims))


def _dot1(a, b, dims=_NN):
    return _mm(a.astype(BF16), b.astype(BF16), dims)


def _dot_hilo_rhs(a, b_bf16):
    h, l = _split2(a)
    return _mm(h, b_bf16) + _mm(l, b_bf16)


def _sigmoid(x):
    return 1.0 / (1.0 + jnp.exp(-x))


def _layernorm(xr, g, b):
    mu = jnp.mean(xr, axis=-1, keepdims=True)
    xc = xr - mu
    var = jnp.mean(xc * xc, axis=-1, keepdims=True)
    return xc * lax.rsqrt(var + LN_EPS) * g + b


def _adaln_body(c_ref, w_ref, b_ref, o_ref):
    c = c_ref[...]
    cond = c * _sigmoid(c)
    o_ref[0] = _dot3(cond, w_ref[0]) + b_ref[0]


def _adaln(c, ada_w, ada_b):
    depth, d, n = ada_w.shape
    bsz = c.shape[0]
    tn = n // 4
    return pl.pallas_call(
        _adaln_body,
        grid=(depth, n // tn),
        in_specs=[
            pl.BlockSpec((bsz, d), lambda i, j: (0, 0)),
            pl.BlockSpec((1, d, tn), lambda i, j: (i, 0, j)),
            pl.BlockSpec((1, 1, tn), lambda i, j: (i, 0, j)),
        ],
        out_specs=pl.BlockSpec((1, bsz, tn), lambda i, j: (i, 0, j)),
        out_shape=jax.ShapeDtypeStruct((depth, bsz, n), F32),
        compiler_params=_cparams(("parallel", "parallel")),
        name="adaln",
    )(c, ada_w, ada_b.reshape(depth, 1, n))


def _ffn_body(mixer_tail, *refs):
    if mixer_tail:
        (x_ref, mod_ref, wg_ref, wu_ref, wo_ref, lng_ref, lnb_ref,
         z_ref, g_ref, wm_ref, lng0_ref, lnb0_ref, o_ref, hin_ref, acc_ref, xm_ref) = refs
    else:
        x_ref, mod_ref, wg_ref, wu_ref, wo_ref, lng_ref, lnb_ref, o_ref, hin_ref, acc_ref = refs
    j = pl.program_id(2)

    @pl.when(j == 0)
    def _():
        x = x_ref[0]
        if mixer_tail:
            y = _mm((z_ref[0] * g_ref[0]).astype(BF16), wm_ref[...])
            x = _layernorm(DEEPNORM_ALPHA * x + (1.0 + mod_ref[0, 2:3, :]) * y, lng0_ref[...], lnb0_ref[...])
            xm_ref[...] = x
        hin_ref[...] = (x * (1.0 + mod_ref[0, 4:5, :]) + mod_ref[0, 3:4, :]).astype(BF16)
        acc_ref[...] = jnp.zeros_like(acc_ref)

    hin = hin_ref[...]
    gate = _mm(hin, wg_ref[...])
    up = _mm(hin, wu_ref[...])
    hid = (gate * _sigmoid(gate) * up).astype(BF16)
    acc_ref[...] += _mm(hid, wo_ref[...])

    @pl.when(j == pl.num_programs(2) - 1)
    def _():
        x = xm_ref[...] if mixer_tail else x_ref[0]
        res = DEEPNORM_ALPHA * x + (1.0 + mod_ref[0, 5:6, :]) * acc_ref[...]
        o_ref[0] = _layernorm(res, lng_ref[...], lnb_ref[...])


def _ffn(x, mod, w_in, w_out, ln_g, ln_b, mixer_tail=None):
    bsz, t, d = x.shape
    f = w_out.shape[0]
    nf = f // FFN_TF
    row = pl.BlockSpec((1, FFN_TM, d), lambda b, i, j: (b, i, 0))
    vec = pl.BlockSpec((1, d), lambda b, i, j: (0, 0))
    ins = [x, mod, w_in, w_in, w_out, ln_g.reshape(1, d), ln_b.reshape(1, d)]
    specs = [row,
             pl.BlockSpec((1, 6, d), lambda b, i, j: (b, 0, 0)),
             pl.BlockSpec((d, FFN_TF), lambda b, i, j: (0, j)),
             pl.BlockSpec((d, FFN_TF), lambda b, i, j: (0, nf + j)),
             pl.BlockSpec((FFN_TF, d), lambda b, i, j: (j, 0)),
             vec, vec]
    scratch = [pltpu.VMEM((FFN_TM, d), BF16), pltpu.VMEM((FFN_TM, d), F32)]
    if mixer_tail is not None:
        z, g, w_mix, ln_g0, ln_b0 = mixer_tail
        ins += [z, g, w_mix, ln_g0.reshape(1, d), ln_b0.reshape(1, d)]
        specs += [row, row, pl.BlockSpec((d, d), lambda b, i, j: (0, 0)), vec, vec]
        scratch += [pltpu.VMEM((FFN_TM, d), F32)]
    return pl.pallas_call(
        functools.partial(_ffn_body, mixer_tail is not None),
        grid=(bsz, t // FFN_TM, nf),
        in_specs=specs,
        out_specs=row,
        out_shape=jax.ShapeDtypeStruct((bsz, t, d), F32),
        scratch_shapes=scratch,
        compiler_params=_cparams(("parallel", "parallel", "arbitrary")),
        name="ffn",
    )(*ins)


def _rwkv_pre_body(has_vres, *refs):
    if has_vres:
        (x_ref, xp_ref, mod_ref, mu_ref, wr_ref, wk_ref, wv_ref, w1_ref, w2_ref, a1_ref, a2_ref,
         g1_ref, g2_ref, vec_ref, seg_ref, segt_ref, tri_ref, vf_ref, v1_ref, v2_ref,
         r_o, k_o, a_o, b_o, v_o, g_o, gl_o, mix_ref) = refs
    else:
        (x_ref, xp_ref, mod_ref, mu_ref, wr_ref, wk_ref, wv_ref, w1_ref, w2_ref, a1_ref, a2_ref,
         g1_ref, g2_ref, vec_ref, seg_ref, segt_ref, tri_ref,
         r_o, k_o, a_o, b_o, v_o, g_o, gl_o, mix_ref) = refs
    i = pl.program_id(1)
    sc = 1.0 + mod_ref[0, 1:2, :]
    sh = mod_ref[0, 0:1, :]
    hin = x_ref[0] * sc + sh
    tm, d = hin.shape
    prev_row = xp_ref[0, 7:8, :] * sc + sh
    prev_row = jnp.where(i == 0, 0.0, prev_row)
    rows = lax.broadcasted_iota(jnp.int32, hin.shape, 0)
    hprev = jnp.where(rows == 0, prev_row, pltpu.roll(hin, 1, 0))
    xx = hprev - hin
    for p in range(6):
        mix_ref[p] = (hin + xx * mu_ref[p:p + 1, :]).astype(BF16)
    lora_w = jnp.tanh(_mm(mix_ref[3], w1_ref[...])).astype(BF16)
    lora_a = _mm(mix_ref[4], a1_ref[...]).astype(BF16)
    lora_g = _sigmoid(_mm(mix_ref[5], g1_ref[...])).astype(BF16)
    if has_vres:
        lora_v = _mm(mix_ref[2], v1_ref[...]).astype(BF16)

    cw = 2 * LANES
    for cb in range(d // cw):
        cols = slice(cb * cw, (cb + 1) * cw)
        w0, a0, kkw, kaw = vec_ref[0:1, cols], vec_ref[1:2, cols], vec_ref[2:3, cols], vec_ref[3:4, cols]
        r = _mm(mix_ref[0], wr_ref[:, cols])
        k = _mm(mix_ref[1], wk_ref[:, cols])
        v = _mm(mix_ref[2], wv_ref[:, cols])

        wl = w0 + _mm(lora_w, w2_ref[:, cols])
        nz = -wl
        softplus = jnp.maximum(nz, 0.0) + jnp.log(1.0 + jnp.exp(-jnp.abs(nz)))
        logdec = -jnp.exp(-softplus - 0.5)

        a = _sigmoid(a0 + _mm(lora_a, a2_ref[:, cols]))
        if has_vres:
            vmix = _sigmoid(vec_ref[4:5, cols] + _mm(lora_v, v2_ref[:, cols]))
            v = v + (vf_ref[0, :, cols] - v) * vmix
        g = _mm(lora_g, g2_ref[:, cols])

        kk = k * kkw
        ss = _dot_hilo_rhs(kk * kk, seg_ref[...])
        nrm = jnp.maximum(jnp.sqrt(ss), 1e-12)
        kk = kk * _dot_hilo_rhs(1.0 / nrm, segt_ref[...])
        k = k * (1.0 + (a - 1.0) * kaw)

        cum = _dot_exact_rhs_lhs(tri_ref[...], logdec)
        ginc = jnp.exp(cum)
        ginv = jnp.exp(-cum)
        gprev = jnp.exp(cum - logdec)

        r_o[0, :, cols] = r * ginc
        k_o[0, :, cols] = k * ginv
        a_o[0, :, cols] = -kk * gprev
        b_o[0, :, cols] = kk * a * ginv
        v_o[0, :, cols] = v
        g_o[0, :, cols] = g
        for cc in range(tm // CHUNK):
            first, last = cc * CHUNK, cc * CHUNK + CHUNK - 1
            gl_o[0, cc, 0:1, cols] = jnp.exp(logdec[first:first + 1, :] - cum[first:first + 1, :])
            gl_o[0, cc, 1:2, cols] = ginc[last:last + 1, :]


def _dot_exact_rhs_lhs(m_bf16, x):
    h, mid, l = _split3(x)
    return _mm(m_bf16, h) + (_mm(m_bf16, mid) + _mm(m_bf16, l))


def _rwkv_pre(x, mod, p, v_first):
    bsz, t, d = x.shape
    tm = PRE_TM
    has_vres = v_first is not None
    row = pl.BlockSpec((1, tm, d), lambda b, i: (b, i, 0))

    def full(shape):
        return pl.BlockSpec(shape, lambda b, i: (0,) * len(shape))

    cw = 2 * LANES
    seg = np.zeros((cw, LANES), np.float32)
    seg[np.arange(cw), np.arange(cw) // RWKV_HEAD] = 1.0
    idx = np.arange(tm)
    same = idx[:, None] // CHUNK == idx[None, :] // CHUNK
    col = idx[None, :] % CHUNK
    tri = (same & (col <= idx[:, None] % CHUNK)).astype(np.float32) \
        - (same & (col <= CHUNK // 2 - 1)).astype(np.float32)
    vec_rows = [p['w0'], p['a0'], p['k_k'], p['k_a']] + ([p['v0']] if has_vres else [])
    vec = jnp.stack(vec_rows + [jnp.zeros_like(p['w0'])] * (8 - len(vec_rows)))
    dl, da, dg = p['w1'].shape[1], p['a1'].shape[1], p['g1'].shape[1]
    ins = [x, x, mod, p['mu'], p['wr'], p['wk'], p['wv'], p['w1'], p['w2'], p['a1'], p['a2'],
           p['g1'], p['g2'], vec, jnp.asarray(seg, BF16), jnp.asarray(seg.T, BF16),
           jnp.asarray(tri, BF16)]
    specs = [row,
             pl.BlockSpec((1, 8, d), lambda b, i: (b, jnp.maximum(i * (tm // 8) - 1, 0), 0)),
             pl.BlockSpec((1, 6, d), lambda b, i: (b, 0, 0)),
             full((6, d)), full((d, d)), full((d, d)), full((d, d)),
             full((d, dl)), full((dl, d)), full((d, da)), full((da, d)),
             full((d, dg)), full((dg, d)), full((8, d)),
             full((cw, LANES)), full((LANES, cw)), full((tm, tm))]
    if has_vres:
        dv = p['v1'].shape[1]
        ins += [v_first, p['v1'], p['v2']]
        specs += [row, full((d, dv)), full((dv, d))]
    act = jax.ShapeDtypeStruct((bsz, t, d), F32)
    nch = t // CHUNK
    outs = pl.pallas_call(
        functools.partial(_rwkv_pre_body, has_vres),
        grid=(bsz, t // tm),
        in_specs=specs,
        out_specs=[row] * 6 + [pl.BlockSpec((1, tm // CHUNK, 2, d), lambda b, i: (b, i, 0, 0))],
        out_shape=[act] * 6 + [jax.ShapeDtypeStruct((bsz, nch, 2, d), F32)],
        scratch_shapes=[pltpu.VMEM((6, tm, d), BF16)],
        compiler_params=_cparams(("parallel", "parallel")),
        name="rwkv_pre",
    )(*ins)
    return outs


_BNN = (((2,), (1,)), ((0,), (0,)))
_BNT = (((2,), (2,)), ((0,), (0,)))


def _rwkv_scan_body(r_ref, k_ref, a_ref, b_ref, v_ref, gl_ref, vec_ref, o_ref, s_ref):
    c = pl.program_id(1)

    @pl.when(c == 0)
    def _():
        s_ref[...] = jnp.zeros_like(s_ref)

    L = CHUNK
    NP = RWKV_HEADS // 2
    shp = (NP, L, LANES)
    plane = lax.broadcasted_iota(jnp.int32, (1, L, 2 * L), 2)
    ph0 = plane < L
    tt = lax.broadcasted_iota(jnp.int32, (1, L, 2 * L), 1)
    ss = plane & (L - 1)
    strict = ss < tt
    incl = ss <= tt
    eye = jnp.where(ss == tt, 1.0, 0.0)
    base_bits = 4

    def same_block(bits):
        return (ss >> bits) == (tt >> bits)

    lvl0 = strict & same_block(base_bits)
    merges = [strict & same_block(bits + 1) & ((ss >> bits) != (tt >> bits))
              for bits in range(base_bits, (L - 1).bit_length())]
    dh0 = lax.broadcasted_iota(jnp.int32, (1, L, LANES), 2) < RWKV_HEAD
    ri = lax.broadcasted_iota(jnp.int32, (1, LANES, LANES), 1)
    ci = lax.broadcasted_iota(jnp.int32, (1, LANES, LANES), 2)
    blockdiag = (ri >> 6) == (ci >> 6)
    seg_mean = jnp.where(blockdiag[0], 1.0, 0.0).astype(BF16)

    def grp(ref):
        return jnp.stack([ref[0, :, hp * LANES:(hp + 1) * LANES] for hp in range(NP)])

    def bd(x):
        return jnp.concatenate([jnp.where(dh0, x, 0.0), jnp.where(dh0, 0.0, x)], axis=1)

    def bdp(y):
        return jnp.concatenate([jnp.where(ph0, y, 0.0), jnp.where(ph0, 0.0, y)], axis=1)

    def pmm(xp, y):
        return _dot1(xp, bd(y), _BNN)

    def ppm(xp, yp):
        return _dot1(xp, bdp(yp), _BNN)

    R, K, A, Bv, V = grp(r_ref), grp(k_ref), grp(a_ref), grp(b_ref), grp(v_ref)
    gvec = lambda n: jnp.stack([gl_ref[0, 0, n:n + 1, hp * LANES:(hp + 1) * LANES] for hp in range(NP)])
    S = s_ref[...] * gvec(0)

    AR = jnp.concatenate([A, R], axis=1)
    BK = jnp.concatenate([bd(Bv), bd(K)], axis=1)
    G = _dot1(AR, BK, _BNT)
    A_ab = jnp.where(strict, G[:, 0:L, 0:2 * L], 0.0)
    A_ak = jnp.where(strict, G[:, 0:L, 2 * L:4 * L], 0.0)
    A_rb = jnp.where(incl, G[:, L:2 * L, 0:2 * L], 0.0)
    A_rk = jnp.where(incl, G[:, L:2 * L, 2 * L:4 * L], 0.0)

    a0 = jnp.where(lvl0, A_ab, 0.0)
    Tm = eye + a0
    P = a0
    for _ in range(base_bits - 1):
        P = ppm(P, P)
        Tm = Tm + ppm(Tm, P)
    for lvl in merges:
        Tm = Tm + ppm(ppm(Tm, jnp.where(lvl, A_ab, 0.0)), Tm)

    PQ = _dot1(AR, S, _BNT)
    W = PQ[:, 0:L] + pmm(A_ak, V)
    U = pmm(Tm, W)
    Y = PQ[:, L:2 * L] + _dot1(jnp.concatenate([A_rb, A_rk], axis=2),
                               jnp.concatenate([bd(U), bd(V)], axis=1), _BNN)
    UV = jnp.concatenate([U, V], axis=1)
    UVt = jnp.stack([UV[hp].T for hp in range(NP)])
    BK2 = jnp.concatenate([Bv, K], axis=1)
    upd = _dot1(UVt, BK2, _BNN)
    s_ref[...] = (S + jnp.where(blockdiag, upd, 0.0)) * gvec(1)

    vrow = lambda n: jnp.stack([vec_ref[n:n + 1, hp * LANES:(hp + 1) * LANES] for hp in range(NP)])
    inv_n = 1.0 / RWKV_HEAD
    flat = lambda x: x.reshape(NP * L, LANES)
    mean = _dot_hilo_rhs(flat(Y), seg_mean).reshape(shp) * inv_n
    yc = Y - mean
    var = _dot_hilo_rhs(flat(yc * yc), seg_mean).reshape(shp) * inv_n
    yn = yc * lax.rsqrt(var + GN_EPS) * vrow(1) + vrow(2)
    bonus = _dot_hilo_rhs(flat(R * K * vrow(0)), seg_mean).reshape(shp)
    out = yn + bonus * V
    for hp in range(NP):
        o_ref[0, :, hp * LANES:(hp + 1) * LANES] = out[hp]


def _rwkv_scan(r, k, a, b, v, gl, r_k, lnx_g, lnx_b):
    bsz, t, d = r.shape
    row = pl.BlockSpec((1, CHUNK, d), lambda bb, c: (bb, c, 0))
    vec = jnp.stack([r_k.reshape(d), lnx_g, lnx_b] + [jnp.zeros((d,), F32)] * 5)
    return pl.pallas_call(
        _rwkv_scan_body,
        grid=(bsz, t // CHUNK),
        in_specs=[row] * 5 + [
            pl.BlockSpec((1, 1, 2, d), lambda bb, c: (bb, c, 0, 0)),
            pl.BlockSpec((8, d), lambda bb, c: (0, 0)),
        ],
        out_specs=row,
        out_shape=jax.ShapeDtypeStruct((bsz, t, d), F32),
        scratch_shapes=[pltpu.VMEM((RWKV_HEADS // 2, LANES, LANES), F32)],
        compiler_params=_cparams(("parallel", "arbitrary")),
        name="rwkv_scan",
    )(r, k, a, b, v, gl, vec)


IDX_COLS = 768


def _dsa_proj_body(x_ref, mod_ref, wq_ref, wc_ref, wi_ref, kvn_ref,
                   q_o, ckv_o, qi_o, ki_o, wi_o):
    hin = (x_ref[0] * (1.0 + mod_ref[0, 1:2, :]) + mod_ref[0, 0:1, :]).astype(BF16)
    q_o[0] = _mm(hin, wq_ref[...]).astype(BF16)
    ckv = _mm(hin, wc_ref[...])
    ms = jnp.mean(ckv * ckv, axis=-1, keepdims=True)
    ckv_o[0] = (ckv * lax.rsqrt(ms + 1e-6) * kvn_ref[...]).astype(BF16)
    idx = _mm(hin, wi_ref[...])
    nq = IDX_HEADS * IDX_DIM
    qi_o[0] = idx[:, 0:nq].astype(BF16)
    ki_o[0] = idx[:, nq:nq + LANES].astype(BF16)
    wi_o[0] = idx[:, nq + LANES:nq + 2 * LANES] * (IDX_HEADS ** -0.5 * IDX_DIM ** -0.5)


def _dsa_proj(x, mod, w_in, kv_norm):
    bsz, t, d = x.shape
    c1 = ATT_HEADS * ATT_HEAD_DIM
    c2 = c1 + KV_LATENT
    c3 = c2 + IDX_HEADS * IDX_DIM
    c4 = c3 + IDX_DIM
    wq = w_in[:, :c1].astype(BF16)
    wc = w_in[:, c1:c2].astype(BF16)
    pad = IDX_COLS - (c3 - c2) - 2 * IDX_DIM - IDX_HEADS
    widx = jnp.concatenate([w_in[:, c2:c3], w_in[:, c3:c4], w_in[:, c3:c4], w_in[:, c4:],
                            jnp.zeros((d, pad), F32)], axis=1).astype(BF16)
    tm = PROJ_TM

    def full(shape):
        return pl.BlockSpec(shape, lambda b, i: (0,) * len(shape))

    def row(n):
        return pl.BlockSpec((1, tm, n), lambda b, i: (b, i, 0))

    def act(n, dtype):
        return jax.ShapeDtypeStruct((bsz, t, n), dtype)

    nq = IDX_HEADS * IDX_DIM
    return pl.pallas_call(
        _dsa_proj_body,
        grid=(bsz, t // tm),
        in_specs=[row(d), pl.BlockSpec((1, 6, d), lambda b, i: (b, 0, 0)),
                  full((d, c1)), full((d, KV_LATENT)), full((d, IDX_COLS)),
                  full((1, KV_LATENT))],
        out_specs=[row(c1), row(KV_LATENT), row(nq), row(LANES), row(LANES)],
        out_shape=[act(c1, BF16), act(KV_LATENT, BF16), act(nq, BF16), act(LANES, BF16), act(LANES, F32)],
        compiler_params=_cparams(("parallel", "parallel")),
        name="dsa_proj",
    )(x, mod, wq, wc, widx, kv_norm.reshape(1, KV_LATENT))


def _dsa_index_body(k_sel, qi_ref, wi_ref, ki_ref, o_ref, key_ref):
    i = pl.program_id(1)
    tq = qi_ref.shape[1]
    t = ki_ref.shape[1]
    nchunk = t // tq
    wt = wi_ref[0].T
    lane = lax.broadcasted_iota(jnp.int32, (tq, LANES), 1)
    first = lane < IDX_DIM
    qheads = []
    for hp in range(IDX_HEADS // 2):
        qp = qi_ref[0, :, hp * LANES:(hp + 1) * LANES].astype(F32)
        qheads.append(jnp.where(first, qp, 0.0).astype(BF16))
        qheads.append(jnp.where(first, 0.0, qp).astype(BF16))
    q_all = jnp.concatenate(qheads, axis=0)
    krow = lax.broadcasted_iota(jnp.int32, (tq, tq), 0)
    qlane = lax.broadcasted_iota(jnp.int32, (tq, tq), 1)

    kf = float(k_sel)
    nbits = int(t - 1).bit_length()

    def colsum(ind):
        return jnp.sum(jnp.sum(ind.reshape(tq // 32, 4, 8, tq), axis=0), axis=0)

    def tile(nc):
        chunks = [slice(c * tq, (c + 1) * tq) for c in range(nc)]
        diag = krow <= qlane

        for c, rows in enumerate(chunks):
            kk2 = ki_ref[0, rows, :]
            s_all = _mm(kk2, q_all, _NT)
            score = jnp.zeros((tq, tq), F32)
            for h in range(IDX_HEADS):
                score = score + wt[h:h + 1, :] * jnp.maximum(s_all[:, h * tq:(h + 1) * tq], 0.0)
            score = jnp.where(score == 0.0, 0.0, score)
            bits = pltpu.bitcast(score, jnp.int32)
            skey = bits ^ ((bits >> 31) & 0x7FFFFFFF)
            key_ref[rows, :] = jnp.where(diag, skey, INT_MIN) if c == nc - 1 else skey

        def count(fn):
            acc = jnp.zeros((8, tq), F32)
            for c, rows in enumerate(chunks):
                acc = acc + colsum(fn(key_ref[rows, :], c * tq + krow))
            return jnp.sum(acc, axis=0, keepdims=True)

        def count_ge(cand):
            return count(lambda keys, kpos: jnp.where(keys >= cand, 1.0, 0.0))

        thr0 = jnp.where(count_ge(jnp.zeros((1, tq), jnp.int32)) >= kf, 0, INT_MIN).astype(jnp.int32)

        def thr_step(n, thr):
            cand = thr | jnp.left_shift(jnp.int32(1), 30 - n)
            return jnp.where(count_ge(cand) >= kf, cand, thr)

        thr = lax.fori_loop(0, 31, thr_step, thr0)
        n_gt = count(lambda keys, kpos: jnp.where(keys > thr, 1.0, 0.0))
        n_eq = count(lambda keys, kpos: jnp.where(keys == thr, 1.0, 0.0))
        need = kf - n_gt

        def tie_cut():
            def cut_step(n, cut):
                cand = cut | jnp.left_shift(jnp.int32(1), nbits - 1 - n)
                cnt = count(lambda keys, kpos: jnp.where(keys == thr, jnp.where(kpos < cand, 1.0, 0.0), 0.0))
                return jnp.where(cnt < need, cand, cut)
            return lax.fori_loop(0, nbits, cut_step, jnp.zeros((1, tq), jnp.int32))

        cut = lax.cond(jnp.max(n_eq - need) > 0.0, tie_cut, lambda: jnp.full((1, tq), t, jnp.int32))

        for c, rows in enumerate(chunks):
            keys = key_ref[rows, :]
            tie = jnp.where(keys == thr, jnp.where(c * tq + krow <= cut, 0.0, MASK_NEG), MASK_NEG)
            bias = jnp.where(keys > thr, 0.0, tie)
            if c == nc - 1:
                bias = jnp.where(diag, bias, MASK_NEG)
            o_ref[0, rows, :] = bias.astype(BF16)
        for c in range(nc, nchunk):
            o_ref[0, c * tq:(c + 1) * tq, :] = jnp.full((tq, tq), MASK_NEG, BF16)

    for nc in range(1, nchunk + 1):
        pl.when(i == nc - 1)(functools.partial(tile, nc))


def _dsa_index(qi, ki, wi, k_sel):
    bsz, t, nq = qi.shape
    tq = ATT_T
    return pl.pallas_call(
        functools.partial(_dsa_index_body, k_sel),
        grid=(bsz, t // tq),
        in_specs=[pl.BlockSpec((1, tq, nq), lambda b, i: (b, i, 0)),
                  pl.BlockSpec((1, tq, LANES), lambda b, i: (b, i, 0)),
                  pl.BlockSpec((1, t, LANES), lambda b, i: (b, 0, 0))],
        out_specs=pl.BlockSpec((1, t, tq), lambda b, i: (b, 0, i)),
        out_shape=jax.ShapeDtypeStruct((bsz, t, t), BF16),
        scratch_shapes=[pltpu.VMEM((t, tq), jnp.int32)],
        compiler_params=_cparams(("parallel", "parallel")),
        name="dsa_index",
    )(qi, wi, ki)


def _t5_bucket_np(n):
    n = np.maximum(n, 0)
    max_exact = REL_BUCKETS // 2
    nf = np.maximum(n, 1).astype(np.float32)
    large = max_exact + (np.log(nf / np.float32(max_exact)) / np.float32(math.log(REL_MAX_DIST / max_exact))
                         * np.float32(REL_BUCKETS - max_exact)).astype(np.int32)
    large = np.minimum(large, REL_BUCKETS - 1)
    return np.where(n < max_exact, n, large).astype(np.int32)


def _band_body(bkt_ref, rb_ref, o_ref):
    h = pl.program_id(1)
    bkt = bkt_ref[0]
    acc = jnp.zeros(bkt.shape, F32)
    for b in range(REL_BUCKETS):
        acc = jnp.where(bkt == b, rb_ref[b, h], acc)
    o_ref[0] = acc * LOG2E


def _band_bias(rel_bias):
    tt = ATT_T
    kc = np.arange(tt)[:, None]
    qr = np.arange(tt)[None, :]
    planes = [_t5_bucket_np(d * tt + qr - kc) for d in range(3)]
    assert (planes[2] == REL_BUCKETS - 1).all() and tt + 1 >= 113
    bkt = jnp.asarray(np.stack(planes))
    return pl.pallas_call(
        _band_body,
        grid=(3, ATT_HEADS),
        in_specs=[pl.BlockSpec((1, tt, tt), lambda d, h: (d, 0, 0)),
                  pl.BlockSpec(memory_space=pltpu.SMEM)],
        out_specs=pl.BlockSpec((1, tt, tt), lambda d, h: (d, 0, h)),
        out_shape=jax.ShapeDtypeStruct((3, tt, ATT_HEADS * tt), F32),
        compiler_params=_cparams(("parallel", "parallel")),
        name="band_bias",
    )(bkt, rel_bias)


LOG2E = 1.4426950408889634
ACC_ROWS = KV_LATENT + 16


def _dsa_attn_body(qt_ref, kt_ref, q_ref, ckv_ref, mask_ref, wuk_ref, wuv_ref, near_ref, far_ref,
                   x_ref, mod_ref, wout_ref, lng_ref, lnb_ref, o_ref,
                   ql_ref, m_ref, sm_ref, ot_ref, *acc_refs):
    i = qt_ref[pl.program_id(1)]
    kp = kt_ref[pl.program_id(1)]
    gap = i - 2 * kp
    nh = ATT_HEADS
    tq = q_ref.shape[1]
    tk = tq
    qscale = ATT_HEAD_DIM ** -0.5 * LOG2E

    @pl.when(kp == 0)
    def _():
        for hp in range(nh // 2):
            qp = q_ref[0, :, hp * LANES:(hp + 1) * LANES].astype(BF16)
            qlat = _mm(qp, wuk_ref[hp]) * qscale
            ql_ref[2 * hp * tq:(2 * hp + 1) * tq, :] = qlat[:, 0:KV_LATENT].astype(BF16)
            ql_ref[(2 * hp + 1) * tq:(2 * hp + 2) * tq, :] = qlat[:, KV_LATENT:2 * KV_LATENT].astype(BF16)
        m_ref[...] = jnp.full(m_ref.shape, MASK_NEG, F32)
        for acc_ref in acc_refs:
            acc_ref[...] = jnp.zeros_like(acc_ref)

    def step(*kinds):
        nk = len(kinds) * tk
        ckv = ckv_ref[0, 0:nk, :]
        ckv_aug = jnp.concatenate([ckv.astype(F32).T, jnp.ones((ACC_ROWS - KV_LATENT, nk), F32)],
                                  axis=0).astype(BF16)
        maskb = mask_ref[0, 0:nk, :].astype(F32)
        m_prev = m_ref[...]
        m_news = []
        for h in range(nh):
            hs = slice(h * tq, (h + 1) * tq)
            s = _mm(ckv, ql_ref[hs, :], _NT)
            m_new = m_prev[:, hs]
            for n, plane in enumerate(kinds):
                rows = slice(n * tk, (n + 1) * tk)
                sm = s[rows] + maskb[rows]
                if plane is None:
                    m_new = jnp.maximum(m_new, jnp.max(sm, axis=0, keepdims=True) + far_ref[:, hs])
                else:
                    sm = sm + near_ref[plane, :, hs]
                    m_new = jnp.maximum(m_new, jnp.max(sm, axis=0, keepdims=True))
                sm_ref[rows, hs] = sm
            m_news.append(m_new)
        for h in range(nh):
            hs = slice(h * tq, (h + 1) * tq)
            m_new = m_news[h]
            p = jnp.concatenate(
                [jnp.exp2(sm_ref[n * tk:(n + 1) * tk, hs]
                          - (m_new - far_ref[:, hs] if plane is None else m_new)).astype(BF16)
                 for n, plane in enumerate(kinds)], axis=0)
            alpha = jnp.exp2(m_prev[:, hs] - m_new)
            acc_refs[h][...] = alpha * acc_refs[h][...] + _mm(ckv_aug, p)
        m_ref[...] = jnp.concatenate(m_news, axis=1)

    @pl.when(gap >= 3)
    def _():
        step(None, None)

    @pl.when(gap == 2)
    def _():
        step(None, 1)

    @pl.when(gap == 1)
    def _():
        step(1, 0)

    @pl.when(gap == 0)
    def _():
        step(0)

    @pl.when(gap <= 1)
    def _():
        def norm(h):
            a = acc_refs[h][...]
            return a[0:KV_LATENT] * (1.0 / a[KV_LATENT:KV_LATENT + 1])
        for hp in range(nh // 2):
            olat = jnp.concatenate([norm(2 * hp), norm(2 * hp + 1)], axis=0).astype(BF16)
            ot_ref[hp * LANES:(hp + 1) * LANES, :] = _mm(wuv_ref[hp], olat)
        y = _mm(ot_ref[...].T.astype(BF16), wout_ref[...])
        res = DEEPNORM_ALPHA * x_ref[0] + (1.0 + mod_ref[0, 2:3, :]) * y
        o_ref[0] = _layernorm(res, lng_ref[...], lnb_ref[...])


def _dsa_attn(q, ckv, maskt, w_uk, w_uv, band, x, mod, w_out, ln_g, ln_b):
    bsz, t, d = q.shape
    tt = ATT_T
    nt = t // tt
    nh = ATT_HEADS
    zk = jnp.zeros((nh // 2, ATT_HEAD_DIM, KV_LATENT), F32)
    wuk2 = jnp.concatenate([jnp.concatenate([w_uk[0::2], zk], axis=2),
                            jnp.concatenate([zk, w_uk[1::2]], axis=2)], axis=1).astype(BF16)
    wuv_t = jnp.swapaxes(w_uv, 1, 2)
    zv = jnp.zeros((nh // 2, ATT_HEAD_DIM, KV_LATENT), F32)
    wuv2 = jnp.concatenate([jnp.concatenate([wuv_t[0::2], zv], axis=2),
                            jnp.concatenate([zv, wuv_t[1::2]], axis=2)], axis=1).astype(BF16)
    assert nt % 2 == 0
    steps = [(i, kp) for i in range(nt) for kp in range(i // 2 + 1)]
    q_tab = jnp.asarray([p[0] for p in steps], jnp.int32)
    k_tab = jnp.asarray([p[1] for p in steps], jnp.int32)
    grid_spec = pltpu.PrefetchScalarGridSpec(
        num_scalar_prefetch=2,
        grid=(bsz, len(steps)),
        in_specs=[pl.BlockSpec((1, tt, d), lambda b, s, qt, kt: (b, qt[s], 0)),
                  pl.BlockSpec((1, 2 * tt, KV_LATENT), lambda b, s, qt, kt: (b, kt[s], 0)),
                  pl.BlockSpec((1, 2 * tt, tt), lambda b, s, qt, kt: (b, kt[s], qt[s])),
                  pl.BlockSpec((nh // 2, LANES, 2 * KV_LATENT), lambda b, s, qt, kt: (0, 0, 0)),
                  pl.BlockSpec((nh // 2, LANES, 2 * KV_LATENT), lambda b, s, qt, kt: (0, 0, 0)),
                  pl.BlockSpec((2, tt, nh * tt), lambda b, s, qt, kt: (0, 0, 0)),
                  pl.BlockSpec((1, nh * tt), lambda b, s, qt, kt: (0, 0)),
                  pl.BlockSpec((1, tt, d), lambda b, s, qt, kt: (b, qt[s], 0)),
                  pl.BlockSpec((1, 6, d), lambda b, s, qt, kt: (b, 0, 0)),
                  pl.BlockSpec((d, d), lambda b, s, qt, kt: (0, 0)),
                  pl.BlockSpec((1, d), lambda b, s, qt, kt: (0, 0)),
                  pl.BlockSpec((1, d), lambda b, s, qt, kt: (0, 0))],
        out_specs=pl.BlockSpec((1, tt, d), lambda b, s, qt, kt: (b, qt[s], 0)),
        scratch_shapes=[pltpu.VMEM((nh * tt, KV_LATENT), BF16),
                        pltpu.VMEM((1, nh * tt), F32),
                        pltpu.VMEM((2 * tt, nh * tt), F32),
                        pltpu.VMEM((d, tt), F32)]
        + [pltpu.VMEM((ACC_ROWS, tt), F32)] * nh)
    return pl.pallas_call(
        _dsa_attn_body,
        grid_spec=grid_spec,
        out_shape=jax.ShapeDtypeStruct((bsz, t, d), F32),
        compiler_params=_cparams(("parallel", "arbitrary")),
        name="dsa_attn",
    )(q_tab, k_tab, q, ckv, maskt, wuk2, wuv2, band[:2], band[2, 0:1, :],
      x, mod, w_out, ln_g.reshape(1, d), ln_b.reshape(1, d))


def kernel(x, c, ada_w, ada_b, ln_g, ln_b, ffn_w_in, ffn_w_out, rwkv_mu, rwkv_w_rkv, rwkv_w0, rwkv_w1, rwkv_w2, rwkv_a0, rwkv_a1, rwkv_a2, rwkv_v0, rwkv_v1, rwkv_v2, rwkv_g1, rwkv_g2, rwkv_k_k, rwkv_k_a, rwkv_r_k, rwkv_lnx_g, rwkv_lnx_b, rwkv_w_out, dsa_w_in, dsa_kv_norm, dsa_w_uk, dsa_w_uv, dsa_w_out, rel_bias):
    bsz, t, d = x.shape
    assert d == D_MODEL and t % ATT_T == 0 and t % FFN_TM == 0 and t % PRE_TM == 0
    mod_all = _adaln(c, ada_w, ada_b).reshape(DEPTH, bsz, 6, d)
    band = _band_bias(rel_bias)
    k_sel = min(TOPK_MAX, t // TOPK_DIV)
    bf = lambda w: w.astype(BF16)
    v_first = None
    for i in range(DEPTH):
        mod = mod_all[i]
        j = i // 2
        if i % 2 == 0:
            p = dict(mu=rwkv_mu[j], wr=bf(rwkv_w_rkv[j, 0]), wk=bf(rwkv_w_rkv[j, 1]), wv=bf(rwkv_w_rkv[j, 2]),
                     w0=rwkv_w0[j], w1=bf(rwkv_w1[j]), w2=bf(rwkv_w2[j]),
                     a0=rwkv_a0[j], a1=bf(rwkv_a1[j]), a2=bf(rwkv_a2[j]),
                     g1=bf(rwkv_g1[j]), g2=bf(rwkv_g2[j]), k_k=rwkv_k_k[j], k_a=rwkv_k_a[j])
            if j > 0:
                p.update(v0=rwkv_v0[j - 1], v1=bf(rwkv_v1[j - 1]), v2=bf(rwkv_v2[j - 1]))
            r_s, k_s, a_s, b_s, v, g, gl = _rwkv_pre(x, mod, p, v_first if j > 0 else None)
            if j == 0:
                v_first = v
            z = _rwkv_scan(r_s, k_s, a_s, b_s, v, gl, rwkv_r_k[j], rwkv_lnx_g[j], rwkv_lnx_b[j])
            tail = (z, g, bf(rwkv_w_out[j]), ln_g[i, 0], ln_b[i, 0])
        else:
            tail = None
            q, ckv, qi, ki, wi = _dsa_proj(x, mod, dsa_w_in[j], dsa_kv_norm[j])
            maskt = _dsa_index(qi, ki, wi, k_sel)
            x = _dsa_attn(q, ckv, maskt, dsa_w_uk[j], dsa_w_uv[j], band,
                          x, mod, bf(dsa_w_out[j]), ln_g[i, 0], ln_b[i, 0])
        x = _ffn(x, mod, bf(ffn_w_in[i]), bf(ffn_w_out[i]), ln_g[i, 1], ln_b[i, 1], tail)
    return x
```

```python
import functools
import math

import numpy as np
import jax
import jax.numpy as jnp
from jax import lax
from jax.experimental import pallas as pl
from jax.experimental.pallas import tpu as pltpu

F32 = jnp.float32
BF16 = jnp.bfloat16

D_MODEL = 1024
DEPTH = 4
RWKV_HEAD = 64
RWKV_HEADS = D_MODEL // RWKV_HEAD
GN_EPS = RWKV_HEAD * 1e-5
ATT_HEADS = 16
ATT_HEAD_DIM = 64
KV_LATENT = 128
IDX_HEADS = 8
IDX_DIM = 64
TOPK_MAX = 256
TOPK_DIV = 4
REL_BUCKETS = 32
REL_MAX_DIST = 128
DEEPNORM_ALPHA = (2 * DEPTH) ** 0.25
LN_EPS = 1e-5

LANES = 128
CHUNK = 128
PRE_TM = 256
PROJ_TM = 512
FFN_TM = 512
FFN_TF = 1408
ATT_T = 256
MASK_NEG = -1e30
V7X_VMEM_BYTES = 64 * 1024 * 1024
VMEM_LIMIT = V7X_VMEM_BYTES - 8 * 1024 * 1024
INT_MIN = -2 ** 31


def _cparams(sem):
    return pltpu.CompilerParams(dimension_semantics=sem, vmem_limit_bytes=VMEM_LIMIT)


def _split2(x):
    hi = x.astype(BF16)
    lo = (x - hi.astype(F32)).astype(BF16)
    return hi, lo


def _split3(x):
    hi = x.astype(BF16)
    r1 = x - hi.astype(F32)
    mid = r1.astype(BF16)
    lo = (r1 - mid.astype(F32)).astype(BF16)
    return hi, mid, lo


_NN = (((1,), (0,)), ((), ()))
_NT = (((1,), (1,)), ((), ()))


def _mm(a, b, dims=_NN):
    return lax.dot_general(a, b, dims, preferred_element_type=F32)


def _dot3(a, b, dims=_NN):
    ah, al = _split2(a)
    bh, bl = _split2(b)
    return _mm(ah, bh, dims) + (_mm(ah, bl, dims) + _mm(al, bh, dims))


def _dot1(a, b, dims=_NN):
    return _mm(a.astype(BF16), b.astype(BF16), dims)


def _dot_hilo_rhs(a, b_bf16):
    h, l = _split2(a)
    return _mm(h, b_bf16) + _mm(l, b_bf16)


def _sigmoid(x):
    return 1.0 / (1.0 + jnp.exp(-x))


def _layernorm(xr, g, b):
    mu = jnp.mean(xr, axis=-1, keepdims=True)
    xc = xr - mu
    var = jnp.mean(xc * xc, axis=-1, keepdims=True)
    return xc * lax.rsqrt(var + LN_EPS) * g + b


def _adaln_body(c_ref, w_ref, b_ref, o_ref):
    c = c_ref[...]
    cond = c * _sigmoid(c)
    o_ref[0] = _dot3(cond, w_ref[0]) + b_ref[0]


def _adaln(c, ada_w, ada_b):
    depth, d, n = ada_w.shape
    bsz = c.shape[0]
    tn = n // 4
    return pl.pallas_call(
        _adaln_body,
        grid=(depth, n // tn),
        in_specs=[
            pl.BlockSpec((bsz, d), lambda i, j: (0, 0)),
            pl.BlockSpec((1, d, tn), lambda i, j: (i, 0, j)),
            pl.BlockSpec((1, 1, tn), lambda i, j: (i, 0, j)),
        ],
        out_specs=pl.BlockSpec((1, bsz, tn), lambda i, j: (i, 0, j)),
        out_shape=jax.ShapeDtypeStruct((depth, bsz, n), F32),
        compiler_params=_cparams(("parallel", "parallel")),
        name="adaln",
    )(c, ada_w, ada_b.reshape(depth, 1, n))


def _ffn_body(mixer_tail, *refs):
    if mixer_tail:
        (x_ref, mod_ref, wg_ref, wu_ref, wo_ref, lng_ref, lnb_ref,
         z_ref, g_ref, wm_ref, lng0_ref, lnb0_ref, o_ref, hin_ref, acc_ref, xm_ref) = refs
    else:
        x_ref, mod_ref, wg_ref, wu_ref, wo_ref, lng_ref, lnb_ref, o_ref, hin_ref, acc_ref = refs
    j = pl.program_id(2)

    @pl.when(j == 0)
    def _():
        x = x_ref[0]
        if mixer_tail:
            y = _mm((z_ref[0] * g_ref[0]).astype(BF16), wm_ref[...])
            x = _layernorm(DEEPNORM_ALPHA * x + (1.0 + mod_ref[0, 2:3, :]) * y, lng0_ref[...], lnb0_ref[...])
            xm_ref[...] = x
        hin_ref[...] = (x * (1.0 + mod_ref[0, 4:5, :]) + mod_ref[0, 3:4, :]).astype(BF16)
        acc_ref[...] = jnp.zeros_like(acc_ref)

    hin = hin_ref[...]
    gate = _mm(hin, wg_ref[...])
    up = _mm(hin, wu_ref[...])
    hid = (gate * _sigmoid(gate) * up).astype(BF16)
    acc_ref[...] += _mm(hid, wo_ref[...])

    @pl.when(j == pl.num_programs(2) - 1)
    def _():
        x = xm_ref[...] if mixer_tail else x_ref[0]
        res = DEEPNORM_ALPHA * x + (1.0 + mod_ref[0, 5:6, :]) * acc_ref[...]
        o_ref[0] = _layernorm(res, lng_ref[...], lnb_ref[...])


def _ffn(x, mod, w_in, w_out, ln_g, ln_b, mixer_tail=None):
    bsz, t, d = x.shape
    f = w_out.shape[0]
    nf = f // FFN_TF
    row = pl.BlockSpec((1, FFN_TM, d), lambda b, i, j: (b, i, 0))
    vec = pl.BlockSpec((1, d), lambda b, i, j: (0, 0))
    ins = [x, mod, w_in, w_in, w_out, ln_g.reshape(1, d), ln_b.reshape(1, d)]
    specs = [row,
             pl.BlockSpec((1, 6, d), lambda b, i, j: (b, 0, 0)),
             pl.BlockSpec((d, FFN_TF), lambda b, i, j: (0, j)),
             pl.BlockSpec((d, FFN_TF), lambda b, i, j: (0, nf + j)),
             pl.BlockSpec((FFN_TF, d), lambda b, i, j: (j, 0)),
             vec, vec]
    scratch = [pltpu.VMEM((FFN_TM, d), BF16), pltpu.VMEM((FFN_TM, d), F32)]
    if mixer_tail is not None:
        z, g, w_mix, ln_g0, ln_b0 = mixer_tail
        ins += [z, g, w_mix, ln_g0.reshape(1, d), ln_b0.reshape(1, d)]
        specs += [row, row, pl.BlockSpec((d, d), lambda b, i, j: (0, 0)), vec, vec]
        scratch += [pltpu.VMEM((FFN_TM, d), F32)]
    return pl.pallas_call(
        functools.partial(_ffn_body, mixer_tail is not None),
        grid=(bsz, t // FFN_TM, nf),
        in_specs=specs,
        out_specs=row,
        out_shape=jax.ShapeDtypeStruct((bsz, t, d), F32),
        scratch_shapes=scratch,
        compiler_params=_cparams(("parallel", "parallel", "arbitrary")),
        name="ffn",
    )(*ins)


def _rwkv_pre_body(has_vres, *refs):
    if has_vres:
        (x_ref, xp_ref, mod_ref, mu_ref, wr_ref, wk_ref, wv_ref, w1_ref, w2_ref, a1_ref, a2_ref,
         g1_ref, g2_ref, vec_ref, seg_ref, segt_ref, tri_ref, vf_ref, v1_ref, v2_ref,
         r_o, k_o, a_o, b_o, v_o, g_o, gl_o, mix_ref) = refs
    else:
        (x_ref, xp_ref, mod_ref, mu_ref, wr_ref, wk_ref, wv_ref, w1_ref, w2_ref, a1_ref, a2_ref,
         g1_ref, g2_ref, vec_ref, seg_ref, segt_ref, tri_ref,
         r_o, k_o, a_o, b_o, v_o, g_o, gl_o, mix_ref) = refs
    i = pl.program_id(1)
    sc = 1.0 + mod_ref[0, 1:2, :]
    sh = mod_ref[0, 0:1, :]
    hin = x_ref[0] * sc + sh
    tm, d = hin.shape
    prev_row = xp_ref[0, 7:8, :] * sc + sh
    prev_row = jnp.where(i == 0, 0.0, prev_row)
    rows = lax.broadcasted_iota(jnp.int32, hin.shape, 0)
    hprev = jnp.where(rows == 0, prev_row, pltpu.roll(hin, 1, 0))
    xx = hprev - hin
    for p in range(6):
        mix_ref[p] = (hin + xx * mu_ref[p:p + 1, :]).astype(BF16)
    lora_w = jnp.tanh(_mm(mix_ref[3], w1_ref[...])).astype(BF16)
    lora_a = _mm(mix_ref[4], a1_ref[...]).astype(BF16)
    lora_g = _sigmoid(_mm(mix_ref[5], g1_ref[...])).astype(BF16)
    if has_vres:
        lora_v = _mm(mix_ref[2], v1_ref[...]).astype(BF16)

    cw = 2 * LANES
    for cb in range(d // cw):
        cols = slice(cb * cw, (cb + 1) * cw)
        w0, a0, kkw, kaw = vec_ref[0:1, cols], vec_ref[1:2, cols], vec_ref[2:3, cols], vec_ref[3:4, cols]
        r = _mm(mix_ref[0], wr_ref[:, cols])
        k = _mm(mix_ref[1], wk_ref[:, cols])
        v = _mm(mix_ref[2], wv_ref[:, cols])

        wl = w0 + _mm(lora_w, w2_ref[:, cols])
        nz = -wl
        softplus = jnp.maximum(nz, 0.0) + jnp.log(1.0 + jnp.exp(-jnp.abs(nz)))
        logdec = -jnp.exp(-softplus - 0.5)

        a = _sigmoid(a0 + _mm(lora_a, a2_ref[:, cols]))
        if has_vres:
            vmix = _sigmoid(vec_ref[4:5, cols] + _mm(lora_v, v2_ref[:, cols]))
            v = v + (vf_ref[0, :, cols] - v) * vmix
        g = _mm(lora_g, g2_ref[:, cols])

        kk = k * kkw
        ss = _dot_hilo_rhs(kk * kk, seg_ref[...])
        nrm = jnp.maximum(jnp.sqrt(ss), 1e-12)
        kk = kk * _dot_hilo_rhs(1.0 / nrm, segt_ref[...])
        k = k * (1.0 + (a - 1.0) * kaw)

        cum = _dot_exact_rhs_lhs(tri_ref[...], logdec)
        ginc = jnp.exp(cum)
        ginv = jnp.exp(-cum)
        gprev = jnp.exp(cum - logdec)

        r_o[0, :, cols] = r * ginc
        k_o[0, :, cols] = k * ginv
        a_o[0, :, cols] = -kk * gprev
        b_o[0, :, cols] = kk * a * ginv
        v_o[0, :, cols] = v
        g_o[0, :, cols] = g
        for cc in range(tm // CHUNK):
            first, last = cc * CHUNK, cc * CHUNK + CHUNK - 1
            gl_o[0, cc, 0:1, cols] = jnp.exp(logdec[first:first + 1, :] - cum[first:first + 1, :])
            gl_o[0, cc, 1:2, cols] = ginc[last:last + 1, :]


def _dot_exact_rhs_lhs(m_bf16, x):
    h, mid, l = _split3(x)
    return _mm(m_bf16, h) + (_mm(m_bf16, mid) + _mm(m_bf16, l))


def _rwkv_pre(x, mod, p, v_first):
    bsz, t, d = x.shape
    tm = PRE_TM
    has_vres = v_first is not None
    row = pl.BlockSpec((1, tm, d), lambda b, i: (b, i, 0))

    def full(shape):
        return pl.BlockSpec(shape, lambda b, i: (0,) * len(shape))

    cw = 2 * LANES
    seg = np.zeros((cw, LANES), np.float32)
    seg[np.arange(cw), np.arange(cw) // RWKV_HEAD] = 1.0
    idx = np.arange(tm)
    same = idx[:, None] // CHUNK == idx[None, :] // CHUNK
    col = idx[None, :] % CHUNK
    tri = (same & (col <= idx[:, None] % CHUNK)).astype(np.float32) \
        - (same & (col <= CHUNK // 2 - 1)).astype(np.float32)
    vec_rows = [p['w0'], p['a0'], p['k_k'], p['k_a']] + ([p['v0']] if has_vres else [])
    vec = jnp.stack(vec_rows + [jnp.zeros_like(p['w0'])] * (8 - len(vec_rows)))
    dl, da, dg = p['w1'].shape[1], p['a1'].shape[1], p['g1'].shape[1]
    ins = [x, x, mod, p['mu'], p['wr'], p['wk'], p['wv'], p['w1'], p['w2'], p['a1'], p['a2'],
           p['g1'], p['g2'], vec, jnp.asarray(seg, BF16), jnp.asarray(seg.T, BF16),
           jnp.asarray(tri, BF16)]
    specs = [row,
             pl.BlockSpec((1, 8, d), lambda b, i: (b, jnp.maximum(i * (tm // 8) - 1, 0), 0)),
             pl.BlockSpec((1, 6, d), lambda b, i: (b, 0, 0)),
             full((6, d)), full((d, d)), full((d, d)), full((d, d)),
             full((d, dl)), full((dl, d)), full((d, da)), full((da, d)),
             full((d, dg)), full((dg, d)), full((8, d)),
             full((cw, LANES)), full((LANES, cw)), full((tm, tm))]
    if has_vres:
        dv = p['v1'].shape[1]
        ins += [v_first, p['v1'], p['v2']]
        specs += [row, full((d, dv)), full((dv, d))]
    act = jax.ShapeDtypeStruct((bsz, t, d), F32)
    nch = t // CHUNK
    outs = pl.pallas_call(
        functools.partial(_rwkv_pre_body, has_vres),
        grid=(bsz, t // tm),
        in_specs=specs,
        out_specs=[row] * 6 + [pl.BlockSpec((1, tm // CHUNK, 2, d), lambda b, i: (b, i, 0, 0))],
        out_shape=[act] * 6 + [jax.ShapeDtypeStruct((bsz, nch, 2, d), F32)],
        scratch_shapes=[pltpu.VMEM((6, tm, d), BF16)],
        compiler_params=_cparams(("parallel", "parallel")),
        name="rwkv_pre",
    )(*ins)
    return outs


_BNN = (((2,), (1,)), ((0,), (0,)))
_BNT = (((2,), (2,)), ((0,), (0,)))


def _rwkv_scan_body(r_ref, k_ref, a_ref, b_ref, v_ref, gl_ref, vec_ref, o_ref, s_ref):
    c = pl.program_id(1)

    @pl.when(c == 0)
    def _():
        s_ref[...] = jnp.zeros_like(s_ref)

    L = CHUNK
    NP = RWKV_HEADS // 2
    shp = (NP, L, LANES)
    plane = lax.broadcasted_iota(jnp.int32, (1, L, 2 * L), 2)
    ph0 = plane < L
    tt = lax.broadcasted_iota(jnp.int32, (1, L, 2 * L), 1)
    ss = plane & (L - 1)
    strict = ss < tt
    incl = ss <= tt
    eye = jnp.where(ss == tt, 1.0, 0.0)
    base_bits = 4

    def same_block(bits):
        return (ss >> bits) == (tt >> bits)

    lvl0 = strict & same_block(base_bits)
    merges = [strict & same_block(bits + 1) & ((ss >> bits) != (tt >> bits))
              for bits in range(base_bits, (L - 1).bit_length())]
    dh0 = lax.broadcasted_iota(jnp.int32, (1, L, LANES), 2) < RWKV_HEAD
    ri = lax.broadcasted_iota(jnp.int32, (1, LANES, LANES), 1)
    ci = lax.broadcasted_iota(jnp.int32, (1, LANES, LANES), 2)
    blockdiag = (ri >> 6) == (ci >> 6)
    seg_mean = jnp.where(blockdiag[0], 1.0, 0.0).astype(BF16)

    def grp(ref):
        return jnp.stack([ref[0, :, hp * LANES:(hp + 1) * LANES] for hp in range(NP)])

    def bd(x):
        return jnp.concatenate([jnp.where(dh0, x, 0.0), jnp.where(dh0, 0.0, x)], axis=1)

    def bdp(y):
        return jnp.concatenate([jnp.where(ph0, y, 0.0), jnp.where(ph0, 0.0, y)], axis=1)

    def pmm(xp, y):
        return _dot1(xp, bd(y), _BNN)

    def ppm(xp, yp):
        return _dot1(xp, bdp(yp), _BNN)

    R, K, A, Bv, V = grp(r_ref), grp(k_ref), grp(a_ref), grp(b_ref), grp(v_ref)
    gvec = lambda n: jnp.stack([gl_ref[0, 0, n:n + 1, hp * LANES:(hp + 1) * LANES] for hp in range(NP)])
    S = s_ref[...] * gvec(0)

    AR = jnp.concatenate([A, R], axis=1)
    BK = jnp.concatenate([bd(Bv), bd(K)], axis=1)
    G = _dot1(AR, BK, _BNT)
    A_ab = jnp.where(strict, G[:, 0:L, 0:2 * L], 0.0)
    A_ak = jnp.where(strict, G[:, 0:L, 2 * L:4 * L], 0.0)
    A_rb = jnp.where(incl, G[:, L:2 * L, 0:2 * L], 0.0)
    A_rk = jnp.where(incl, G[:, L:2 * L, 2 * L:4 * L], 0.0)

    a0 = jnp.where(lvl0, A_ab, 0.0)
    Tm = eye + a0
    P = a0
    for _ in range(base_bits - 1):
        P = ppm(P, P)
        Tm = Tm + ppm(Tm, P)
    for lvl in merges:
        Tm = Tm + ppm(ppm(Tm, jnp.where(lvl, A_ab, 0.0)), Tm)

    PQ = _dot1(AR, S, _BNT)
    W = PQ[:, 0:L] + pmm(A_ak, V)
    U = pmm(Tm, W)
    Y = PQ[:, L:2 * L] + _dot1(jnp.concatenate([A_rb, A_rk], axis=2),
                               jnp.concatenate([bd(U), bd(V)], axis=1), _BNN)
    UV = jnp.concatenate([U, V], axis=1)
    UVt = jnp.stack([UV[hp].T for hp in range(NP)])
    BK2 = jnp.concatenate([Bv, K], axis=1)
    upd = _dot1(UVt, BK2, _BNN)
    s_ref[...] = (S + jnp.where(blockdiag, upd, 0.0)) * gvec(1)

    vrow = lambda n: jnp.stack([vec_ref[n:n + 1, hp * LANES:(hp + 1) * LANES] for hp in range(NP)])
    inv_n = 1.0 / RWKV_HEAD
    flat = lambda x: x.reshape(NP * L, LANES)
    mean = _dot_hilo_rhs(flat(Y), seg_mean).reshape(shp) * inv_n
    yc = Y - mean
    var = _dot_hilo_rhs(flat(yc * yc), seg_mean).reshape(shp) * inv_n
    yn = yc * lax.rsqrt(var + GN_EPS) * vrow(1) + vrow(2)
    bonus = _dot_hilo_rhs(flat(R * K * vrow(0)), seg_mean).reshape(shp)
    out = yn + bonus * V
    for hp in range(NP):
        o_ref[0, :, hp * LANES:(hp + 1) * LANES] = out[hp]


def _rwkv_scan(r, k, a, b, v, gl, r_k, lnx_g, lnx_b):
    bsz, t, d = r.shape
    row = pl.BlockSpec((1, CHUNK, d), lambda bb, c: (bb, c, 0))
    vec = jnp.stack([r_k.reshape(d), lnx_g, lnx_b] + [jnp.zeros((d,), F32)] * 5)
    return pl.pallas_call(
        _rwkv_scan_body,
        grid=(bsz, t // CHUNK),
        in_specs=[row] * 5 + [
            pl.BlockSpec((1, 1, 2, d), lambda bb, c: (bb, c, 0, 0)),
            pl.BlockSpec((8, d), lambda bb, c: (0, 0)),
        ],
        out_specs=row,
        out_shape=jax.ShapeDtypeStruct((bsz, t, d), F32),
        scratch_shapes=[pltpu.VMEM((RWKV_HEADS // 2, LANES, LANES), F32)],
        compiler_params=_cparams(("parallel", "arbitrary")),
        name="rwkv_scan",
    )(r, k, a, b, v, gl, vec)


IDX_COLS = 768


def _dsa_proj_body(x_ref, mod_ref, wq_ref, wc_ref, wi_ref, kvn_ref,
                   q_o, ckv_o, qi_o, ki_o, wi_o):
    hin = (x_ref[0] * (1.0 + mod_ref[0, 1:2, :]) + mod_ref[0, 0:1, :]).astype(BF16)
    q_o[0] = _mm(hin, wq_ref[...]).astype(BF16)
    ckv = _mm(hin, wc_ref[...])
    ms = jnp.mean(ckv * ckv, axis=-1, keepdims=True)
    ckv_o[0] = (ckv * lax.rsqrt(ms + 1e-6) * kvn_ref[...]).astype(BF16)
    idx = _mm(hin, wi_ref[...])
    nq = IDX_HEADS * IDX_DIM
    qi_o[0] = idx[:, 0:nq].astype(BF16)
    ki_o[0] = idx[:, nq:nq + LANES].astype(BF16)
    wi_o[0] = idx[:, nq + LANES:nq + 2 * LANES] * (IDX_HEADS ** -0.5 * IDX_DIM ** -0.5)


def _dsa_proj(x, mod, w_in, kv_norm):
    bsz, t, d = x.shape
    c1 = ATT_HEADS * ATT_HEAD_DIM
    c2 = c1 + KV_LATENT
    c3 = c2 + IDX_HEADS * IDX_DIM
    c4 = c3 + IDX_DIM
    wq = w_in[:, :c1].astype(BF16)
    wc = w_in[:, c1:c2].astype(BF16)
    pad = IDX_COLS - (c3 - c2) - 2 * IDX_DIM - IDX_HEADS
    widx = jnp.concatenate([w_in[:, c2:c3], w_in[:, c3:c4], w_in[:, c3:c4], w_in[:, c4:],
                            jnp.zeros((d, pad), F32)], axis=1).astype(BF16)
    tm = PROJ_TM

    def full(shape):
        return pl.BlockSpec(shape, lambda b, i: (0,) * len(shape))

    def row(n):
        return pl.BlockSpec((1, tm, n), lambda b, i: (b, i, 0))

    def act(n, dtype):
        return jax.ShapeDtypeStruct((bsz, t, n), dtype)

    nq = IDX_HEADS * IDX_DIM
    return pl.pallas_call(
        _dsa_proj_body,
        grid=(bsz, t // tm),
        in_specs=[row(d), pl.BlockSpec((1, 6, d), lambda b, i: (b, 0, 0)),
                  full((d, c1)), full((d, KV_LATENT)), full((d, IDX_COLS)),
                  full((1, KV_LATENT))],
        out_specs=[row(c1), row(KV_LATENT), row(nq), row(LANES), row(LANES)],
        out_shape=[act(c1, BF16), act(KV_LATENT, BF16), act(nq, BF16), act(LANES, BF16), act(LANES, F32)],
        compiler_params=_cparams(("parallel", "parallel")),
        name="dsa_proj",
    )(x, mod, wq, wc, widx, kv_norm.reshape(1, KV_LATENT))


def _dsa_index_body(k_sel, qi_ref, wi_ref, ki_ref, o_ref, key_ref):
    i = pl.program_id(1)
    tq = qi_ref.shape[1]
    t = ki_ref.shape[1]
    nchunk = t // tq
    wt = wi_ref[0].T
    lane = lax.broadcasted_iota(jnp.int32, (tq, LANES), 1)
    first = lane < IDX_DIM
    qheads = []
    for hp in range(IDX_HEADS // 2):
        qp = qi_ref[0, :, hp * LANES:(hp + 1) * LANES].astype(F32)
        qheads.append(jnp.where(first, qp, 0.0).astype(BF16))
        qheads.append(jnp.where(first, 0.0, qp).astype(BF16))
    q_all = jnp.concatenate(qheads, axis=0)
    krow = lax.broadcasted_iota(jnp.int32, (tq, tq), 0)
    qlane = lax.broadcasted_iota(jnp.int32, (tq, tq), 1)

    kf = float(k_sel)
    nbits = int(t - 1).bit_length()

    def colsum(ind):
        return jnp.sum(jnp.sum(ind.reshape(tq // 32, 4, 8, tq), axis=0), axis=0)

    def tile(nc):
        chunks = [slice(c * tq, (c + 1) * tq) for c in range(nc)]
        diag = krow <= qlane

        for c, rows in enumerate(chunks):
            kk2 = ki_ref[0, rows, :]
            s_all = _mm(kk2, q_all, _NT)
            score = jnp.zeros((tq, tq), F32)
            for h in range(IDX_HEADS):
                score = score + wt[h:h + 1, :] * jnp.maximum(s_all[:, h * tq:(h + 1) * tq], 0.0)
            score = jnp.where(score == 0.0, 0.0, score)
            bits = pltpu.bitcast(score, jnp.int32)
            skey = bits ^ ((bits >> 31) & 0x7FFFFFFF)
            key_ref[rows, :] = jnp.where(diag, skey, INT_MIN) if c == nc - 1 else skey

        def count(fn):
            acc = jnp.zeros((8, tq), F32)
            for c, rows in enumerate(chunks):
                acc = acc + colsum(fn(key_ref[rows, :], c * tq + krow))
            return jnp.sum(acc, axis=0, keepdims=True)

        def count_ge(cand):
            return count(lambda keys, kpos: jnp.where(keys >= cand, 1.0, 0.0))

        thr0 = jnp.where(count_ge(jnp.zeros((1, tq), jnp.int32)) >= kf, 0, INT_MIN).astype(jnp.int32)

        def thr_step(n, thr):
            cand = thr | jnp.left_shift(jnp.int32(1), 30 - n)
            return jnp.where(count_ge(cand) >= kf, cand, thr)

        thr = lax.fori_loop(0, 31, thr_step, thr0)
        n_gt = count(lambda keys, kpos: jnp.where(keys > thr, 1.0, 0.0))
        n_eq = count(lambda keys, kpos: jnp.where(keys == thr, 1.0, 0.0))
        need = kf - n_gt

        def tie_cut():
            def cut_step(n, cut):
                cand = cut | jnp.left_shift(jnp.int32(1), nbits - 1 - n)
                cnt = count(lambda keys, kpos: jnp.where(keys == thr, jnp.where(kpos < cand, 1.0, 0.0), 0.0))
                return jnp.where(cnt < need, cand, cut)
            return lax.fori_loop(0, nbits, cut_step, jnp.zeros((1, tq), jnp.int32))

        cut = lax.cond(jnp.max(n_eq - need) > 0.0, tie_cut, lambda: jnp.full((1, tq), t, jnp.int32))

        for c, rows in enumerate(chunks):
            keys = key_ref[rows, :]
            tie = jnp.where(keys == thr, jnp.where(c * tq + krow <= cut, 0.0, MASK_NEG), MASK_NEG)
            bias = jnp.where(keys > thr, 0.0, tie)
            if c == nc - 1:
                bias = jnp.where(diag, bias, MASK_NEG)
            o_ref[0, rows, :] = bias.astype(BF16)
        for c in range(nc, nchunk):
            o_ref[0, c * tq:(c + 1) * tq, :] = jnp.full((tq, tq), MASK_NEG, BF16)

    for nc in range(1, nchunk + 1):
        pl.when(i == nc - 1)(functools.partial(tile, nc))


def _dsa_index(qi, ki, wi, k_sel):
    bsz, t, nq = qi.shape
    tq = ATT_T
    return pl.pallas_call(
        functools.partial(_dsa_index_body, k_sel),
        grid=(bsz, t // tq),
        in_specs=[pl.BlockSpec((1, tq, nq), lambda b, i: (b, i, 0)),
                  pl.BlockSpec((1, tq, LANES), lambda b, i: (b, i, 0)),
                  pl.BlockSpec((1, t, LANES), lambda b, i: (b, 0, 0))],
        out_specs=pl.BlockSpec((1, t, tq), lambda b, i: (b, 0, i)),
        out_shape=jax.ShapeDtypeStruct((bsz, t, t), BF16),
        scratch_shapes=[pltpu.VMEM((t, tq), jnp.int32)],
        compiler_params=_cparams(("parallel", "parallel")),
        name="dsa_index",
    )(qi, wi, ki)


def _t5_bucket_np(n):
    n = np.maximum(n, 0)
    max_exact = REL_BUCKETS // 2
    nf = np.maximum(n, 1).astype(np.float32)
    large = max_exact + (np.log(nf / np.float32(max_exact)) / np.float32(math.log(REL_MAX_DIST / max_exact))
                         * np.float32(REL_BUCKETS - max_exact)).astype(np.int32)
    large = np.minimum(large, REL_BUCKETS - 1)
    return np.where(n < max_exact, n, large).astype(np.int32)


def _band_body(bkt_ref, rb_ref, o_ref):
    h = pl.program_id(1)
    bkt = bkt_ref[0]
    acc = jnp.zeros(bkt.shape, F32)
    for b in range(REL_BUCKETS):
        acc = jnp.where(bkt == b, rb_ref[b, h], acc)
    o_ref[0] = acc * LOG2E


def _band_bias(rel_bias):
    tt = ATT_T
    kc = np.arange(tt)[:, None]
    qr = np.arange(tt)[None, :]
    planes = [_t5_bucket_np(d * tt + qr - kc) for d in range(3)]
    assert (planes[2] == REL_BUCKETS - 1).all() and tt + 1 >= 113
    bkt = jnp.asarray(np.stack(planes))
    return pl.pallas_call(
        _band_body,
        grid=(3, ATT_HEADS),
        in_specs=[pl.BlockSpec((1, tt, tt), lambda d, h: (d, 0, 0)),
                  pl.BlockSpec(memory_space=pltpu.SMEM)],
        out_specs=pl.BlockSpec((1, tt, tt), lambda d, h: (d, 0, h)),
        out_shape=jax.ShapeDtypeStruct((3, tt, ATT_HEADS * tt), F32),
        compiler_params=_cparams(("parallel", "parallel")),
        name="band_bias",
    )(bkt, rel_bias)


LOG2E = 1.4426950408889634
ACC_ROWS = KV_LATENT + 16


def _dsa_attn_body(qt_ref, kt_ref, q_ref, ckv_ref, mask_ref, wuk_ref, wuv_ref, near_ref, far_ref,
                   x_ref, mod_ref, wout_ref, lng_ref, lnb_ref, o_ref,
                   ql_ref, m_ref, sm_ref, ot_ref, *acc_refs):
    i = qt_ref[pl.program_id(1)]
    kp = kt_ref[pl.program_id(1)]
    gap = i - 2 * kp
    nh = ATT_HEADS
    tq = q_ref.shape[1]
    tk = tq
    qscale = ATT_HEAD_DIM ** -0.5 * LOG2E

    @pl.when(kp == 0)
    def _():
        for hp in range(nh // 2):
            qp = q_ref[0, :, hp * LANES:(hp + 1) * LANES].astype(BF16)
            qlat = _mm(qp, wuk_ref[hp]) * qscale
            ql_ref[2 * hp * tq:(2 * hp + 1) * tq, :] = qlat[:, 0:KV_LATENT].astype(BF16)
            ql_ref[(2 * hp + 1) * tq:(2 * hp + 2) * tq, :] = qlat[:, KV_LATENT:2 * KV_LATENT].astype(BF16)
        m_ref[...] = jnp.full(m_ref.shape, MASK_NEG, F32)
        for acc_ref in acc_refs:
            acc_ref[...] = jnp.zeros_like(acc_ref)

    def step(*kinds):
        nk = len(kinds) * tk
        ckv = ckv_ref[0, 0:nk, :]
        ckv_aug = jnp.concatenate([ckv.astype(F32).T, jnp.ones((ACC_ROWS - KV_LATENT, nk), F32)],
                                  axis=0).astype(BF16)
        maskb = mask_ref[0, 0:nk, :].astype(F32)
        m_prev = m_ref[...]
        m_news = []
        for h in range(nh):
            hs = slice(h * tq, (h + 1) * tq)
            s = _mm(ckv, ql_ref[hs, :], _NT)
            m_new = m_prev[:, hs]
            for n, plane in enumerate(kinds):
                rows = slice(n * tk, (n + 1) * tk)
                sm = s[rows] + maskb[rows]
                if plane is None:
                    m_new = jnp.maximum(m_new, jnp.max(sm, axis=0, keepdims=True) + far_ref[:, hs])
                else:
                    sm = sm + near_ref[plane, :, hs]
                    m_new = jnp.maximum(m_new, jnp.max(sm, axis=0, keepdims=True))
                sm_ref[rows, hs] = sm
            m_news.append(m_new)
        for h in range(nh):
            hs = slice(h * tq, (h + 1) * tq)
            m_new = m_news[h]
            p = jnp.concatenate(
                [jnp.exp2(sm_ref[n * tk:(n + 1) * tk, hs]
                          - (m_new - far_ref[:, hs] if plane is None else m_new)).astype(BF16)
                 for n, plane in enumerate(kinds)], axis=0)
            alpha = jnp.exp2(m_prev[:, hs] - m_new)
            acc_refs[h][...] = alpha * acc_refs[h][...] + _mm(ckv_aug, p)
        m_ref[...] = jnp.concatenate(m_news, axis=1)

    @pl.when(gap >= 3)
    def _():
        step(None, None)

    @pl.when(gap == 2)
    def _():
        step(None, 1)

    @pl.when(gap == 1)
    def _():
        step(1, 0)

    @pl.when(gap == 0)
    def _():
        step(0)

    @pl.when(gap <= 1)
    def _():
        def norm(h):
            a = acc_refs[h][...]
            return a[0:KV_LATENT] * (1.0 / a[KV_LATENT:KV_LATENT + 1])
        for hp in range(nh // 2):
            olat = jnp.concatenate([norm(2 * hp), norm(2 * hp + 1)], axis=0).astype(BF16)
            ot_ref[hp * LANES:(hp + 1) * LANES, :] = _mm(wuv_ref[hp], olat)
        y = _mm(ot_ref[...].T.astype(BF16), wout_ref[...])
        res = DEEPNORM_ALPHA * x_ref[0] + (1.0 + mod_ref[0, 2:3, :]) * y
        o_ref[0] = _layernorm(res, lng_ref[...], lnb_ref[...])


def _dsa_attn(q, ckv, maskt, w_uk, w_uv, band, x, mod, w_out, ln_g, ln_b):
    bsz, t, d = q.shape
    tt = ATT_T
    nt = t // tt
    nh = ATT_HEADS
    zk = jnp.zeros((nh // 2, ATT_HEAD_DIM, KV_LATENT), F32)
    wuk2 = jnp.concatenate([jnp.concatenate([w_uk[0::2], zk], axis=2),
                            jnp.concatenate([zk, w_uk[1::2]], axis=2)], axis=1).astype(BF16)
    wuv_t = jnp.swapaxes(w_uv, 1, 2)
    zv = jnp.zeros((nh // 2, ATT_HEAD_DIM, KV_LATENT), F32)
    wuv2 = jnp.concatenate([jnp.concatenate([wuv_t[0::2], zv], axis=2),
                            jnp.concatenate([zv, wuv_t[1::2]], axis=2)], axis=1).astype(BF16)
    assert nt % 2 == 0
    steps = [(i, kp) for i in range(nt) for kp in range(i // 2 + 1)]
    q_tab = jnp.asarray([p[0] for p in steps], jnp.int32)
    k_tab = jnp.asarray([p[1] for p in steps], jnp.int32)
    grid_spec = pltpu.PrefetchScalarGridSpec(
        num_scalar_prefetch=2,
        grid=(bsz, len(steps)),
        in_specs=[pl.BlockSpec((1, tt, d), lambda b, s, qt, kt: (b, qt[s], 0)),
                  pl.BlockSpec((1, 2 * tt, KV_LATENT), lambda b, s, qt, kt: (b, kt[s], 0)),
                  pl.BlockSpec((1, 2 * tt, tt), lambda b, s, qt, kt: (b, kt[s], qt[s])),
                  pl.BlockSpec((nh // 2, LANES, 2 * KV_LATENT), lambda b, s, qt, kt: (0, 0, 0)),
                  pl.BlockSpec((nh // 2, LANES, 2 * KV_LATENT), lambda b, s, qt, kt: (0, 0, 0)),
                  pl.BlockSpec((2, tt, nh * tt), lambda b, s, qt, kt: (0, 0, 0)),
                  pl.BlockSpec((1, nh * tt), lambda b, s, qt, kt: (0, 0)),
                  pl.BlockSpec((1, tt, d), lambda b, s, qt, kt: (b, qt[s], 0)),
                  pl.BlockSpec((1, 6, d), lambda b, s, qt, kt: (b, 0, 0)),
                  pl.BlockSpec((d, d), lambda b, s, qt, kt: (0, 0)),
                  pl.BlockSpec((1, d), lambda b, s, qt, kt: (0, 0)),
                  pl.BlockSpec((1, d), lambda b, s, qt, kt: (0, 0))],
        out_specs=pl.BlockSpec((1, tt, d), lambda b, s, qt, kt: (b, qt[s], 0)),
        scratch_shapes=[pltpu.VMEM((nh * tt, KV_LATENT), BF16),
                        pltpu.VMEM((1, nh * tt), F32),
                        pltpu.VMEM((2 * tt, nh * tt), F32),
                        pltpu.VMEM((d, tt), F32)]
        + [pltpu.VMEM((ACC_ROWS, tt), F32)] * nh)
    return pl.pallas_call(
        _dsa_attn_body,
        grid_spec=grid_spec,
        out_shape=jax.ShapeDtypeStruct((bsz, t, d), F32),
        compiler_params=_cparams(("parallel", "arbitrary")),
        name="dsa_attn",
    )(q_tab, k_tab, q, ckv, maskt, wuk2, wuv2, band[:2], band[2, 0:1, :],
      x, mod, w_out, ln_g.reshape(1, d), ln_b.reshape(1, d))


def kernel(x, c, ada_w, ada_b, ln_g, ln_b, ffn_w_in, ffn_w_out, rwkv_mu, rwkv_w_rkv, rwkv_w0, rwkv_w1, rwkv_w2, rwkv_a0, rwkv_a1, rwkv_a2, rwkv_v0, rwkv_v1, rwkv_v2, rwkv_g1, rwkv_g2, rwkv_k_k, rwkv_k_a, rwkv_r_k, rwkv_lnx_g, rwkv_lnx_b, rwkv_w_out, dsa_w_in, dsa_kv_norm, dsa_w_uk, dsa_w_uv, dsa_w_out, rel_bias):
    bsz, t, d = x.shape
    assert d == D_MODEL and t % ATT_T == 0 and t % FFN_TM == 0 and t % PRE_TM == 0
    mod_all = _adaln(c, ada_w, ada_b).reshape(DEPTH, bsz, 6, d)
    band = _band_bias(rel_bias)
    k_sel = min(TOPK_MAX, t // TOPK_DIV)
    bf = lambda w: w.astype(BF16)
    v_first = None
    for i in range(DEPTH):
        mod = mod_all[i]
        j = i // 2
        if i % 2 == 0:
            p = dict(mu=rwkv_mu[j], wr=bf(rwkv_w_rkv[j, 0]), wk=bf(rwkv_w_rkv[j, 1]), wv=bf(rwkv_w_rkv[j, 2]),
                     w0=rwkv_w0[j], w1=bf(rwkv_w1[j]), w2=bf(rwkv_w2[j]),
                     a0=rwkv_a0[j], a1=bf(rwkv_a1[j]), a2=bf(rwkv_a2[j]),
                     g1=bf(rwkv_g1[j]), g2=bf(rwkv_g2[j]), k_k=rwkv_k_k[j], k_a=rwkv_k_a[j])
            if j > 0:
                p.update(v0=rwkv_v0[j - 1], v1=bf(rwkv_v1[j - 1]), v2=bf(rwkv_v2[j - 1]))
            r_s, k_s, a_s, b_s, v, g, gl = _rwkv_pre(x, mod, p, v_first if j > 0 else None)
            if j == 0:
                v_first = v
            z = _rwkv_scan(r_s, k_s, a_s, b_s, v, gl, rwkv_r_k[j], rwkv_lnx_g[j], rwkv_lnx_b[j])
            tail = (z, g, bf(rwkv_w_out[j]), ln_g[i, 0], ln_b[i, 0])
        else:
            tail = None
            q, ckv, qi, ki, wi = _dsa_proj(x, mod, dsa_w_in[j], dsa_kv_norm[j])
            maskt = _dsa_index(qi, ki, wi, k_sel)
            x = _dsa_attn(q, ckv, maskt, dsa_w_uk[j], dsa_w_uv[j], band,
                          x, mod, bf(dsa_w_out[j]), ln_g[i, 0], ln_b[i, 0])
        x = _ffn(x, mod, bf(ffn_w_in[i]), bf(ffn_w_out[i]), ln_g[i, 1], ln_b[i, 1], tail)
    return x
```

```python
import functools
import math

import numpy as np
import jax
import jax.numpy as jnp
from jax import lax
from jax.experimental import pallas as pl
from jax.experimental.pallas import tpu as pltpu

F32 = jnp.float32
BF16 = jnp.bfloat16

D_MODEL = 1024
DEPTH = 4
RWKV_HEAD = 64
RWKV_HEADS = D_MODEL // RWKV_HEAD
GN_EPS = RWKV_HEAD * 1e-5
ATT_HEADS = 16
ATT_HEAD_DIM = 64
KV_LATENT = 128
IDX_HEADS = 8
IDX_DIM = 64
TOPK_MAX = 256
TOPK_DIV = 4
REL_BUCKETS = 32
REL_MAX_DIST = 128
DEEPNORM_ALPHA = (2 * DEPTH) ** 0.25
LN_EPS = 1e-5

LANES = 128
CHUNK = 128
PRE_TM = 256
PROJ_TM = 512
FFN_TM = 512
FFN_TF = 2816
ATT_T = 256
MASK_NEG = -1e30
V7X_VMEM_BYTES = 64 * 1024 * 1024
VMEM_LIMIT = V7X_VMEM_BYTES - 8 * 1024 * 1024
INT_MIN = -2 ** 31


def _cparams(sem):
    return pltpu.CompilerParams(dimension_semantics=sem, vmem_limit_bytes=VMEM_LIMIT)


def _split2(x):
    hi = x.astype(BF16)
    lo = (x - hi.astype(F32)).astype(BF16)
    return hi, lo


def _split3(x):
    hi = x.astype(BF16)
    r1 = x - hi.astype(F32)
    mid = r1.astype(BF16)
    lo = (r1 - mid.astype(F32)).astype(BF16)
    return hi, mid, lo


_NN = (((1,), (0,)), ((), ()))
_NT = (((1,), (1,)), ((), ()))


def _mm(a, b, dims=_NN):
    return lax.dot_general(a, b, dims, preferred_element_type=F32)


def _dot3(a, b, dims=_NN):
    ah, al = _split2(a)
    bh, bl = _split2(b)
    return _mm(ah, bh, dims) + (_mm(ah, bl, dims) + _mm(al, bh, dims))


def _dot1(a, b, dims=_NN):
    return _mm(a.astype(BF16), b.astype(BF16), dims)


def _dot_hilo_rhs(a, b_bf16):
    h, l = _split2(a)
    return _mm(h, b_bf16) + _mm(l, b_bf16)


def _sigmoid(x):
    return 1.0 / (1.0 + jnp.exp(-x))


def _layernorm(xr, g, b):
    mu = jnp.mean(xr, axis=-1, keepdims=True)
    xc = xr - mu
    var = jnp.mean(xc * xc, axis=-1, keepdims=True)
    return xc * lax.rsqrt(var + LN_EPS) * g + b


def _adaln_body(c_ref, w_ref, b_ref, o_ref):
    c = c_ref[...]
    cond = c * _sigmoid(c)
    o_ref[0] = _dot3(cond, w_ref[0]) + b_ref[0]


def _adaln(c, ada_w, ada_b):
    depth, d, n = ada_w.shape
    bsz = c.shape[0]
    tn = n // 4
    return pl.pallas_call(
        _adaln_body,
        grid=(depth, n // tn),
        in_specs=[
            pl.BlockSpec((bsz, d), lambda i, j: (0, 0)),
            pl.BlockSpec((1, d, tn), lambda i, j: (i, 0, j)),
            pl.BlockSpec((1, 1, tn), lambda i, j: (i, 0, j)),
        ],
        out_specs=pl.BlockSpec((1, bsz, tn), lambda i, j: (i, 0, j)),
        out_shape=jax.ShapeDtypeStruct((depth, bsz, n), F32),
        compiler_params=_cparams(("parallel", "parallel")),
        name="adaln",
    )(c, ada_w, ada_b.reshape(depth, 1, n))


def _ffn_body(mixer_tail, *refs):
    if mixer_tail:
        (x_ref, mod_ref, wg_ref, wu_ref, wo_ref, lng_ref, lnb_ref,
         z_ref, g_ref, wm_ref, lng0_ref, lnb0_ref, o_ref, hin_ref, acc_ref, xm_ref) = refs
    else:
        x_ref, mod_ref, wg_ref, wu_ref, wo_ref, lng_ref, lnb_ref, o_ref, hin_ref, acc_ref = refs
    j = pl.program_id(2)

    @pl.when(j == 0)
    def _():
        x = x_ref[0]
        if mixer_tail:
            y = _mm((z_ref[0] * g_ref[0]).astype(BF16), wm_ref[...])
            x = _layernorm(DEEPNORM_ALPHA * x + (1.0 + mod_ref[0, 2:3, :]) * y, lng0_ref[...], lnb0_ref[...])
            xm_ref[...] = x
        hin_ref[...] = (x * (1.0 + mod_ref[0, 4:5, :]) + mod_ref[0, 3:4, :]).astype(BF16)
        acc_ref[...] = jnp.zeros_like(acc_ref)

    hin = hin_ref[...]
    gate = _mm(hin, wg_ref[...])
    up = _mm(hin, wu_ref[...])
    hid = (gate * _sigmoid(gate) * up).astype(BF16)
    acc_ref[...] += _mm(hid, wo_ref[...])

    @pl.when(j == pl.num_programs(2) - 1)
    def _():
        x = xm_ref[...] if mixer_tail else x_ref[0]
        res = DEEPNORM_ALPHA * x + (1.0 + mod_ref[0, 5:6, :]) * acc_ref[...]
        o_ref[0] = _layernorm(res, lng_ref[...], lnb_ref[...])


def _ffn(x, mod, w_in, w_out, ln_g, ln_b, mixer_tail=None):
    bsz, t, d = x.shape
    f = w_out.shape[0]
    nf = f // FFN_TF
    row = pl.BlockSpec((1, FFN_TM, d), lambda b, i, j: (b, i, 0))
    vec = pl.BlockSpec((1, d), lambda b, i, j: (0, 0))
    ins = [x, mod, w_in, w_in, w_out, ln_g.reshape(1, d), ln_b.reshape(1, d)]
    specs = [row,
             pl.BlockSpec((1, 6, d), lambda b, i, j: (b, 0, 0)),
             pl.BlockSpec((d, FFN_TF), lambda b, i, j: (0, j)),
             pl.BlockSpec((d, FFN_TF), lambda b, i, j: (0, nf + j)),
             pl.BlockSpec((FFN_TF, d), lambda b, i, j: (j, 0)),
             vec, vec]
    scratch = [pltpu.VMEM((FFN_TM, d), BF16), pltpu.VMEM((FFN_TM, d), F32)]
    if mixer_tail is not None:
        z, g, w_mix, ln_g0, ln_b0 = mixer_tail
        ins += [z, g, w_mix, ln_g0.reshape(1, d), ln_b0.reshape(1, d)]
        specs += [row, row, pl.BlockSpec((d, d), lambda b, i, j: (0, 0)), vec, vec]
        scratch += [pltpu.VMEM((FFN_TM, d), F32)]
    return pl.pallas_call(
        functools.partial(_ffn_body, mixer_tail is not None),
        grid=(bsz, t // FFN_TM, nf),
        in_specs=specs,
        out_specs=row,
        out_shape=jax.ShapeDtypeStruct((bsz, t, d), F32),
        scratch_shapes=scratch,
        compiler_params=_cparams(("parallel", "parallel", "arbitrary")),
        name="ffn",
    )(*ins)


def _rwkv_pre_body(has_vres, *refs):
    if has_vres:
        (x_ref, xp_ref, mod_ref, mu_ref, wr_ref, wk_ref, wv_ref, w1_ref, w2_ref, a1_ref, a2_ref,
         g1_ref, g2_ref, vec_ref, seg_ref, segt_ref, tri_ref, vf_ref, v1_ref, v2_ref,
         r_o, k_o, a_o, b_o, v_o, g_o, gl_o, mix_ref) = refs
    else:
        (x_ref, xp_ref, mod_ref, mu_ref, wr_ref, wk_ref, wv_ref, w1_ref, w2_ref, a1_ref, a2_ref,
         g1_ref, g2_ref, vec_ref, seg_ref, segt_ref, tri_ref,
         r_o, k_o, a_o, b_o, v_o, g_o, gl_o, mix_ref) = refs
    i = pl.program_id(1)
    sc = 1.0 + mod_ref[0, 1:2, :]
    sh = mod_ref[0, 0:1, :]
    hin = x_ref[0] * sc + sh
    tm, d = hin.shape
    prev_row = xp_ref[0, 7:8, :] * sc + sh
    prev_row = jnp.where(i == 0, 0.0, prev_row)
    rows = lax.broadcasted_iota(jnp.int32, hin.shape, 0)
    hprev = jnp.where(rows == 0, prev_row, pltpu.roll(hin, 1, 0))
    xx = hprev - hin
    for p in range(6):
        mix_ref[p] = (hin + xx * mu_ref[p:p + 1, :]).astype(BF16)
    lora_w = jnp.tanh(_mm(mix_ref[3], w1_ref[...])).astype(BF16)
    lora_a = _mm(mix_ref[4], a1_ref[...]).astype(BF16)
    lora_g = _sigmoid(_mm(mix_ref[5], g1_ref[...])).astype(BF16)
    if has_vres:
        lora_v = _mm(mix_ref[2], v1_ref[...]).astype(BF16)

    cw = 2 * LANES
    for cb in range(d // cw):
        cols = slice(cb * cw, (cb + 1) * cw)
        w0, a0, kkw, kaw = vec_ref[0:1, cols], vec_ref[1:2, cols], vec_ref[2:3, cols], vec_ref[3:4, cols]
        r = _mm(mix_ref[0], wr_ref[:, cols])
        k = _mm(mix_ref[1], wk_ref[:, cols])
        v = _mm(mix_ref[2], wv_ref[:, cols])

        wl = w0 + _mm(lora_w, w2_ref[:, cols])
        nz = -wl
        softplus = jnp.maximum(nz, 0.0) + jnp.log(1.0 + jnp.exp(-jnp.abs(nz)))
        logdec = -jnp.exp(-softplus - 0.5)

        a = _sigmoid(a0 + _mm(lora_a, a2_ref[:, cols]))
        if has_vres:
            vmix = _sigmoid(vec_ref[4:5, cols] + _mm(lora_v, v2_ref[:, cols]))
            v = v + (vf_ref[0, :, cols] - v) * vmix
        g = _mm(lora_g, g2_ref[:, cols])

        kk = k * kkw
        ss = _dot_hilo_rhs(kk * kk, seg_ref[...])
        nrm = jnp.maximum(jnp.sqrt(ss), 1e-12)
        kk = kk * _dot_hilo_rhs(1.0 / nrm, segt_ref[...])
        k = k * (1.0 + (a - 1.0) * kaw)

        cum = _dot_exact_rhs_lhs(tri_ref[...], logdec)
        ginc = jnp.exp(cum)
        ginv = jnp.exp(-cum)
        gprev = jnp.exp(cum - logdec)

        r_o[0, :, cols] = r * ginc
        k_o[0, :, cols] = k * ginv
        a_o[0, :, cols] = -kk * gprev
        b_o[0, :, cols] = kk * a * ginv
        v_o[0, :, cols] = v
        g_o[0, :, cols] = g
        for cc in range(tm // CHUNK):
            first, last = cc * CHUNK, cc * CHUNK + CHUNK - 1
            gl_o[0, cc, 0:1, cols] = jnp.exp(logdec[first:first + 1, :] - cum[first:first + 1, :])
            gl_o[0, cc, 1:2, cols] = ginc[last:last + 1, :]


def _dot_exact_rhs_lhs(m_bf16, x):
    h, mid, l = _split3(x)
    return _mm(m_bf16, h) + (_mm(m_bf16, mid) + _mm(m_bf16, l))


def _rwkv_pre(x, mod, p, v_first):
    bsz, t, d = x.shape
    tm = PRE_TM
    has_vres = v_first is not None
    row = pl.BlockSpec((1, tm, d), lambda b, i: (b, i, 0))

    def full(shape):
        return pl.BlockSpec(shape, lambda b, i: (0,) * len(shape))

    cw = 2 * LANES
    seg = np.zeros((cw, LANES), np.float32)
    seg[np.arange(cw), np.arange(cw) // RWKV_HEAD] = 1.0
    idx = np.arange(tm)
    same = idx[:, None] // CHUNK == idx[None, :] // CHUNK
    col = idx[None, :] % CHUNK
    tri = (same & (col <= idx[:, None] % CHUNK)).astype(np.float32) \
        - (same & (col <= CHUNK // 2 - 1)).astype(np.float32)
    vec_rows = [p['w0'], p['a0'], p['k_k'], p['k_a']] + ([p['v0']] if has_vres else [])
    vec = jnp.stack(vec_rows + [jnp.zeros_like(p['w0'])] * (8 - len(vec_rows)))
    dl, da, dg = p['w1'].shape[1], p['a1'].shape[1], p['g1'].shape[1]
    ins = [x, x, mod, p['mu'], p['wr'], p['wk'], p['wv'], p['w1'], p['w2'], p['a1'], p['a2'],
           p['g1'], p['g2'], vec, jnp.asarray(seg, BF16), jnp.asarray(seg.T, BF16),
           jnp.asarray(tri, BF16)]
    specs = [row,
             pl.BlockSpec((1, 8, d), lambda b, i: (b, jnp.maximum(i * (tm // 8) - 1, 0), 0)),
             pl.BlockSpec((1, 6, d), lambda b, i: (b, 0, 0)),
             full((6, d)), full((d, d)), full((d, d)), full((d, d)),
             full((d, dl)), full((dl, d)), full((d, da)), full((da, d)),
             full((d, dg)), full((dg, d)), full((8, d)),
             full((cw, LANES)), full((LANES, cw)), full((tm, tm))]
    if has_vres:
        dv = p['v1'].shape[1]
        ins += [v_first, p['v1'], p['v2']]
        specs += [row, full((d, dv)), full((dv, d))]
    act = jax.ShapeDtypeStruct((bsz, t, d), F32)
    nch = t // CHUNK
    outs = pl.pallas_call(
        functools.partial(_rwkv_pre_body, has_vres),
        grid=(bsz, t // tm),
        in_specs=specs,
        out_specs=[row] * 6 + [pl.BlockSpec((1, tm // CHUNK, 2, d), lambda b, i: (b, i, 0, 0))],
        out_shape=[act] * 6 + [jax.ShapeDtypeStruct((bsz, nch, 2, d), F32)],
        scratch_shapes=[pltpu.VMEM((6, tm, d), BF16)],
        compiler_params=_cparams(("parallel", "parallel")),
        name="rwkv_pre",
    )(*ins)
    return outs


_BNN = (((2,), (1,)), ((0,), (0,)))
_BNT = (((2,), (2,)), ((0,), (0,)))


def _rwkv_scan_body(r_ref, k_ref, a_ref, b_ref, v_ref, gl_ref, vec_ref, o_ref, s_ref):
    c = pl.program_id(1)

    @pl.when(c == 0)
    def _():
        s_ref[...] = jnp.zeros_like(s_ref)

    L = CHUNK
    NP = RWKV_HEADS // 2
    shp = (NP, L, LANES)
    plane = lax.broadcasted_iota(jnp.int32, (1, L, 2 * L), 2)
    ph0 = plane < L
    tt = lax.broadcasted_iota(jnp.int32, (1, L, 2 * L), 1)
    ss = plane & (L - 1)
    strict = ss < tt
    incl = ss <= tt
    eye = jnp.where(ss == tt, 1.0, 0.0)
    base_bits = 4

    def same_block(bits):
        return (ss >> bits) == (tt >> bits)

    lvl0 = strict & same_block(base_bits)
    merges = [strict & same_block(bits + 1) & ((ss >> bits) != (tt >> bits))
              for bits in range(base_bits, (L - 1).bit_length())]
    dh0 = lax.broadcasted_iota(jnp.int32, (1, L, LANES), 2) < RWKV_HEAD
    ri = lax.broadcasted_iota(jnp.int32, (1, LANES, LANES), 1)
    ci = lax.broadcasted_iota(jnp.int32, (1, LANES, LANES), 2)
    blockdiag = (ri >> 6) == (ci >> 6)
    seg_mean = jnp.where(blockdiag[0], 1.0, 0.0).astype(BF16)

    def grp(ref):
        return jnp.stack([ref[0, :, hp * LANES:(hp + 1) * LANES] for hp in range(NP)])

    def bd(x):
        return jnp.concatenate([jnp.where(dh0, x, 0.0), jnp.where(dh0, 0.0, x)], axis=1)

    def bdp(y):
        return jnp.concatenate([jnp.where(ph0, y, 0.0), jnp.where(ph0, 0.0, y)], axis=1)

    def pmm(xp, y):
        return _dot1(xp, bd(y), _BNN)

    def ppm(xp, yp):
        return _dot1(xp, bdp(yp), _BNN)

    R, K, A, Bv, V = grp(r_ref), grp(k_ref), grp(a_ref), grp(b_ref), grp(v_ref)
    gvec = lambda n: jnp.stack([gl_ref[0, 0, n:n + 1, hp * LANES:(hp + 1) * LANES] for hp in range(NP)])
    S = s_ref[...] * gvec(0)

    AR = jnp.concatenate([A, R], axis=1)
    BK = jnp.concatenate([bd(Bv), bd(K)], axis=1)
    G = _dot1(AR, BK, _BNT)
    A_ab = jnp.where(strict, G[:, 0:L, 0:2 * L], 0.0)
    A_ak = jnp.where(strict, G[:, 0:L, 2 * L:4 * L], 0.0)
    A_rb = jnp.where(incl, G[:, L:2 * L, 0:2 * L], 0.0)
    A_rk = jnp.where(incl, G[:, L:2 * L, 2 * L:4 * L], 0.0)

    a0 = jnp.where(lvl0, A_ab, 0.0)
    Tm = eye + a0
    P = a0
    for _ in range(base_bits - 1):
        P = ppm(P, P)
        Tm = Tm + ppm(Tm, P)
    for lvl in merges:
        Tm = Tm + ppm(ppm(Tm, jnp.where(lvl, A_ab, 0.0)), Tm)

    PQ = _dot1(AR, S, _BNT)
    W = PQ[:, 0:L] + pmm(A_ak, V)
    U = pmm(Tm, W)
    Y = PQ[:, L:2 * L] + _dot1(jnp.concatenate([A_rb, A_rk], axis=2),
                               jnp.concatenate([bd(U), bd(V)], axis=1), _BNN)
    UV = jnp.concatenate([U, V], axis=1)
    UVt = jnp.stack([UV[hp].T for hp in range(NP)])
    BK2 = jnp.concatenate([Bv, K], axis=1)
    upd = _dot1(UVt, BK2, _BNN)
    s_ref[...] = (S + jnp.where(blockdiag, upd, 0.0)) * gvec(1)

    vrow = lambda n: jnp.stack([vec_ref[n:n + 1, hp * LANES:(hp + 1) * LANES] for hp in range(NP)])
    inv_n = 1.0 / RWKV_HEAD
    flat = lambda x: x.reshape(NP * L, LANES)
    mean = _dot_hilo_rhs(flat(Y), seg_mean).reshape(shp) * inv_n
    yc = Y - mean
    var = _dot_hilo_rhs(flat(yc * yc), seg_mean).reshape(shp) * inv_n
    yn = yc * lax.rsqrt(var + GN_EPS) * vrow(1) + vrow(2)
    bonus = _dot_hilo_rhs(flat(R * K * vrow(0)), seg_mean).reshape(shp)
    out = yn + bonus * V
    for hp in range(NP):
        o_ref[0, :, hp * LANES:(hp + 1) * LANES] = out[hp]


def _rwkv_scan(r, k, a, b, v, gl, r_k, lnx_g, lnx_b):
    bsz, t, d = r.shape
    row = pl.BlockSpec((1, CHUNK, d), lambda bb, c: (bb, c, 0))
    vec = jnp.stack([r_k.reshape(d), lnx_g, lnx_b] + [jnp.zeros((d,), F32)] * 5)
    return pl.pallas_call(
        _rwkv_scan_body,
        grid=(bsz, t // CHUNK),
        in_specs=[row] * 5 + [
            pl.BlockSpec((1, 1, 2, d), lambda bb, c: (bb, c, 0, 0)),
            pl.BlockSpec((8, d), lambda bb, c: (0, 0)),
        ],
        out_specs=row,
        out_shape=jax.ShapeDtypeStruct((bsz, t, d), F32),
        scratch_shapes=[pltpu.VMEM((RWKV_HEADS // 2, LANES, LANES), F32)],
        compiler_params=_cparams(("parallel", "arbitrary")),
        name="rwkv_scan",
    )(r, k, a, b, v, gl, vec)


IDX_COLS = 768


def _dsa_proj_body(x_ref, mod_ref, wq_ref, wc_ref, wi_ref, kvn_ref,
                   q_o, ckv_o, qi_o, ki_o, wi_o):
    hin = (x_ref[0] * (1.0 + mod_ref[0, 1:2, :]) + mod_ref[0, 0:1, :]).astype(BF16)
    q_o[0] = _mm(hin, wq_ref[...]).astype(BF16)
    ckv = _mm(hin, wc_ref[...])
    ms = jnp.mean(ckv * ckv, axis=-1, keepdims=True)
    ckv_o[0] = (ckv * lax.rsqrt(ms + 1e-6) * kvn_ref[...]).astype(BF16)
    idx = _mm(hin, wi_ref[...])
    nq = IDX_HEADS * IDX_DIM
    qi_o[0] = idx[:, 0:nq].astype(BF16)
    ki_o[0] = idx[:, nq:nq + LANES].astype(BF16)
    wi_o[0] = idx[:, nq + LANES:nq + 2 * LANES] * (IDX_HEADS ** -0.5 * IDX_DIM ** -0.5)


def _dsa_proj(x, mod, w_in, kv_norm):
    bsz, t, d = x.shape
    c1 = ATT_HEADS * ATT_HEAD_DIM
    c2 = c1 + KV_LATENT
    c3 = c2 + IDX_HEADS * IDX_DIM
    c4 = c3 + IDX_DIM
    wq = w_in[:, :c1].astype(BF16)
    wc = w_in[:, c1:c2].astype(BF16)
    pad = IDX_COLS - (c3 - c2) - 2 * IDX_DIM - IDX_HEADS
    widx = jnp.concatenate([w_in[:, c2:c3], w_in[:, c3:c4], w_in[:, c3:c4], w_in[:, c4:],
                            jnp.zeros((d, pad), F32)], axis=1).astype(BF16)
    tm = PROJ_TM

    def full(shape):
        return pl.BlockSpec(shape, lambda b, i: (0,) * len(shape))

    def row(n):
        return pl.BlockSpec((1, tm, n), lambda b, i: (b, i, 0))

    def act(n, dtype):
        return jax.ShapeDtypeStruct((bsz, t, n), dtype)

    nq = IDX_HEADS * IDX_DIM
    return pl.pallas_call(
        _dsa_proj_body,
        grid=(bsz, t // tm),
        in_specs=[row(d), pl.BlockSpec((1, 6, d), lambda b, i: (b, 0, 0)),
                  full((d, c1)), full((d, KV_LATENT)), full((d, IDX_COLS)),
                  full((1, KV_LATENT))],
        out_specs=[row(c1), row(KV_LATENT), row(nq), row(LANES), row(LANES)],
        out_shape=[act(c1, BF16), act(KV_LATENT, BF16), act(nq, BF16), act(LANES, BF16), act(LANES, F32)],
        compiler_params=_cparams(("parallel", "parallel")),
        name="dsa_proj",
    )(x, mod, wq, wc, widx, kv_norm.reshape(1, KV_LATENT))


def _dsa_index_body(k_sel, qi_ref, wi_ref, ki_ref, o_ref, key_ref):
    i = pl.program_id(1)
    tq = qi_ref.shape[1]
    t = ki_ref.shape[1]
    nchunk = t // tq
    wt = wi_ref[0].T
    lane = lax.broadcasted_iota(jnp.int32, (tq, LANES), 1)
    first = lane < IDX_DIM
    qheads = []
    for hp in range(IDX_HEADS // 2):
        qp = qi_ref[0, :, hp * LANES:(hp + 1) * LANES].astype(F32)
        qheads.append(jnp.where(first, qp, 0.0).astype(BF16))
        qheads.append(jnp.where(first, 0.0, qp).astype(BF16))
    q_all = jnp.concatenate(qheads, axis=0)
    krow = lax.broadcasted_iota(jnp.int32, (tq, tq), 0)
    qlane = lax.broadcasted_iota(jnp.int32, (tq, tq), 1)

    kf = float(k_sel)
    nbits = int(t - 1).bit_length()

    def colsum(ind):
        return jnp.sum(jnp.sum(ind.reshape(tq // 32, 4, 8, tq), axis=0), axis=0)

    def tile(nc):
        chunks = [slice(c * tq, (c + 1) * tq) for c in range(nc)]
        diag = krow <= qlane

        for c, rows in enumerate(chunks):
            kk2 = ki_ref[0, rows, :]
            s_all = _mm(kk2, q_all, _NT)
            score = jnp.zeros((tq, tq), F32)
            for h in range(IDX_HEADS):
                score = score + wt[h:h + 1, :] * jnp.maximum(s_all[:, h * tq:(h + 1) * tq], 0.0)
            score = jnp.where(score == 0.0, 0.0, score)
            bits = pltpu.bitcast(score, jnp.int32)
            skey = bits ^ ((bits >> 31) & 0x7FFFFFFF)
            key_ref[rows, :] = jnp.where(diag, skey, INT_MIN) if c == nc - 1 else skey

        def count(fn):
            acc = jnp.zeros((8, tq), F32)
            for c, rows in enumerate(chunks):
                acc = acc + colsum(fn(key_ref[rows, :], c * tq + krow))
            return jnp.sum(acc, axis=0, keepdims=True)

        def count_ge(cand):
            return count(lambda keys, kpos: jnp.where(keys >= cand, 1.0, 0.0))

        thr0 = jnp.where(count_ge(jnp.zeros((1, tq), jnp.int32)) >= kf, 0, INT_MIN).astype(jnp.int32)

        def thr_step(n, thr):
            cand = thr | jnp.left_shift(jnp.int32(1), 30 - n)
            return jnp.where(count_ge(cand) >= kf, cand, thr)

        thr = lax.fori_loop(0, 31, thr_step, thr0)
        n_gt = count(lambda keys, kpos: jnp.where(keys > thr, 1.0, 0.0))
        n_eq = count(lambda keys, kpos: jnp.where(keys == thr, 1.0, 0.0))
        need = kf - n_gt

        def tie_cut():
            def cut_step(n, cut):
                cand = cut | jnp.left_shift(jnp.int32(1), nbits - 1 - n)
                cnt = count(lambda keys, kpos: jnp.where(keys == thr, jnp.where(kpos < cand, 1.0, 0.0), 0.0))
                return jnp.where(cnt < need, cand, cut)
            return lax.fori_loop(0, nbits, cut_step, jnp.zeros((1, tq), jnp.int32))

        cut = lax.cond(jnp.max(n_eq - need) > 0.0, tie_cut, lambda: jnp.full((1, tq), t, jnp.int32))

        for c, rows in enumerate(chunks):
            keys = key_ref[rows, :]
            tie = jnp.where(keys == thr, jnp.where(c * tq + krow <= cut, 0.0, MASK_NEG), MASK_NEG)
            bias = jnp.where(keys > thr, 0.0, tie)
            if c == nc - 1:
                bias = jnp.where(diag, bias, MASK_NEG)
            o_ref[0, rows, :] = bias.astype(BF16)
        for c in range(nc, nchunk):
            o_ref[0, c * tq:(c + 1) * tq, :] = jnp.full((tq, tq), MASK_NEG, BF16)

    for nc in range(1, nchunk + 1):
        pl.when(i == nc - 1)(functools.partial(tile, nc))


def _dsa_index(qi, ki, wi, k_sel):
    bsz, t, nq = qi.shape
    tq = ATT_T
    return pl.pallas_call(
        functools.partial(_dsa_index_body, k_sel),
        grid=(bsz, t // tq),
        in_specs=[pl.BlockSpec((1, tq, nq), lambda b, i: (b, i, 0)),
                  pl.BlockSpec((1, tq, LANES), lambda b, i: (b, i, 0)),
                  pl.BlockSpec((1, t, LANES), lambda b, i: (b, 0, 0))],
        out_specs=pl.BlockSpec((1, t, tq), lambda b, i: (b, 0, i)),
        out_shape=jax.ShapeDtypeStruct((bsz, t, t), BF16),
        scratch_shapes=[pltpu.VMEM((t, tq), jnp.int32)],
        compiler_params=_cparams(("parallel", "parallel")),
        name="dsa_index",
    )(qi, wi, ki)


def _t5_bucket_np(n):
    n = np.maximum(n, 0)
    max_exact = REL_BUCKETS // 2
    nf = np.maximum(n, 1).astype(np.float32)
    large = max_exact + (np.log(nf / np.float32(max_exact)) / np.float32(math.log(REL_MAX_DIST / max_exact))
                         * np.float32(REL_BUCKETS - max_exact)).astype(np.int32)
    large = np.minimum(large, REL_BUCKETS - 1)
    return np.where(n < max_exact, n, large).astype(np.int32)


def _band_body(bkt_ref, rb_ref, o_ref):
    h = pl.program_id(1)
    bkt = bkt_ref[0]
    acc = jnp.zeros(bkt.shape, F32)
    for b in range(REL_BUCKETS):
        acc = jnp.where(bkt == b, rb_ref[b, h], acc)
    o_ref[0] = acc * LOG2E


def _band_bias(rel_bias):
    tt = ATT_T
    kc = np.arange(tt)[:, None]
    qr = np.arange(tt)[None, :]
    planes = [_t5_bucket_np(d * tt + qr - kc) for d in range(3)]
    assert (planes[2] == REL_BUCKETS - 1).all() and tt + 1 >= 113
    bkt = jnp.asarray(np.stack(planes))
    return pl.pallas_call(
        _band_body,
        grid=(3, ATT_HEADS),
        in_specs=[pl.BlockSpec((1, tt, tt), lambda d, h: (d, 0, 0)),
                  pl.BlockSpec(memory_space=pltpu.SMEM)],
        out_specs=pl.BlockSpec((1, tt, tt), lambda d, h: (d, 0, h)),
        out_shape=jax.ShapeDtypeStruct((3, tt, ATT_HEADS * tt), F32),
        compiler_params=_cparams(("parallel", "parallel")),
        name="band_bias",
    )(bkt, rel_bias)


LOG2E = 1.4426950408889634
ACC_ROWS = KV_LATENT + 16


def _dsa_attn_body(qt_ref, kt_ref, q_ref, ckv_ref, mask_ref, wuk_ref, wuv_ref, near_ref, far_ref,
                   x_ref, mod_ref, wout_ref, lng_ref, lnb_ref, o_ref,
                   ql_ref, m_ref, sm_ref, ot_ref, *acc_refs):
    i = qt_ref[pl.program_id(1)]
    kp = kt_ref[pl.program_id(1)]
    gap = i - 2 * kp
    nh = ATT_HEADS
    tq = q_ref.shape[1]
    tk = tq
    qscale = ATT_HEAD_DIM ** -0.5 * LOG2E

    @pl.when(kp == 0)
    def _():
        for hp in range(nh // 2):
            qp = q_ref[0, :, hp * LANES:(hp + 1) * LANES].astype(BF16)
            qlat = _mm(qp, wuk_ref[hp]) * qscale
            ql_ref[2 * hp * tq:(2 * hp + 1) * tq, :] = qlat[:, 0:KV_LATENT].astype(BF16)
            ql_ref[(2 * hp + 1) * tq:(2 * hp + 2) * tq, :] = qlat[:, KV_LATENT:2 * KV_LATENT].astype(BF16)
        m_ref[...] = jnp.full(m_ref.shape, MASK_NEG, F32)
        for acc_ref in acc_refs:
            acc_ref[...] = jnp.zeros_like(acc_ref)

    def step(*kinds):
        nk = len(kinds) * tk
        ckv = ckv_ref[0, 0:nk, :]
        ckv_aug = jnp.concatenate([ckv.astype(F32).T, jnp.ones((ACC_ROWS - KV_LATENT, nk), F32)],
                                  axis=0).astype(BF16)
        maskb = mask_ref[0, 0:nk, :].astype(F32)
        m_prev = m_ref[...]
        m_news = []
        for h in range(nh):
            hs = slice(h * tq, (h + 1) * tq)
            s = _mm(ckv, ql_ref[hs, :], _NT)
            m_new = m_prev[:, hs]
            for n, plane in enumerate(kinds):
                rows = slice(n * tk, (n + 1) * tk)
                sm = s[rows] + maskb[rows]
                if plane is None:
                    m_new = jnp.maximum(m_new, jnp.max(sm, axis=0, keepdims=True) + far_ref[:, hs])
                else:
                    sm = sm + near_ref[plane, :, hs]
                    m_new = jnp.maximum(m_new, jnp.max(sm, axis=0, keepdims=True))
                sm_ref[rows, hs] = sm
            m_news.append(m_new)
        for h in range(nh):
            hs = slice(h * tq, (h + 1) * tq)
            m_new = m_news[h]
            p = jnp.concatenate(
                [jnp.exp2(sm_ref[n * tk:(n + 1) * tk, hs]
                          - (m_new - far_ref[:, hs] if plane is None else m_new)).astype(BF16)
                 for n, plane in enumerate(kinds)], axis=0)
            alpha = jnp.exp2(m_prev[:, hs] - m_new)
            acc_refs[h][...] = alpha * acc_refs[h][...] + _mm(ckv_aug, p)
        m_ref[...] = jnp.concatenate(m_news, axis=1)

    @pl.when(gap >= 3)
    def _():
        step(None, None)

    @pl.when(gap == 2)
    def _():
        step(None, 1)

    @pl.when(gap == 1)
    def _():
        step(1, 0)

    @pl.when(gap == 0)
    def _():
        step(0)

    @pl.when(gap <= 1)
    def _():
        def norm(h):
            a = acc_refs[h][...]
            return a[0:KV_LATENT] * (1.0 / a[KV_LATENT:KV_LATENT + 1])
        for hp in range(nh // 2):
            olat = jnp.concatenate([norm(2 * hp), norm(2 * hp + 1)], axis=0).astype(BF16)
            ot_ref[hp * LANES:(hp + 1) * LANES, :] = _mm(wuv_ref[hp], olat)
        y = _mm(ot_ref[...].T.astype(BF16), wout_ref[...])
        res = DEEPNORM_ALPHA * x_ref[0] + (1.0 + mod_ref[0, 2:3, :]) * y
        o_ref[0] = _layernorm(res, lng_ref[...], lnb_ref[...])


def _dsa_attn(q, ckv, maskt, w_uk, w_uv, band, x, mod, w_out, ln_g, ln_b):
    bsz, t, d = q.shape
    tt = ATT_T
    nt = t // tt
    nh = ATT_HEADS
    zk = jnp.zeros((nh // 2, ATT_HEAD_DIM, KV_LATENT), F32)
    wuk2 = jnp.concatenate([jnp.concatenate([w_uk[0::2], zk], axis=2),
                            jnp.concatenate([zk, w_uk[1::2]], axis=2)], axis=1).astype(BF16)
    wuv_t = jnp.swapaxes(w_uv, 1, 2)
    zv = jnp.zeros((nh // 2, ATT_HEAD_DIM, KV_LATENT), F32)
    wuv2 = jnp.concatenate([jnp.concatenate([wuv_t[0::2], zv], axis=2),
                            jnp.concatenate([zv, wuv_t[1::2]], axis=2)], axis=1).astype(BF16)
    assert nt % 2 == 0
    steps = [(i, kp) for i in range(nt) for kp in range(i // 2 + 1)]
    q_tab = jnp.asarray([p[0] for p in steps], jnp.int32)
    k_tab = jnp.asarray([p[1] for p in steps], jnp.int32)
    grid_spec = pltpu.PrefetchScalarGridSpec(
        num_scalar_prefetch=2,
        grid=(bsz, len(steps)),
        in_specs=[pl.BlockSpec((1, tt, d), lambda b, s, qt, kt: (b, qt[s], 0)),
                  pl.BlockSpec((1, 2 * tt, KV_LATENT), lambda b, s, qt, kt: (b, kt[s], 0)),
                  pl.BlockSpec((1, 2 * tt, tt), lambda b, s, qt, kt: (b, kt[s], qt[s])),
                  pl.BlockSpec((nh // 2, LANES, 2 * KV_LATENT), lambda b, s, qt, kt: (0, 0, 0)),
                  pl.BlockSpec((nh // 2, LANES, 2 * KV_LATENT), lambda b, s, qt, kt: (0, 0, 0)),
                  pl.BlockSpec((2, tt, nh * tt), lambda b, s, qt, kt: (0, 0, 0)),
                  pl.BlockSpec((1, nh * tt), lambda b, s, qt, kt: (0, 0)),
                  pl.BlockSpec((1, tt, d), lambda b, s, qt, kt: (b, qt[s], 0)),
                  pl.BlockSpec((1, 6, d), lambda b, s, qt, kt: (b, 0, 0)),
                  pl.BlockSpec((d, d), lambda b, s, qt, kt: (0, 0)),
                  pl.BlockSpec((1, d), lambda b, s, qt, kt: (0, 0)),
                  pl.BlockSpec((1, d), lambda b, s, qt, kt: (0, 0))],
        out_specs=pl.BlockSpec((1, tt, d), lambda b, s, qt, kt: (b, qt[s], 0)),
        scratch_shapes=[pltpu.VMEM((nh * tt, KV_LATENT), BF16),
                        pltpu.VMEM((1, nh * tt), F32),
                        pltpu.VMEM((2 * tt, nh * tt), F32),
                        pltpu.VMEM((d, tt), F32)]
        + [pltpu.VMEM((ACC_ROWS, tt), F32)] * nh)
    return pl.pallas_call(
        _dsa_attn_body,
        grid_spec=grid_spec,
        out_shape=jax.ShapeDtypeStruct((bsz, t, d), F32),
        compiler_params=_cparams(("parallel", "arbitrary")),
        name="dsa_attn",
    )(q_tab, k_tab, q, ckv, maskt, wuk2, wuv2, band[:2], band[2, 0:1, :],
      x, mod, w_out, ln_g.reshape(1, d), ln_b.reshape(1, d))


def kernel(x, c, ada_w, ada_b, ln_g, ln_b, ffn_w_in, ffn_w_out, rwkv_mu, rwkv_w_rkv, rwkv_w0, rwkv_w1, rwkv_w2, rwkv_a0, rwkv_a1, rwkv_a2, rwkv_v0, rwkv_v1, rwkv_v2, rwkv_g1, rwkv_g2, rwkv_k_k, rwkv_k_a, rwkv_r_k, rwkv_lnx_g, rwkv_lnx_b, rwkv_w_out, dsa_w_in, dsa_kv_norm, dsa_w_uk, dsa_w_uv, dsa_w_out, rel_bias):
    bsz, t, d = x.shape
    assert d == D_MODEL and t % ATT_T == 0 and t % FFN_TM == 0 and t % PRE_TM == 0
    mod_all = _adaln(c, ada_w, ada_b).reshape(DEPTH, bsz, 6, d)
    band = _band_bias(rel_bias)
    k_sel = min(TOPK_MAX, t // TOPK_DIV)
    bf = lambda w: w.astype(BF16)
    v_first = None
    for i in range(DEPTH):
        mod = mod_all[i]
        j = i // 2
        if i % 2 == 0:
            p = dict(mu=rwkv_mu[j], wr=bf(rwkv_w_rkv[j, 0]), wk=bf(rwkv_w_rkv[j, 1]), wv=bf(rwkv_w_rkv[j, 2]),
                     w0=rwkv_w0[j], w1=bf(rwkv_w1[j]), w2=bf(rwkv_w2[j]),
                     a0=rwkv_a0[j], a1=bf(rwkv_a1[j]), a2=bf(rwkv_a2[j]),
                     g1=bf(rwkv_g1[j]), g2=bf(rwkv_g2[j]), k_k=rwkv_k_k[j], k_a=rwkv_k_a[j])
            if j > 0:
                p.update(v0=rwkv_v0[j - 1], v1=bf(rwkv_v1[j - 1]), v2=bf(rwkv_v2[j - 1]))
            r_s, k_s, a_s, b_s, v, g, gl = _rwkv_pre(x, mod, p, v_first if j > 0 else None)
            if j == 0:
                v_first = v
            z = _rwkv_scan(r_s, k_s, a_s, b_s, v, gl, rwkv_r_k[j], rwkv_lnx_g[j], rwkv_lnx_b[j])
            tail = (z, g, bf(rwkv_w_out[j]), ln_g[i, 0], ln_b[i, 0])
        else:
            tail = None
            q, ckv, qi, ki, wi = _dsa_proj(x, mod, dsa_w_in[j], dsa_kv_norm[j])
            maskt = _dsa_index(qi, ki, wi, k_sel)
            x = _dsa_attn(q, ckv, maskt, dsa_w_uk[j], dsa_w_uv[j], band,
                          x, mod, bf(dsa_w_out[j]), ln_g[i, 0], ln_b[i, 0])
        x = _ffn(x, mod, bf(ffn_w_in[i]), bf(ffn_w_out[i]), ln_g[i, 1], ln_b[i, 1], tail)
    return x
```

```python
import functools
import math

import numpy as np
import jax
import jax.numpy as jnp
from jax import lax
from jax.experimental import pallas as pl
from jax.experimental.pallas import tpu as pltpu

F32 = jnp.float32
BF16 = jnp.bfloat16

D_MODEL = 1024
DEPTH = 4
RWKV_HEAD = 64
RWKV_HEADS = D_MODEL // RWKV_HEAD
GN_EPS = RWKV_HEAD * 1e-5
ATT_HEADS = 16
ATT_HEAD_DIM = 64
KV_LATENT = 128
IDX_HEADS = 8
IDX_DIM = 64
TOPK_MAX = 256
TOPK_DIV = 4
REL_BUCKETS = 32
REL_MAX_DIST = 128
DEEPNORM_ALPHA = (2 * DEPTH) ** 0.25
LN_EPS = 1e-5

LANES = 128
CHUNK = 128
SCAN_NB = 2
PRE_TM = 256
PROJ_TM = 512
FFN_TM = 512
FFN_TF = 2816
ATT_T = 256
MASK_NEG = -1e30
V7X_VMEM_BYTES = 64 * 1024 * 1024
VMEM_LIMIT = V7X_VMEM_BYTES - 8 * 1024 * 1024
INT_MIN = -2 ** 31


def _cparams(sem):
    return pltpu.CompilerParams(dimension_semantics=sem, vmem_limit_bytes=VMEM_LIMIT)


def _split2(x):
    hi = x.astype(BF16)
    lo = (x - hi.astype(F32)).astype(BF16)
    return hi, lo


def _split3(x):
    hi = x.astype(BF16)
    r1 = x - hi.astype(F32)
    mid = r1.astype(BF16)
    lo = (r1 - mid.astype(F32)).astype(BF16)
    return hi, mid, lo


_NN = (((1,), (0,)), ((), ()))
_NT = (((1,), (1,)), ((), ()))


def _mm(a, b, dims=_NN):
    return lax.dot_general(a, b, dims, preferred_element_type=F32)


def _dot3(a, b, dims=_NN):
    ah, al = _split2(a)
    bh, bl = _split2(b)
    return _mm(ah, bh, dims) + (_mm(ah, bl, dims) + _mm(al, bh, dims))


def _dot1(a, b, dims=_NN):
    return _mm(a.astype(BF16), b.astype(BF16), dims)


def _dot_hilo_rhs(a, b_bf16):
    h, l = _split2(a)
    return _mm(h, b_bf16) + _mm(l, b_bf16)


def _sigmoid(x):
    return 1.0 / (1.0 + jnp.exp(-x))


def _layernorm(xr, g, b):
    mu = jnp.mean(xr, axis=-1, keepdims=True)
    xc = xr - mu
    var = jnp.mean(xc * xc, axis=-1, keepdims=True)
    return xc * lax.rsqrt(var + LN_EPS) * g + b


def _adaln_body(c_ref, w_ref, b_ref, o_ref):
    c = c_ref[...]
    cond = c * _sigmoid(c)
    o_ref[0] = _dot3(cond, w_ref[0]) + b_ref[0]


def _adaln(c, ada_w, ada_b):
    depth, d, n = ada_w.shape
    bsz = c.shape[0]
    tn = n // 4
    return pl.pallas_call(
        _adaln_body,
        grid=(depth, n // tn),
        in_specs=[
            pl.BlockSpec((bsz, d), lambda i, j: (0, 0)),
            pl.BlockSpec((1, d, tn), lambda i, j: (i, 0, j)),
            pl.BlockSpec((1, 1, tn), lambda i, j: (i, 0, j)),
        ],
        out_specs=pl.BlockSpec((1, bsz, tn), lambda i, j: (i, 0, j)),
        out_shape=jax.ShapeDtypeStruct((depth, bsz, n), F32),
        compiler_params=_cparams(("parallel", "parallel")),
        name="adaln",
    )(c, ada_w, ada_b.reshape(depth, 1, n))


def _ffn_body(mixer_tail, *refs):
    if mixer_tail:
        (x_ref, mod_ref, wg_ref, wu_ref, wo_ref, lng_ref, lnb_ref,
         z_ref, g_ref, wm_ref, lng0_ref, lnb0_ref, o_ref, hin_ref, acc_ref, xm_ref) = refs
    else:
        x_ref, mod_ref, wg_ref, wu_ref, wo_ref, lng_ref, lnb_ref, o_ref, hin_ref, acc_ref = refs
    j = pl.program_id(2)

    @pl.when(j == 0)
    def _():
        x = x_ref[0]
        if mixer_tail:
            y = _mm((z_ref[0] * g_ref[0]).astype(BF16), wm_ref[...])
            x = _layernorm(DEEPNORM_ALPHA * x + (1.0 + mod_ref[0, 2:3, :]) * y, lng0_ref[...], lnb0_ref[...])
            xm_ref[...] = x
        hin_ref[...] = (x * (1.0 + mod_ref[0, 4:5, :]) + mod_ref[0, 3:4, :]).astype(BF16)
        acc_ref[...] = jnp.zeros_like(acc_ref)

    hin = hin_ref[...]
    gate = _mm(hin, wg_ref[...])
    up = _mm(hin, wu_ref[...])
    hid = (gate * _sigmoid(gate) * up).astype(BF16)
    acc_ref[...] += _mm(hid, wo_ref[...])

    @pl.when(j == pl.num_programs(2) - 1)
    def _():
        x = xm_ref[...] if mixer_tail else x_ref[0]
        res = DEEPNORM_ALPHA * x + (1.0 + mod_ref[0, 5:6, :]) * acc_ref[...]
        o_ref[0] = _layernorm(res, lng_ref[...], lnb_ref[...])


def _ffn(x, mod, w_in, w_out, ln_g, ln_b, mixer_tail=None):
    bsz, t, d = x.shape
    f = w_out.shape[0]
    nf = f // FFN_TF
    row = pl.BlockSpec((1, FFN_TM, d), lambda b, i, j: (b, i, 0))
    vec = pl.BlockSpec((1, d), lambda b, i, j: (0, 0))
    ins = [x, mod, w_in, w_in, w_out, ln_g.reshape(1, d), ln_b.reshape(1, d)]
    specs = [row,
             pl.BlockSpec((1, 6, d), lambda b, i, j: (b, 0, 0)),
             pl.BlockSpec((d, FFN_TF), lambda b, i, j: (0, j)),
             pl.BlockSpec((d, FFN_TF), lambda b, i, j: (0, nf + j)),
             pl.BlockSpec((FFN_TF, d), lambda b, i, j: (j, 0)),
             vec, vec]
    scratch = [pltpu.VMEM((FFN_TM, d), BF16), pltpu.VMEM((FFN_TM, d), F32)]
    if mixer_tail is not None:
        z, g, w_mix, ln_g0, ln_b0 = mixer_tail
        ins += [z, g, w_mix, ln_g0.reshape(1, d), ln_b0.reshape(1, d)]
        specs += [row, row, pl.BlockSpec((d, d), lambda b, i, j: (0, 0)), vec, vec]
        scratch += [pltpu.VMEM((FFN_TM, d), F32)]
    return pl.pallas_call(
        functools.partial(_ffn_body, mixer_tail is not None),
        grid=(bsz, t // FFN_TM, nf),
        in_specs=specs,
        out_specs=row,
        out_shape=jax.ShapeDtypeStruct((bsz, t, d), F32),
        scratch_shapes=scratch,
        compiler_params=_cparams(("parallel", "parallel", "arbitrary")),
        name="ffn",
    )(*ins)


def _rwkv_pre_body(has_vres, *refs):
    if has_vres:
        (x_ref, xp_ref, mod_ref, mu_ref, wr_ref, wk_ref, wv_ref, w1_ref, w2_ref, a1_ref, a2_ref,
         g1_ref, g2_ref, vec_ref, seg_ref, segt_ref, tri_ref, vf_ref, v1_ref, v2_ref,
         r_o, k_o, a_o, b_o, v_o, g_o, gl_o, mix_ref) = refs
    else:
        (x_ref, xp_ref, mod_ref, mu_ref, wr_ref, wk_ref, wv_ref, w1_ref, w2_ref, a1_ref, a2_ref,
         g1_ref, g2_ref, vec_ref, seg_ref, segt_ref, tri_ref,
         r_o, k_o, a_o, b_o, v_o, g_o, gl_o, mix_ref) = refs
    i = pl.program_id(1)
    sc = 1.0 + mod_ref[0, 1:2, :]
    sh = mod_ref[0, 0:1, :]
    hin = x_ref[0] * sc + sh
    tm, d = hin.shape
    prev_row = xp_ref[0, 7:8, :] * sc + sh
    prev_row = jnp.where(i == 0, 0.0, prev_row)
    rows = lax.broadcasted_iota(jnp.int32, hin.shape, 0)
    hprev = jnp.where(rows == 0, prev_row, pltpu.roll(hin, 1, 0))
    xx = hprev - hin
    for p in range(6):
        mix_ref[p] = (hin + xx * mu_ref[p:p + 1, :]).astype(BF16)
    lora_w = jnp.tanh(_mm(mix_ref[3], w1_ref[...])).astype(BF16)
    lora_a = _mm(mix_ref[4], a1_ref[...]).astype(BF16)
    lora_g = _sigmoid(_mm(mix_ref[5], g1_ref[...])).astype(BF16)
    if has_vres:
        lora_v = _mm(mix_ref[2], v1_ref[...]).astype(BF16)

    cw = 2 * LANES
    for cb in range(d // cw):
        cols = slice(cb * cw, (cb + 1) * cw)
        w0, a0, kkw, kaw = vec_ref[0:1, cols], vec_ref[1:2, cols], vec_ref[2:3, cols], vec_ref[3:4, cols]
        r = _mm(mix_ref[0], wr_ref[:, cols])
        k = _mm(mix_ref[1], wk_ref[:, cols])
        v = _mm(mix_ref[2], wv_ref[:, cols])

        wl = w0 + _mm(lora_w, w2_ref[:, cols])
        nz = -wl
        softplus = jnp.maximum(nz, 0.0) + jnp.log(1.0 + jnp.exp(-jnp.abs(nz)))
        logdec = -jnp.exp(-softplus - 0.5)

        a = _sigmoid(a0 + _mm(lora_a, a2_ref[:, cols]))
        if has_vres:
            vmix = _sigmoid(vec_ref[4:5, cols] + _mm(lora_v, v2_ref[:, cols]))
            v = v + (vf_ref[0, :, cols] - v) * vmix
        g = _mm(lora_g, g2_ref[:, cols])

        kk = k * kkw
        ss = _dot_hilo_rhs(kk * kk, seg_ref[...])
        nrm = jnp.maximum(jnp.sqrt(ss), 1e-12)
        kk = kk * _dot_hilo_rhs(1.0 / nrm, segt_ref[...])
        k = k * (1.0 + (a - 1.0) * kaw)

        cum = _dot_exact_rhs_lhs(tri_ref[...], logdec)
        ginc = jnp.exp(cum)
        ginv = jnp.exp(-cum)
        gprev = jnp.exp(cum - logdec)

        r_o[0, :, cols] = r * ginc
        k_o[0, :, cols] = k * ginv
        a_o[0, :, cols] = -kk * gprev
        b_o[0, :, cols] = kk * a * ginv
        v_o[0, :, cols] = v
        g_o[0, :, cols] = g
        for cc in range(tm // CHUNK):
            first, last = cc * CHUNK, cc * CHUNK + CHUNK - 1
            gl_o[0, cc, 0:1, cols] = jnp.exp(logdec[first:first + 1, :] - cum[first:first + 1, :])
            gl_o[0, cc, 1:2, cols] = ginc[last:last + 1, :]


def _dot_exact_rhs_lhs(m_bf16, x):
    h, mid, l = _split3(x)
    return _mm(m_bf16, h) + (_mm(m_bf16, mid) + _mm(m_bf16, l))


def _rwkv_pre(x, mod, p, v_first):
    bsz, t, d = x.shape
    tm = PRE_TM
    has_vres = v_first is not None
    row = pl.BlockSpec((1, tm, d), lambda b, i: (b, i, 0))

    def full(shape):
        return pl.BlockSpec(shape, lambda b, i: (0,) * len(shape))

    cw = 2 * LANES
    seg = np.zeros((cw, LANES), np.float32)
    seg[np.arange(cw), np.arange(cw) // RWKV_HEAD] = 1.0
    idx = np.arange(tm)
    same = idx[:, None] // CHUNK == idx[None, :] // CHUNK
    col = idx[None, :] % CHUNK
    tri = (same & (col <= idx[:, None] % CHUNK)).astype(np.float32) \
        - (same & (col <= CHUNK // 2 - 1)).astype(np.float32)
    vec_rows = [p['w0'], p['a0'], p['k_k'], p['k_a']] + ([p['v0']] if has_vres else [])
    vec = jnp.stack(vec_rows + [jnp.zeros_like(p['w0'])] * (8 - len(vec_rows)))
    dl, da, dg = p['w1'].shape[1], p['a1'].shape[1], p['g1'].shape[1]
    ins = [x, x, mod, p['mu'], p['wr'], p['wk'], p['wv'], p['w1'], p['w2'], p['a1'], p['a2'],
           p['g1'], p['g2'], vec, jnp.asarray(seg, BF16), jnp.asarray(seg.T, BF16),
           jnp.asarray(tri, BF16)]
    specs = [row,
             pl.BlockSpec((1, 8, d), lambda b, i: (b, jnp.maximum(i * (tm // 8) - 1, 0), 0)),
             pl.BlockSpec((1, 6, d), lambda b, i: (b, 0, 0)),
             full((6, d)), full((d, d)), full((d, d)), full((d, d)),
             full((d, dl)), full((dl, d)), full((d, da)), full((da, d)),
             full((d, dg)), full((dg, d)), full((8, d)),
             full((cw, LANES)), full((LANES, cw)), full((tm, tm))]
    if has_vres:
        dv = p['v1'].shape[1]
        ins += [v_first, p['v1'], p['v2']]
        specs += [row, full((d, dv)), full((dv, d))]
    act = jax.ShapeDtypeStruct((bsz, t, d), F32)
    nch = t // CHUNK
    outs = pl.pallas_call(
        functools.partial(_rwkv_pre_body, has_vres),
        grid=(bsz, t // tm),
        in_specs=specs,
        out_specs=[row] * 6 + [pl.BlockSpec((1, tm // CHUNK, 2, d), lambda b, i: (b, i, 0, 0))],
        out_shape=[act] * 6 + [jax.ShapeDtypeStruct((bsz, nch, 2, d), F32)],
        scratch_shapes=[pltpu.VMEM((6, tm, d), BF16)],
        compiler_params=_cparams(("parallel", "parallel")),
        name="rwkv_pre",
    )(*ins)
    return outs


_BNN = (((2,), (1,)), ((0,), (0,)))
_BNT = (((2,), (2,)), ((0,), (0,)))


def _rwkv_scan_body(r_ref, k_ref, a_ref, b_ref, v_ref, gl_ref, vec_ref, o_ref, s_ref):
    c = pl.program_id(1)

    @pl.when(c == 0)
    def _():
        s_ref[...] = jnp.zeros_like(s_ref)

    L = CHUNK
    nb = r_ref.shape[0]
    pairs = [(bi, hp) for bi in range(nb) for hp in range(RWKV_HEADS // 2)]
    NP = len(pairs)
    shp = (NP, L, LANES)
    plane = lax.broadcasted_iota(jnp.int32, (1, L, 2 * L), 2)
    ph0 = plane < L
    tt = lax.broadcasted_iota(jnp.int32, (1, L, 2 * L), 1)
    ss = plane & (L - 1)
    strict = ss < tt
    incl = ss <= tt
    eye = jnp.where(ss == tt, 1.0, 0.0)
    base_bits = 4

    def same_block(bits):
        return (ss >> bits) == (tt >> bits)

    lvl0 = strict & same_block(base_bits)
    merges = [strict & same_block(bits + 1) & ((ss >> bits) != (tt >> bits))
              for bits in range(base_bits, (L - 1).bit_length())]
    dh0 = lax.broadcasted_iota(jnp.int32, (1, L, LANES), 2) < RWKV_HEAD
    ri = lax.broadcasted_iota(jnp.int32, (1, LANES, LANES), 1)
    ci = lax.broadcasted_iota(jnp.int32, (1, LANES, LANES), 2)
    blockdiag = (ri >> 6) == (ci >> 6)
    seg_mean = jnp.where(blockdiag[0], 1.0, 0.0).astype(BF16)

    def grp(ref):
        return jnp.stack([ref[bi, :, hp * LANES:(hp + 1) * LANES] for bi, hp in pairs])

    def bd(x):
        return jnp.concatenate([jnp.where(dh0, x, 0.0), jnp.where(dh0, 0.0, x)], axis=1)

    def bdp(y):
        return jnp.concatenate([jnp.where(ph0, y, 0.0), jnp.where(ph0, 0.0, y)], axis=1)

    def pmm(xp, y):
        return _dot1(xp, bd(y), _BNN)

    def ppm(xp, yp):
        return _dot1(xp, bdp(yp), _BNN)

    R, K, A, Bv, V = grp(r_ref), grp(k_ref), grp(a_ref), grp(b_ref), grp(v_ref)
    gvec = lambda n: jnp.stack([gl_ref[bi, 0, n:n + 1, hp * LANES:(hp + 1) * LANES] for bi, hp in pairs])
    S = s_ref[...] * gvec(0)

    AR = jnp.concatenate([A, R], axis=1)
    BK = jnp.concatenate([bd(Bv), bd(K)], axis=1)
    G = _dot1(AR, BK, _BNT)
    A_ab = jnp.where(strict, G[:, 0:L, 0:2 * L], 0.0)
    A_ak = jnp.where(strict, G[:, 0:L, 2 * L:4 * L], 0.0)
    A_rb = jnp.where(incl, G[:, L:2 * L, 0:2 * L], 0.0)
    A_rk = jnp.where(incl, G[:, L:2 * L, 2 * L:4 * L], 0.0)

    a0 = jnp.where(lvl0, A_ab, 0.0)
    Tm = eye + a0
    P = a0
    for _ in range(base_bits - 1):
        P = ppm(P, P)
        Tm = Tm + ppm(Tm, P)
    for lvl in merges:
        Tm = Tm + ppm(ppm(Tm, jnp.where(lvl, A_ab, 0.0)), Tm)

    PQ = _dot1(AR, S, _BNT)
    W = PQ[:, 0:L] + pmm(A_ak, V)
    U = pmm(Tm, W)
    Y = PQ[:, L:2 * L] + _dot1(jnp.concatenate([A_rb, A_rk], axis=2),
                               jnp.concatenate([bd(U), bd(V)], axis=1), _BNN)
    UV = jnp.concatenate([U, V], axis=1)
    UVt = jnp.stack([UV[hp].T for hp in range(NP)])
    BK2 = jnp.concatenate([Bv, K], axis=1)
    upd = _dot1(UVt, BK2, _BNN)
    s_ref[...] = (S + jnp.where(blockdiag, upd, 0.0)) * gvec(1)

    vrow = lambda n: jnp.stack([vec_ref[n:n + 1, hp * LANES:(hp + 1) * LANES] for _, hp in pairs])
    inv_n = 1.0 / RWKV_HEAD
    flat = lambda x: x.reshape(NP * L, LANES)
    mean = _dot_hilo_rhs(flat(Y), seg_mean).reshape(shp) * inv_n
    yc = Y - mean
    var = _dot_hilo_rhs(flat(yc * yc), seg_mean).reshape(shp) * inv_n
    yn = yc * lax.rsqrt(var + GN_EPS) * vrow(1) + vrow(2)
    bonus = _dot_hilo_rhs(flat(R * K * vrow(0)), seg_mean).reshape(shp)
    out = yn + bonus * V
    for n, (bi, hp) in enumerate(pairs):
        o_ref[bi, :, hp * LANES:(hp + 1) * LANES] = out[n]


def _rwkv_scan(r, k, a, b, v, gl, r_k, lnx_g, lnx_b):
    bsz, t, d = r.shape
    nb = SCAN_NB
    row = pl.BlockSpec((nb, CHUNK, d), lambda bb, c: (bb, c, 0))
    vec = jnp.stack([r_k.reshape(d), lnx_g, lnx_b] + [jnp.zeros((d,), F32)] * 5)
    return pl.pallas_call(
        _rwkv_scan_body,
        grid=(bsz // nb, t // CHUNK),
        in_specs=[row] * 5 + [
            pl.BlockSpec((nb, 1, 2, d), lambda bb, c: (bb, c, 0, 0)),
            pl.BlockSpec((8, d), lambda bb, c: (0, 0)),
        ],
        out_specs=row,
        out_shape=jax.ShapeDtypeStruct((bsz, t, d), F32),
        scratch_shapes=[pltpu.VMEM((nb * RWKV_HEADS // 2, LANES, LANES), F32)],
        compiler_params=_cparams(("parallel", "arbitrary")),
        name="rwkv_scan",
    )(r, k, a, b, v, gl, vec)


IDX_COLS = 768


def _dsa_proj_body(x_ref, mod_ref, wq_ref, wc_ref, wi_ref, kvn_ref,
                   q_o, ckv_o, qi_o, ki_o, wi_o):
    hin = (x_ref[0] * (1.0 + mod_ref[0, 1:2, :]) + mod_ref[0, 0:1, :]).astype(BF16)
    q_o[0] = _mm(hin, wq_ref[...]).astype(BF16)
    ckv = _mm(hin, wc_ref[...])
    ms = jnp.mean(ckv * ckv, axis=-1, keepdims=True)
    ckv_o[0] = (ckv * lax.rsqrt(ms + 1e-6) * kvn_ref[...]).astype(BF16)
    idx = _mm(hin, wi_ref[...])
    nq = IDX_HEADS * IDX_DIM
    qi_o[0] = idx[:, 0:nq].astype(BF16)
    ki_o[0] = idx[:, nq:nq + LANES].astype(BF16)
    wi_o[0] = idx[:, nq + LANES:nq + 2 * LANES] * (IDX_HEADS ** -0.5 * IDX_DIM ** -0.5)


def _dsa_proj(x, mod, w_in, kv_norm):
    bsz, t, d = x.shape
    c1 = ATT_HEADS * ATT_HEAD_DIM
    c2 = c1 + KV_LATENT
    c3 = c2 + IDX_HEADS * IDX_DIM
    c4 = c3 + IDX_DIM
    wq = w_in[:, :c1].astype(BF16)
    wc = w_in[:, c1:c2].astype(BF16)
    pad = IDX_COLS - (c3 - c2) - 2 * IDX_DIM - IDX_HEADS
    widx = jnp.concatenate([w_in[:, c2:c3], w_in[:, c3:c4], w_in[:, c3:c4], w_in[:, c4:],
                            jnp.zeros((d, pad), F32)], axis=1).astype(BF16)
    tm = PROJ_TM

    def full(shape):
        return pl.BlockSpec(shape, lambda b, i: (0,) * len(shape))

    def row(n):
        return pl.BlockSpec((1, tm, n), lambda b, i: (b, i, 0))

    def act(n, dtype):
        return jax.ShapeDtypeStruct((bsz, t, n), dtype)

    nq = IDX_HEADS * IDX_DIM
    return pl.pallas_call(
        _dsa_proj_body,
        grid=(bsz, t // tm),
        in_specs=[row(d), pl.BlockSpec((1, 6, d), lambda b, i: (b, 0, 0)),
                  full((d, c1)), full((d, KV_LATENT)), full((d, IDX_COLS)),
                  full((1, KV_LATENT))],
        out_specs=[row(c1), row(KV_LATENT), row(nq), row(LANES), row(LANES)],
        out_shape=[act(c1, BF16), act(KV_LATENT, BF16), act(nq, BF16), act(LANES, BF16), act(LANES, F32)],
        compiler_params=_cparams(("parallel", "parallel")),
        name="dsa_proj",
    )(x, mod, wq, wc, widx, kv_norm.reshape(1, KV_LATENT))


def _dsa_index_body(k_sel, qi_ref, wi_ref, ki_ref, o_ref, key_ref):
    i = pl.program_id(1)
    tq = qi_ref.shape[1]
    t = ki_ref.shape[1]
    nchunk = t // tq
    wt = wi_ref[0].T
    lane = lax.broadcasted_iota(jnp.int32, (tq, LANES), 1)
    first = lane < IDX_DIM
    qheads = []
    for hp in range(IDX_HEADS // 2):
        qp = qi_ref[0, :, hp * LANES:(hp + 1) * LANES].astype(F32)
        qheads.append(jnp.where(first, qp, 0.0).astype(BF16))
        qheads.append(jnp.where(first, 0.0, qp).astype(BF16))
    q_all = jnp.concatenate(qheads, axis=0)
    krow = lax.broadcasted_iota(jnp.int32, (tq, tq), 0)
    qlane = lax.broadcasted_iota(jnp.int32, (tq, tq), 1)

    kf = float(k_sel)
    nbits = int(t - 1).bit_length()

    def colsum(ind):
        return jnp.sum(jnp.sum(ind.reshape(tq // 32, 4, 8, tq), axis=0), axis=0)

    def tile(nc):
        chunks = [slice(c * tq, (c + 1) * tq) for c in range(nc)]
        diag = krow <= qlane

        for c, rows in enumerate(chunks):
            kk2 = ki_ref[0, rows, :]
            s_all = _mm(kk2, q_all, _NT)
            score = jnp.zeros((tq, tq), F32)
            for h in range(IDX_HEADS):
                score = score + wt[h:h + 1, :] * jnp.maximum(s_all[:, h * tq:(h + 1) * tq], 0.0)
            score = jnp.where(score == 0.0, 0.0, score)
            bits = pltpu.bitcast(score, jnp.int32)
            skey = bits ^ ((bits >> 31) & 0x7FFFFFFF)
            key_ref[rows, :] = jnp.where(diag, skey, INT_MIN) if c == nc - 1 else skey

        def count(fn):
            acc = jnp.zeros((8, tq), F32)
            for c, rows in enumerate(chunks):
                acc = acc + colsum(fn(key_ref[rows, :], c * tq + krow))
            return jnp.sum(acc, axis=0, keepdims=True)

        def count_ge(cand):
            return count(lambda keys, kpos: jnp.where(keys >= cand, 1.0, 0.0))

        thr0 = jnp.where(count_ge(jnp.zeros((1, tq), jnp.int32)) >= kf, 0, INT_MIN).astype(jnp.int32)

        def thr_step(n, thr):
            cand = thr | jnp.left_shift(jnp.int32(1), 30 - n)
            return jnp.where(count_ge(cand) >= kf, cand, thr)

        thr = lax.fori_loop(0, 31, thr_step, thr0)
        n_gt = count(lambda keys, kpos: jnp.where(keys > thr, 1.0, 0.0))
        n_eq = count(lambda keys, kpos: jnp.where(keys == thr, 1.0, 0.0))
        need = kf - n_gt

        def tie_cut():
            def cut_step(n, cut):
                cand = cut | jnp.left_shift(jnp.int32(1), nbits - 1 - n)
                cnt = count(lambda keys, kpos: jnp.where(keys == thr, jnp.where(kpos < cand, 1.0, 0.0), 0.0))
                return jnp.where(cnt < need, cand, cut)
            return lax.fori_loop(0, nbits, cut_step, jnp.zeros((1, tq), jnp.int32))

        cut = lax.cond(jnp.max(n_eq - need) > 0.0, tie_cut, lambda: jnp.full((1, tq), t, jnp.int32))

        for c, rows in enumerate(chunks):
            keys = key_ref[rows, :]
            tie = jnp.where(keys == thr, jnp.where(c * tq + krow <= cut, 0.0, MASK_NEG), MASK_NEG)
            bias = jnp.where(keys > thr, 0.0, tie)
            if c == nc - 1:
                bias = jnp.where(diag, bias, MASK_NEG)
            o_ref[0, rows, :] = bias.astype(BF16)
        for c in range(nc, nchunk):
            o_ref[0, c * tq:(c + 1) * tq, :] = jnp.full((tq, tq), MASK_NEG, BF16)

    for nc in range(1, nchunk + 1):
        pl.when(i == nc - 1)(functools.partial(tile, nc))


def _dsa_index(qi, ki, wi, k_sel):
    bsz, t, nq = qi.shape
    tq = ATT_T
    return pl.pallas_call(
        functools.partial(_dsa_index_body, k_sel),
        grid=(bsz, t // tq),
        in_specs=[pl.BlockSpec((1, tq, nq), lambda b, i: (b, i, 0)),
                  pl.BlockSpec((1, tq, LANES), lambda b, i: (b, i, 0)),
                  pl.BlockSpec((1, t, LANES), lambda b, i: (b, 0, 0))],
        out_specs=pl.BlockSpec((1, t, tq), lambda b, i: (b, 0, i)),
        out_shape=jax.ShapeDtypeStruct((bsz, t, t), BF16),
        scratch_shapes=[pltpu.VMEM((t, tq), jnp.int32)],
        compiler_params=_cparams(("parallel", "parallel")),
        name="dsa_index",
    )(qi, wi, ki)


def _t5_bucket_np(n):
    n = np.maximum(n, 0)
    max_exact = REL_BUCKETS // 2
    nf = np.maximum(n, 1).astype(np.float32)
    large = max_exact + (np.log(nf / np.float32(max_exact)) / np.float32(math.log(REL_MAX_DIST / max_exact))
                         * np.float32(REL_BUCKETS - max_exact)).astype(np.int32)
    large = np.minimum(large, REL_BUCKETS - 1)
    return np.where(n < max_exact, n, large).astype(np.int32)


def _band_body(bkt_ref, rb_ref, o_ref):
    h = pl.program_id(1)
    bkt = bkt_ref[0]
    acc = jnp.zeros(bkt.shape, F32)
    for b in range(REL_BUCKETS):
        acc = jnp.where(bkt == b, rb_ref[b, h], acc)
    o_ref[0] = acc * LOG2E


def _band_bias(rel_bias):
    tt = ATT_T
    kc = np.arange(tt)[:, None]
    qr = np.arange(tt)[None, :]
    planes = [_t5_bucket_np(d * tt + qr - kc) for d in range(3)]
    assert (planes[2] == REL_BUCKETS - 1).all() and tt + 1 >= 113
    bkt = jnp.asarray(np.stack(planes))
    return pl.pallas_call(
        _band_body,
        grid=(3, ATT_HEADS),
        in_specs=[pl.BlockSpec((1, tt, tt), lambda d, h: (d, 0, 0)),
                  pl.BlockSpec(memory_space=pltpu.SMEM)],
        out_specs=pl.BlockSpec((1, tt, tt), lambda d, h: (d, 0, h)),
        out_shape=jax.ShapeDtypeStruct((3, tt, ATT_HEADS * tt), F32),
        compiler_params=_cparams(("parallel", "parallel")),
        name="band_bias",
    )(bkt, rel_bias)


LOG2E = 1.4426950408889634
ACC_ROWS = KV_LATENT + 16


def _dsa_attn_body(qt_ref, kt_ref, q_ref, ckv_ref, mask_ref, wuk_ref, wuv_ref, near_ref, far_ref,
                   x_ref, mod_ref, wout_ref, lng_ref, lnb_ref, o_ref,
                   ql_ref, m_ref, sm_ref, ot_ref, *acc_refs):
    i = qt_ref[pl.program_id(1)]
    kp = kt_ref[pl.program_id(1)]
    gap = i - 2 * kp
    nh = ATT_HEADS
    tq = q_ref.shape[1]
    tk = tq
    qscale = ATT_HEAD_DIM ** -0.5 * LOG2E

    @pl.when(kp == 0)
    def _():
        for hp in range(nh // 2):
            qp = q_ref[0, :, hp * LANES:(hp + 1) * LANES].astype(BF16)
            qlat = _mm(qp, wuk_ref[hp]) * qscale
            ql_ref[2 * hp * tq:(2 * hp + 1) * tq, :] = qlat[:, 0:KV_LATENT].astype(BF16)
            ql_ref[(2 * hp + 1) * tq:(2 * hp + 2) * tq, :] = qlat[:, KV_LATENT:2 * KV_LATENT].astype(BF16)
        m_ref[...] = jnp.full(m_ref.shape, MASK_NEG, F32)
        for acc_ref in acc_refs:
            acc_ref[...] = jnp.zeros_like(acc_ref)

    def step(*kinds):
        nk = len(kinds) * tk
        ckv = ckv_ref[0, 0:nk, :]
        ckv_aug = jnp.concatenate([ckv.astype(F32).T, jnp.ones((ACC_ROWS - KV_LATENT, nk), F32)],
                                  axis=0).astype(BF16)
        maskb = mask_ref[0, 0:nk, :].astype(F32)
        m_prev = m_ref[...]
        m_news = []
        for h in range(nh):
            hs = slice(h * tq, (h + 1) * tq)
            s = _mm(ckv, ql_ref[hs, :], _NT)
            m_new = m_prev[:, hs]
            for n, plane in enumerate(kinds):
                rows = slice(n * tk, (n + 1) * tk)
                sm = s[rows] + maskb[rows]
                if plane is None:
                    m_new = jnp.maximum(m_new, jnp.max(sm, axis=0, keepdims=True) + far_ref[:, hs])
                else:
                    sm = sm + near_ref[plane, :, hs]
                    m_new = jnp.maximum(m_new, jnp.max(sm, axis=0, keepdims=True))
                sm_ref[rows, hs] = sm
            m_news.append(m_new)
        for h in range(nh):
            hs = slice(h * tq, (h + 1) * tq)
            m_new = m_news[h]
            p = jnp.concatenate(
                [jnp.exp2(sm_ref[n * tk:(n + 1) * tk, hs]
                          - (m_new - far_ref[:, hs] if plane is None else m_new)).astype(BF16)
                 for n, plane in enumerate(kinds)], axis=0)
            alpha = jnp.exp2(m_prev[:, hs] - m_new)
            acc_refs[h][...] = alpha * acc_refs[h][...] + _mm(ckv_aug, p)
        m_ref[...] = jnp.concatenate(m_news, axis=1)

    @pl.when(gap >= 3)
    def _():
        step(None, None)

    @pl.when(gap == 2)
    def _():
        step(None, 1)

    @pl.when(gap == 1)
    def _():
        step(1, 0)

    @pl.when(gap == 0)
    def _():
        step(0)

    @pl.when(gap <= 1)
    def _():
        def norm(h):
            a = acc_refs[h][...]
            return a[0:KV_LATENT] * (1.0 / a[KV_LATENT:KV_LATENT + 1])
        for hp in range(nh // 2):
            olat = jnp.concatenate([norm(2 * hp), norm(2 * hp + 1)], axis=0).astype(BF16)
            ot_ref[hp * LANES:(hp + 1) * LANES, :] = _mm(wuv_ref[hp], olat)
        y = _mm(ot_ref[...].T.astype(BF16), wout_ref[...])
        res = DEEPNORM_ALPHA * x_ref[0] + (1.0 + mod_ref[0, 2:3, :]) * y
        o_ref[0] = _layernorm(res, lng_ref[...], lnb_ref[...])


def _dsa_attn(q, ckv, maskt, w_uk, w_uv, band, x, mod, w_out, ln_g, ln_b):
    bsz, t, d = q.shape
    tt = ATT_T
    nt = t // tt
    nh = ATT_HEADS
    zk = jnp.zeros((nh // 2, ATT_HEAD_DIM, KV_LATENT), F32)
    wuk2 = jnp.concatenate([jnp.concatenate([w_uk[0::2], zk], axis=2),
                            jnp.concatenate([zk, w_uk[1::2]], axis=2)], axis=1).astype(BF16)
    wuv_t = jnp.swapaxes(w_uv, 1, 2)
    zv = jnp.zeros((nh // 2, ATT_HEAD_DIM, KV_LATENT), F32)
    wuv2 = jnp.concatenate([jnp.concatenate([wuv_t[0::2], zv], axis=2),
                            jnp.concatenate([zv, wuv_t[1::2]], axis=2)], axis=1).astype(BF16)
    assert nt % 2 == 0
    steps = [(i, kp) for i in range(nt) for kp in range(i // 2 + 1)]
    q_tab = jnp.asarray([p[0] for p in steps], jnp.int32)
    k_tab = jnp.asarray([p[1] for p in steps], jnp.int32)
    grid_spec = pltpu.PrefetchScalarGridSpec(
        num_scalar_prefetch=2,
        grid=(bsz, len(steps)),
        in_specs=[pl.BlockSpec((1, tt, d), lambda b, s, qt, kt: (b, qt[s], 0)),
                  pl.BlockSpec((1, 2 * tt, KV_LATENT), lambda b, s, qt, kt: (b, kt[s], 0)),
                  pl.BlockSpec((1, 2 * tt, tt), lambda b, s, qt, kt: (b, kt[s], qt[s])),
                  pl.BlockSpec((nh // 2, LANES, 2 * KV_LATENT), lambda b, s, qt, kt: (0, 0, 0)),
                  pl.BlockSpec((nh // 2, LANES, 2 * KV_LATENT), lambda b, s, qt, kt: (0, 0, 0)),
                  pl.BlockSpec((2, tt, nh * tt), lambda b, s, qt, kt: (0, 0, 0)),
                  pl.BlockSpec((1, nh * tt), lambda b, s, qt, kt: (0, 0)),
                  pl.BlockSpec((1, tt, d), lambda b, s, qt, kt: (b, qt[s], 0)),
                  pl.BlockSpec((1, 6, d), lambda b, s, qt, kt: (b, 0, 0)),
                  pl.BlockSpec((d, d), lambda b, s, qt, kt: (0, 0)),
                  pl.BlockSpec((1, d), lambda b, s, qt, kt: (0, 0)),
                  pl.BlockSpec((1, d), lambda b, s, qt, kt: (0, 0))],
        out_specs=pl.BlockSpec((1, tt, d), lambda b, s, qt, kt: (b, qt[s], 0)),
        scratch_shapes=[pltpu.VMEM((nh * tt, KV_LATENT), BF16),
                        pltpu.VMEM((1, nh * tt), F32),
                        pltpu.VMEM((2 * tt, nh * tt), F32),
                        pltpu.VMEM((d, tt), F32)]
        + [pltpu.VMEM((ACC_ROWS, tt), F32)] * nh)
    return pl.pallas_call(
        _dsa_attn_body,
        grid_spec=grid_spec,
        out_shape=jax.ShapeDtypeStruct((bsz, t, d), F32),
        compiler_params=_cparams(("parallel", "arbitrary")),
        name="dsa_attn",
    )(q_tab, k_tab, q, ckv, maskt, wuk2, wuv2, band[:2], band[2, 0:1, :],
      x, mod, w_out, ln_g.reshape(1, d), ln_b.reshape(1, d))


def kernel(x, c, ada_w, ada_b, ln_g, ln_b, ffn_w_in, ffn_w_out, rwkv_mu, rwkv_w_rkv, rwkv_w0, rwkv_w1, rwkv_w2, rwkv_a0, rwkv_a1, rwkv_a2, rwkv_v0, rwkv_v1, rwkv_v2, rwkv_g1, rwkv_g2, rwkv_k_k, rwkv_k_a, rwkv_r_k, rwkv_lnx_g, rwkv_lnx_b, rwkv_w_out, dsa_w_in, dsa_kv_norm, dsa_w_uk, dsa_w_uv, dsa_w_out, rel_bias):
    bsz, t, d = x.shape
    assert d == D_MODEL and t % (2 * ATT_T) == 0 and t % FFN_TM == 0 and t % PRE_TM == 0 and bsz % SCAN_NB == 0
    mod_all = _adaln(c, ada_w, ada_b).reshape(DEPTH, bsz, 6, d)
    band = _band_bias(rel_bias)
    k_sel = min(TOPK_MAX, t // TOPK_DIV)
    bf = lambda w: w.astype(BF16)
    v_first = None
    for i in range(DEPTH):
        mod = mod_all[i]
        j = i // 2
        if i % 2 == 0:
            p = dict(mu=rwkv_mu[j], wr=bf(rwkv_w_rkv[j, 0]), wk=bf(rwkv_w_rkv[j, 1]), wv=bf(rwkv_w_rkv[j, 2]),
                     w0=rwkv_w0[j], w1=bf(rwkv_w1[j]), w2=bf(rwkv_w2[j]),
                     a0=rwkv_a0[j], a1=bf(rwkv_a1[j]), a2=bf(rwkv_a2[j]),
                     g1=bf(rwkv_g1[j]), g2=bf(rwkv_g2[j]), k_k=rwkv_k_k[j], k_a=rwkv_k_a[j])
            if j > 0:
                p.update(v0=rwkv_v0[j - 1], v1=bf(rwkv_v1[j - 1]), v2=bf(rwkv_v2[j - 1]))
            r_s, k_s, a_s, b_s, v, g, gl = _rwkv_pre(x, mod, p, v_first if j > 0 else None)
            if j == 0:
                v_first = v
            z = _rwkv_scan(r_s, k_s, a_s, b_s, v, gl, rwkv_r_k[j], rwkv_lnx_g[j], rwkv_lnx_b[j])
            tail = (z, g, bf(rwkv_w_out[j]), ln_g[i, 0], ln_b[i, 0])
        else:
            tail = None
            q, ckv, qi, ki, wi = _dsa_proj(x, mod, dsa_w_in[j], dsa_kv_norm[j])
            maskt = _dsa_index(qi, ki, wi, k_sel)
            x = _dsa_attn(q, ckv, maskt, dsa_w_uk[j], dsa_w_uv[j], band,
                          x, mod, bf(dsa_w_out[j]), ln_g[i, 0], ln_b[i, 0])
        x = _ffn(x, mod, bf(ffn_w_in[i]), bf(ffn_w_out[i]), ln_g[i, 1], ln_b[i, 1], tail)
    return x
```

```python
import functools
import math

import numpy as np
import jax
import jax.numpy as jnp
from jax import lax
from jax.experimental import pallas as pl
from jax.experimental.pallas import tpu as pltpu

F32 = jnp.float32
BF16 = jnp.bfloat16

D_MODEL = 1024
DEPTH = 4
RWKV_HEAD = 64
RWKV_HEADS = D_MODEL // RWKV_HEAD
GN_EPS = RWKV_HEAD * 1e-5
ATT_HEADS = 16
ATT_HEAD_DIM = 64
KV_LATENT = 128
IDX_HEADS = 8
IDX_DIM = 64
TOPK_MAX = 256
TOPK_DIV = 4
REL_BUCKETS = 32
REL_MAX_DIST = 128
DEEPNORM_ALPHA = (2 * DEPTH) ** 0.25
LN_EPS = 1e-5

LANES = 128
CHUNK = 128
SCAN_NB = 2
PRE_TM = 256
PROJ_TM = 512
FFN_TM = 512
FFN_TF = 2816
ATT_T = 256
MASK_NEG = -1e30
V7X_VMEM_BYTES = 64 * 1024 * 1024
VMEM_LIMIT = V7X_VMEM_BYTES - 8 * 1024 * 1024
INT_MIN = -2 ** 31


def _cparams(sem):
    return pltpu.CompilerParams(dimension_semantics=sem, vmem_limit_bytes=VMEM_LIMIT)


def _split2(x):
    hi = x.astype(BF16)
    lo = (x - hi.astype(F32)).astype(BF16)
    return hi, lo


def _split3(x):
    hi = x.astype(BF16)
    r1 = x - hi.astype(F32)
    mid = r1.astype(BF16)
    lo = (r1 - mid.astype(F32)).astype(BF16)
    return hi, mid, lo


_NN = (((1,), (0,)), ((), ()))
_NT = (((1,), (1,)), ((), ()))


def _mm(a, b, dims=_NN):
    return lax.dot_general(a, b, dims, preferred_element_type=F32)


def _dot3(a, b, dims=_NN):
    ah, al = _split2(a)
    bh, bl = _split2(b)
    return _mm(ah, bh, dims) + (_mm(ah, bl, dims) + _mm(al, bh, dims))


def _dot1(a, b, dims=_NN):
    return _mm(a.astype(BF16), b.astype(BF16), dims)


def _dot_hilo_rhs(a, b_bf16):
    h, l = _split2(a)
    return _mm(h, b_bf16) + _mm(l, b_bf16)


def _sigmoid(x):
    return 1.0 / (1.0 + jnp.exp(-x))


def _layernorm(xr, g, b):
    mu = jnp.mean(xr, axis=-1, keepdims=True)
    xc = xr - mu
    var = jnp.mean(xc * xc, axis=-1, keepdims=True)
    return xc * lax.rsqrt(var + LN_EPS) * g + b


def _adaln_body(c_ref, w_ref, b_ref, o_ref):
    c = c_ref[...]
    cond = c * _sigmoid(c)
    o_ref[0] = _dot3(cond, w_ref[0]) + b_ref[0]


def _adaln(c, ada_w, ada_b):
    depth, d, n = ada_w.shape
    bsz = c.shape[0]
    tn = n // 4
    return pl.pallas_call(
        _adaln_body,
        grid=(depth, n // tn),
        in_specs=[
            pl.BlockSpec((bsz, d), lambda i, j: (0, 0)),
            pl.BlockSpec((1, d, tn), lambda i, j: (i, 0, j)),
            pl.BlockSpec((1, 1, tn), lambda i, j: (i, 0, j)),
        ],
        out_specs=pl.BlockSpec((1, bsz, tn), lambda i, j: (i, 0, j)),
        out_shape=jax.ShapeDtypeStruct((depth, bsz, n), F32),
        compiler_params=_cparams(("parallel", "parallel")),
        name="adaln",
    )(c, ada_w, ada_b.reshape(depth, 1, n))


def _ffn_body(mixer_tail, *refs):
    if mixer_tail:
        (x_ref, mod_ref, wg_ref, wu_ref, wo_ref, lng_ref, lnb_ref,
         z_ref, g_ref, wm_ref, lng0_ref, lnb0_ref, o_ref, hin_ref, acc_ref, xm_ref) = refs
    else:
        x_ref, mod_ref, wg_ref, wu_ref, wo_ref, lng_ref, lnb_ref, o_ref, hin_ref, acc_ref = refs
    j = pl.program_id(2)

    @pl.when(j == 0)
    def _():
        x = x_ref[0]
        if mixer_tail:
            y = _mm((z_ref[0] * g_ref[0]).astype(BF16), wm_ref[...])
            x = _layernorm(DEEPNORM_ALPHA * x + (1.0 + mod_ref[0, 2:3, :]) * y, lng0_ref[...], lnb0_ref[...])
            xm_ref[...] = x
        hin_ref[...] = (x * (1.0 + mod_ref[0, 4:5, :]) + mod_ref[0, 3:4, :]).astype(BF16)
        acc_ref[...] = jnp.zeros_like(acc_ref)

    hin = hin_ref[...]
    gate = _mm(hin, wg_ref[...])
    up = _mm(hin, wu_ref[...])
    hid = (gate * _sigmoid(gate) * up).astype(BF16)
    acc_ref[...] += _mm(hid, wo_ref[...])

    @pl.when(j == pl.num_programs(2) - 1)
    def _():
        x = xm_ref[...] if mixer_tail else x_ref[0]
        res = DEEPNORM_ALPHA * x + (1.0 + mod_ref[0, 5:6, :]) * acc_ref[...]
        o_ref[0] = _layernorm(res, lng_ref[...], lnb_ref[...])


def _ffn(x, mod, w_in, w_out, ln_g, ln_b, mixer_tail=None):
    bsz, t, d = x.shape
    f = w_out.shape[0]
    nf = f // FFN_TF
    row = pl.BlockSpec((1, FFN_TM, d), lambda b, i, j: (b, i, 0))
    vec = pl.BlockSpec((1, d), lambda b, i, j: (0, 0))
    ins = [x, mod, w_in, w_in, w_out, ln_g.reshape(1, d), ln_b.reshape(1, d)]
    specs = [row,
             pl.BlockSpec((1, 6, d), lambda b, i, j: (b, 0, 0)),
             pl.BlockSpec((d, FFN_TF), lambda b, i, j: (0, j)),
             pl.BlockSpec((d, FFN_TF), lambda b, i, j: (0, nf + j)),
             pl.BlockSpec((FFN_TF, d), lambda b, i, j: (j, 0)),
             vec, vec]
    scratch = [pltpu.VMEM((FFN_TM, d), BF16), pltpu.VMEM((FFN_TM, d), F32)]
    if mixer_tail is not None:
        z, g, w_mix, ln_g0, ln_b0 = mixer_tail
        ins += [z, g, w_mix, ln_g0.reshape(1, d), ln_b0.reshape(1, d)]
        specs += [row, row, pl.BlockSpec((d, d), lambda b, i, j: (0, 0)), vec, vec]
        scratch += [pltpu.VMEM((FFN_TM, d), F32)]
    return pl.pallas_call(
        functools.partial(_ffn_body, mixer_tail is not None),
        grid=(bsz, t // FFN_TM, nf),
        in_specs=specs,
        out_specs=row,
        out_shape=jax.ShapeDtypeStruct((bsz, t, d), F32),
        scratch_shapes=scratch,
        compiler_params=_cparams(("parallel", "parallel", "arbitrary")),
        name="ffn",
    )(*ins)


def _rwkv_pre_body(has_vres, *refs):
    if has_vres:
        (x_ref, xp_ref, mod_ref, mu_ref, wr_ref, wk_ref, wv_ref, w1_ref, w2_ref, a1_ref, a2_ref,
         g1_ref, g2_ref, vec_ref, seg_ref, segt_ref, tri_ref, vf_ref, v1_ref, v2_ref,
         r_o, k_o, a_o, b_o, v_o, g_o, gl_o, mix_ref) = refs
    else:
        (x_ref, xp_ref, mod_ref, mu_ref, wr_ref, wk_ref, wv_ref, w1_ref, w2_ref, a1_ref, a2_ref,
         g1_ref, g2_ref, vec_ref, seg_ref, segt_ref, tri_ref,
         r_o, k_o, a_o, b_o, v_o, g_o, gl_o, mix_ref) = refs
    i = pl.program_id(1)
    sc = 1.0 + mod_ref[0, 1:2, :]
    sh = mod_ref[0, 0:1, :]
    hin = x_ref[0] * sc + sh
    tm, d = hin.shape
    prev_row = xp_ref[0, 7:8, :] * sc + sh
    prev_row = jnp.where(i == 0, 0.0, prev_row)
    rows = lax.broadcasted_iota(jnp.int32, hin.shape, 0)
    hprev = jnp.where(rows == 0, prev_row, pltpu.roll(hin, 1, 0))
    xx = hprev - hin
    for p in range(6):
        mix_ref[p] = (hin + xx * mu_ref[p:p + 1, :]).astype(BF16)
    lora_w = jnp.tanh(_mm(mix_ref[3], w1_ref[...])).astype(BF16)
    lora_a = _mm(mix_ref[4], a1_ref[...]).astype(BF16)
    lora_g = _sigmoid(_mm(mix_ref[5], g1_ref[...])).astype(BF16)
    if has_vres:
        lora_v = _mm(mix_ref[2], v1_ref[...]).astype(BF16)

    cw = 2 * LANES
    for cb in range(d // cw):
        cols = slice(cb * cw, (cb + 1) * cw)
        w0, a0, kkw, kaw = vec_ref[0:1, cols], vec_ref[1:2, cols], vec_ref[2:3, cols], vec_ref[3:4, cols]
        r = _mm(mix_ref[0], wr_ref[:, cols])
        k = _mm(mix_ref[1], wk_ref[:, cols])
        v = _mm(mix_ref[2], wv_ref[:, cols])

        wl = w0 + _mm(lora_w, w2_ref[:, cols])
        nz = -wl
        softplus = jnp.maximum(nz, 0.0) + jnp.log(1.0 + jnp.exp(-jnp.abs(nz)))
        logdec = -jnp.exp(-softplus - 0.5)

        a = _sigmoid(a0 + _mm(lora_a, a2_ref[:, cols]))
        if has_vres:
            vmix = _sigmoid(vec_ref[4:5, cols] + _mm(lora_v, v2_ref[:, cols]))
            v = v + (vf_ref[0, :, cols] - v) * vmix
        g = _mm(lora_g, g2_ref[:, cols])

        kk = k * kkw
        ss = _dot_hilo_rhs(kk * kk, seg_ref[...])
        nrm = jnp.maximum(jnp.sqrt(ss), 1e-12)
        kk = kk * _dot_hilo_rhs(1.0 / nrm, segt_ref[...])
        k = k * (1.0 + (a - 1.0) * kaw)

        cum = _dot_exact_rhs_lhs(tri_ref[...], logdec)
        ginc = jnp.exp(cum)
        ginv = jnp.exp(-cum)
        gprev = jnp.exp(cum - logdec)

        r_o[0, :, cols] = r * ginc
        k_o[0, :, cols] = k * ginv
        a_o[0, :, cols] = -kk * gprev
        b_o[0, :, cols] = kk * a * ginv
        v_o[0, :, cols] = v
        g_o[0, :, cols] = g
        for cc in range(tm // CHUNK):
            first, last = cc * CHUNK, cc * CHUNK + CHUNK - 1
            gl_o[0, cc, 0:1, cols] = jnp.exp(logdec[first:first + 1, :] - cum[first:first + 1, :])
            gl_o[0, cc, 1:2, cols] = ginc[last:last + 1, :]


def _dot_exact_rhs_lhs(m_bf16, x):
    h, mid, l = _split3(x)
    return _mm(m_bf16, h) + (_mm(m_bf16, mid) + _mm(m_bf16, l))


def _rwkv_pre(x, mod, p, v_first):
    bsz, t, d = x.shape
    tm = PRE_TM
    has_vres = v_first is not None
    row = pl.BlockSpec((1, tm, d), lambda b, i: (b, i, 0))

    def full(shape):
        return pl.BlockSpec(shape, lambda b, i: (0,) * len(shape))

    cw = 2 * LANES
    seg = np.zeros((cw, LANES), np.float32)
    seg[np.arange(cw), np.arange(cw) // RWKV_HEAD] = 1.0
    idx = np.arange(tm)
    same = idx[:, None] // CHUNK == idx[None, :] // CHUNK
    col = idx[None, :] % CHUNK
    tri = (same & (col <= idx[:, None] % CHUNK)).astype(np.float32) \
        - (same & (col <= CHUNK // 2 - 1)).astype(np.float32)
    vec_rows = [p['w0'], p['a0'], p['k_k'], p['k_a']] + ([p['v0']] if has_vres else [])
    vec = jnp.stack(vec_rows + [jnp.zeros_like(p['w0'])] * (8 - len(vec_rows)))
    dl, da, dg = p['w1'].shape[1], p['a1'].shape[1], p['g1'].shape[1]
    ins = [x, x, mod, p['mu'], p['wr'], p['wk'], p['wv'], p['w1'], p['w2'], p['a1'], p['a2'],
           p['g1'], p['g2'], vec, jnp.asarray(seg, BF16), jnp.asarray(seg.T, BF16),
           jnp.asarray(tri, BF16)]
    specs = [row,
             pl.BlockSpec((1, 8, d), lambda b, i: (b, jnp.maximum(i * (tm // 8) - 1, 0), 0)),
             pl.BlockSpec((1, 6, d), lambda b, i: (b, 0, 0)),
             full((6, d)), full((d, d)), full((d, d)), full((d, d)),
             full((d, dl)), full((dl, d)), full((d, da)), full((da, d)),
             full((d, dg)), full((dg, d)), full((8, d)),
             full((cw, LANES)), full((LANES, cw)), full((tm, tm))]
    if has_vres:
        dv = p['v1'].shape[1]
        ins += [v_first, p['v1'], p['v2']]
        specs += [row, full((d, dv)), full((dv, d))]
    act = jax.ShapeDtypeStruct((bsz, t, d), F32)
    nch = t // CHUNK
    outs = pl.pallas_call(
        functools.partial(_rwkv_pre_body, has_vres),
        grid=(bsz, t // tm),
        in_specs=specs,
        out_specs=[row] * 6 + [pl.BlockSpec((1, tm // CHUNK, 2, d), lambda b, i: (b, i, 0, 0))],
        out_shape=[act] * 6 + [jax.ShapeDtypeStruct((bsz, nch, 2, d), F32)],
        scratch_shapes=[pltpu.VMEM((6, tm, d), BF16)],
        compiler_params=_cparams(("parallel", "parallel")),
        name="rwkv_pre",
    )(*ins)
    return outs


_BNN = (((2,), (1,)), ((0,), (0,)))
_BNT = (((2,), (2,)), ((0,), (0,)))


def _rwkv_scan_body(r_ref, k_ref, a_ref, b_ref, v_ref, gl_ref, vec_ref, o_ref, s_ref):
    c = pl.program_id(1)

    @pl.when(c == 0)
    def _():
        s_ref[...] = jnp.zeros_like(s_ref)

    L = CHUNK
    nb = r_ref.shape[0]
    pairs = [(bi, hp) for bi in range(nb) for hp in range(RWKV_HEADS // 2)]
    NP = len(pairs)
    shp = (NP, L, LANES)
    plane = lax.broadcasted_iota(jnp.int32, (1, L, 2 * L), 2)
    ph0 = plane < L
    tt = lax.broadcasted_iota(jnp.int32, (1, L, 2 * L), 1)
    ss = plane & (L - 1)
    strict = ss < tt
    incl = ss <= tt
    eye = jnp.where(ss == tt, 1.0, 0.0)
    base_bits = 4

    def same_block(bits):
        return (ss >> bits) == (tt >> bits)

    lvl0 = strict & same_block(base_bits)
    merges = [strict & same_block(bits + 1) & ((ss >> bits) != (tt >> bits))
              for bits in range(base_bits, (L - 1).bit_length())]
    dh0 = lax.broadcasted_iota(jnp.int32, (1, L, LANES), 2) < RWKV_HEAD
    ri = lax.broadcasted_iota(jnp.int32, (1, LANES, LANES), 1)
    ci = lax.broadcasted_iota(jnp.int32, (1, LANES, LANES), 2)
    blockdiag = (ri >> 6) == (ci >> 6)
    seg_mean = jnp.where(blockdiag[0], 1.0, 0.0).astype(BF16)

    def grp(ref):
        return jnp.stack([ref[bi, :, hp * LANES:(hp + 1) * LANES] for bi, hp in pairs])

    def bd(x):
        return jnp.concatenate([jnp.where(dh0, x, 0.0), jnp.where(dh0, 0.0, x)], axis=1)

    def bdp(y):
        return jnp.concatenate([jnp.where(ph0, y, 0.0), jnp.where(ph0, 0.0, y)], axis=1)

    def pmm(xp, y):
        return _dot1(xp, bd(y), _BNN)

    def ppm(xp, yp):
        return _dot1(xp, bdp(yp), _BNN)

    R, K, A, Bv, V = grp(r_ref), grp(k_ref), grp(a_ref), grp(b_ref), grp(v_ref)
    gvec = lambda n: jnp.stack([gl_ref[bi, 0, n:n + 1, hp * LANES:(hp + 1) * LANES] for bi, hp in pairs])
    S = s_ref[...] * gvec(0)

    AR = jnp.concatenate([A, R], axis=1)
    BK = jnp.concatenate([bd(Bv), bd(K)], axis=1)
    G = _dot1(AR, BK, _BNT)
    A_ab = jnp.where(strict, G[:, 0:L, 0:2 * L], 0.0)
    A_ak = jnp.where(strict, G[:, 0:L, 2 * L:4 * L], 0.0)
    A_rb = jnp.where(incl, G[:, L:2 * L, 0:2 * L], 0.0)
    A_rk = jnp.where(incl, G[:, L:2 * L, 2 * L:4 * L], 0.0)

    a0 = jnp.where(lvl0, A_ab, 0.0)
    Tm = eye + a0
    P = a0
    for _ in range(base_bits - 1):
        P = ppm(P, P)
        Tm = Tm + ppm(Tm, P)
    for lvl in merges:
        Tm = Tm + ppm(ppm(Tm, jnp.where(lvl, A_ab, 0.0)), Tm)

    PQ = _dot1(AR, S, _BNT)
    W = PQ[:, 0:L] + pmm(A_ak, V)
    U = pmm(Tm, W)
    Y = PQ[:, L:2 * L] + _dot1(jnp.concatenate([A_rb, A_rk], axis=2),
                               jnp.concatenate([bd(U), bd(V)], axis=1), _BNN)
    UV = jnp.concatenate([U, V], axis=1)
    UVt = jnp.stack([UV[hp].T for hp in range(NP)])
    BK2 = jnp.concatenate([Bv, K], axis=1)
    upd = _dot1(UVt, BK2, _BNN)
    s_ref[...] = (S + jnp.where(blockdiag, upd, 0.0)) * gvec(1)

    vrow = lambda n: jnp.stack([vec_ref[n:n + 1, hp * LANES:(hp + 1) * LANES] for _, hp in pairs])
    inv_n = 1.0 / RWKV_HEAD
    flat = lambda x: x.reshape(NP * L, LANES)
    mean = _dot_hilo_rhs(flat(Y), seg_mean).reshape(shp) * inv_n
    yc = Y - mean
    var = _dot_hilo_rhs(flat(yc * yc), seg_mean).reshape(shp) * inv_n
    yn = yc * lax.rsqrt(var + GN_EPS) * vrow(1) + vrow(2)
    bonus = _dot_hilo_rhs(flat(R * K * vrow(0)), seg_mean).reshape(shp)
    out = yn + bonus * V
    for n, (bi, hp) in enumerate(pairs):
        o_ref[bi, :, hp * LANES:(hp + 1) * LANES] = out[n]


def _rwkv_scan(r, k, a, b, v, gl, r_k, lnx_g, lnx_b):
    bsz, t, d = r.shape
    nb = SCAN_NB
    row = pl.BlockSpec((nb, CHUNK, d), lambda bb, c: (bb, c, 0))
    vec = jnp.stack([r_k.reshape(d), lnx_g, lnx_b] + [jnp.zeros((d,), F32)] * 5)
    return pl.pallas_call(
        _rwkv_scan_body,
        grid=(bsz // nb, t // CHUNK),
        in_specs=[row] * 5 + [
            pl.BlockSpec((nb, 1, 2, d), lambda bb, c: (bb, c, 0, 0)),
            pl.BlockSpec((8, d), lambda bb, c: (0, 0)),
        ],
        out_specs=row,
        out_shape=jax.ShapeDtypeStruct((bsz, t, d), F32),
        scratch_shapes=[pltpu.VMEM((nb * RWKV_HEADS // 2, LANES, LANES), F32)],
        compiler_params=_cparams(("parallel", "arbitrary")),
        name="rwkv_scan",
    )(r, k, a, b, v, gl, vec)


IDX_COLS = 768


def _dsa_proj_body(x_ref, mod_ref, wq_ref, wc_ref, wi_ref, kvn_ref,
                   q_o, ckv_o, qi_o, ki_o, wi_o):
    hin = (x_ref[0] * (1.0 + mod_ref[0, 1:2, :]) + mod_ref[0, 0:1, :]).astype(BF16)
    q_o[0] = _mm(hin, wq_ref[...]).astype(BF16)
    ckv = _mm(hin, wc_ref[...])
    ms = jnp.mean(ckv * ckv, axis=-1, keepdims=True)
    ckv_o[0] = (ckv * lax.rsqrt(ms + 1e-6) * kvn_ref[...]).astype(BF16)
    idx = _mm(hin, wi_ref[...])
    nq = IDX_HEADS * IDX_DIM
    qi_o[0] = idx[:, 0:nq].astype(BF16)
    ki_o[0] = idx[:, nq:nq + LANES].astype(BF16)
    wi_o[0] = idx[:, nq + LANES:nq + 2 * LANES] * (IDX_HEADS ** -0.5 * IDX_DIM ** -0.5)


def _dsa_proj(x, mod, w_in, kv_norm):
    bsz, t, d = x.shape
    c1 = ATT_HEADS * ATT_HEAD_DIM
    c2 = c1 + KV_LATENT
    c3 = c2 + IDX_HEADS * IDX_DIM
    c4 = c3 + IDX_DIM
    wq = w_in[:, :c1].astype(BF16)
    wc = w_in[:, c1:c2].astype(BF16)
    pad = IDX_COLS - (c3 - c2) - 2 * IDX_DIM - IDX_HEADS
    widx = jnp.concatenate([w_in[:, c2:c3], w_in[:, c3:c4], w_in[:, c3:c4], w_in[:, c4:],
                            jnp.zeros((d, pad), F32)], axis=1).astype(BF16)
    tm = PROJ_TM

    def full(shape):
        return pl.BlockSpec(shape, lambda b, i: (0,) * len(shape))

    def row(n):
        return pl.BlockSpec((1, tm, n), lambda b, i: (b, i, 0))

    def act(n, dtype):
        return jax.ShapeDtypeStruct((bsz, t, n), dtype)

    nq = IDX_HEADS * IDX_DIM
    return pl.pallas_call(
        _dsa_proj_body,
        grid=(bsz, t // tm),
        in_specs=[row(d), pl.BlockSpec((1, 6, d), lambda b, i: (b, 0, 0)),
                  full((d, c1)), full((d, KV_LATENT)), full((d, IDX_COLS)),
                  full((1, KV_LATENT))],
        out_specs=[row(c1), row(KV_LATENT), row(nq), row(LANES), row(LANES)],
        out_shape=[act(c1, BF16), act(KV_LATENT, BF16), act(nq, BF16), act(LANES, BF16), act(LANES, F32)],
        compiler_params=_cparams(("parallel", "parallel")),
        name="dsa_proj",
    )(x, mod, wq, wc, widx, kv_norm.reshape(1, KV_LATENT))


def _dsa_index_body(k_sel, qi_ref, wi_ref, ki_ref, o_ref, key_ref):
    i = pl.program_id(1)
    tq = qi_ref.shape[1]
    t = ki_ref.shape[1]
    nchunk = t // tq
    wt = wi_ref[0].T
    lane = lax.broadcasted_iota(jnp.int32, (tq, LANES), 1)
    first = lane < IDX_DIM
    qheads = []
    for hp in range(IDX_HEADS // 2):
        qp = qi_ref[0, :, hp * LANES:(hp + 1) * LANES].astype(F32)
        qheads.append(jnp.where(first, qp, 0.0).astype(BF16))
        qheads.append(jnp.where(first, 0.0, qp).astype(BF16))
    q_all = jnp.concatenate(qheads, axis=0)
    krow = lax.broadcasted_iota(jnp.int32, (tq, tq), 0)
    qlane = lax.broadcasted_iota(jnp.int32, (tq, tq), 1)

    kf = float(k_sel)
    nbits = int(t - 1).bit_length()

    def colsum(ind):
        return jnp.sum(jnp.sum(ind.reshape(tq // 32, 4, 8, tq), axis=0), axis=0)

    def tile(nc):
        chunks = [slice(c * tq, (c + 1) * tq) for c in range(nc)]
        diag = krow <= qlane

        if nc * tq <= k_sel:
            for c, rows in enumerate(chunks):
                keep = diag if c == nc - 1 else (krow >= 0)
                o_ref[0, rows, :] = jnp.where(keep, 0.0, MASK_NEG).astype(BF16)
            for c in range(nc, nchunk):
                o_ref[0, c * tq:(c + 1) * tq, :] = jnp.full((tq, tq), MASK_NEG, BF16)
            return

        for c, rows in enumerate(chunks):
            kk2 = ki_ref[0, rows, :]
            s_all = _mm(kk2, q_all, _NT)
            score = jnp.zeros((tq, tq), F32)
            for h in range(IDX_HEADS):
                score = score + wt[h:h + 1, :] * jnp.maximum(s_all[:, h * tq:(h + 1) * tq], 0.0)
            score = jnp.where(score == 0.0, 0.0, score)
            bits = pltpu.bitcast(score, jnp.int32)
            skey = bits ^ ((bits >> 31) & 0x7FFFFFFF)
            key_ref[rows, :] = jnp.where(diag, skey, INT_MIN) if c == nc - 1 else skey

        def count(fn):
            acc = jnp.zeros((8, tq), F32)
            for c, rows in enumerate(chunks):
                acc = acc + colsum(fn(key_ref[rows, :], c * tq + krow))
            return jnp.sum(acc, axis=0, keepdims=True)

        def count_ge(cand):
            return count(lambda keys, kpos: jnp.where(keys >= cand, 1.0, 0.0))

        thr0 = jnp.where(count_ge(jnp.zeros((1, tq), jnp.int32)) >= kf, 0, INT_MIN).astype(jnp.int32)

        def thr_step(n, thr):
            cand = thr | jnp.left_shift(jnp.int32(1), 30 - n)
            return jnp.where(count_ge(cand) >= kf, cand, thr)

        thr = lax.fori_loop(0, 31, thr_step, thr0)
        n_gt = count(lambda keys, kpos: jnp.where(keys > thr, 1.0, 0.0))
        n_eq = count(lambda keys, kpos: jnp.where(keys == thr, 1.0, 0.0))
        need = kf - n_gt

        def tie_cut():
            def cut_step(n, cut):
                cand = cut | jnp.left_shift(jnp.int32(1), nbits - 1 - n)
                cnt = count(lambda keys, kpos: jnp.where(keys == thr, jnp.where(kpos < cand, 1.0, 0.0), 0.0))
                return jnp.where(cnt < need, cand, cut)
            return lax.fori_loop(0, nbits, cut_step, jnp.zeros((1, tq), jnp.int32))

        cut = lax.cond(jnp.max(n_eq - need) > 0.0, tie_cut, lambda: jnp.full((1, tq), t, jnp.int32))

        for c, rows in enumerate(chunks):
            keys = key_ref[rows, :]
            tie = jnp.where(keys == thr, jnp.where(c * tq + krow <= cut, 0.0, MASK_NEG), MASK_NEG)
            bias = jnp.where(keys > thr, 0.0, tie)
            if c == nc - 1:
                bias = jnp.where(diag, bias, MASK_NEG)
            o_ref[0, rows, :] = bias.astype(BF16)
        for c in range(nc, nchunk):
            o_ref[0, c * tq:(c + 1) * tq, :] = jnp.full((tq, tq), MASK_NEG, BF16)

    for nc in range(1, nchunk + 1):
        pl.when(i == nc - 1)(functools.partial(tile, nc))


def _dsa_index(qi, ki, wi, k_sel):
    bsz, t, nq = qi.shape
    tq = ATT_T
    return pl.pallas_call(
        functools.partial(_dsa_index_body, k_sel),
        grid=(bsz, t // tq),
        in_specs=[pl.BlockSpec((1, tq, nq), lambda b, i: (b, i, 0)),
                  pl.BlockSpec((1, tq, LANES), lambda b, i: (b, i, 0)),
                  pl.BlockSpec((1, t, LANES), lambda b, i: (b, 0, 0))],
        out_specs=pl.BlockSpec((1, t, tq), lambda b, i: (b, 0, i)),
        out_shape=jax.ShapeDtypeStruct((bsz, t, t), BF16),
        scratch_shapes=[pltpu.VMEM((t, tq), jnp.int32)],
        compiler_params=_cparams(("parallel", "parallel")),
        name="dsa_index",
    )(qi, wi, ki)


def _t5_bucket_np(n):
    n = np.maximum(n, 0)
    max_exact = REL_BUCKETS // 2
    nf = np.maximum(n, 1).astype(np.float32)
    large = max_exact + (np.log(nf / np.float32(max_exact)) / np.float32(math.log(REL_MAX_DIST / max_exact))
                         * np.float32(REL_BUCKETS - max_exact)).astype(np.int32)
    large = np.minimum(large, REL_BUCKETS - 1)
    return np.where(n < max_exact, n, large).astype(np.int32)


def _band_body(bkt_ref, rb_ref, o_ref):
    h = pl.program_id(1)
    bkt = bkt_ref[0]
    acc = jnp.zeros(bkt.shape, F32)
    for b in range(REL_BUCKETS):
        acc = jnp.where(bkt == b, rb_ref[b, h], acc)
    o_ref[0] = acc * LOG2E


def _band_bias(rel_bias):
    tt = ATT_T
    kc = np.arange(tt)[:, None]
    qr = np.arange(tt)[None, :]
    planes = [_t5_bucket_np(d * tt + qr - kc) for d in range(3)]
    assert (planes[2] == REL_BUCKETS - 1).all() and tt + 1 >= 113
    bkt = jnp.asarray(np.stack(planes))
    return pl.pallas_call(
        _band_body,
        grid=(3, ATT_HEADS),
        in_specs=[pl.BlockSpec((1, tt, tt), lambda d, h: (d, 0, 0)),
                  pl.BlockSpec(memory_space=pltpu.SMEM)],
        out_specs=pl.BlockSpec((1, tt, tt), lambda d, h: (d, 0, h)),
        out_shape=jax.ShapeDtypeStruct((3, tt, ATT_HEADS * tt), F32),
        compiler_params=_cparams(("parallel", "parallel")),
        name="band_bias",
    )(bkt, rel_bias)


LOG2E = 1.4426950408889634
ACC_ROWS = KV_LATENT + 16


def _dsa_attn_body(qt_ref, kt_ref, q_ref, ckv_ref, mask_ref, wuk_ref, wuv_ref, near_ref, far_ref,
                   x_ref, mod_ref, wout_ref, lng_ref, lnb_ref, o_ref,
                   ql_ref, m_ref, sm_ref, ot_ref, *acc_refs):
    i = qt_ref[pl.program_id(1)]
    kp = kt_ref[pl.program_id(1)]
    gap = i - 2 * kp
    nh = ATT_HEADS
    tq = q_ref.shape[1]
    tk = tq
    qscale = ATT_HEAD_DIM ** -0.5 * LOG2E

    @pl.when(kp == 0)
    def _():
        for hp in range(nh // 2):
            qp = q_ref[0, :, hp * LANES:(hp + 1) * LANES].astype(BF16)
            qlat = _mm(qp, wuk_ref[hp]) * qscale
            ql_ref[2 * hp * tq:(2 * hp + 1) * tq, :] = qlat[:, 0:KV_LATENT].astype(BF16)
            ql_ref[(2 * hp + 1) * tq:(2 * hp + 2) * tq, :] = qlat[:, KV_LATENT:2 * KV_LATENT].astype(BF16)
        m_ref[...] = jnp.full(m_ref.shape, MASK_NEG, F32)
        for acc_ref in acc_refs:
            acc_ref[...] = jnp.zeros_like(acc_ref)

    def step(*kinds):
        nk = len(kinds) * tk
        ckv = ckv_ref[0, 0:nk, :]
        ckv_aug = jnp.concatenate([ckv.astype(F32).T, jnp.ones((ACC_ROWS - KV_LATENT, nk), F32)],
                                  axis=0).astype(BF16)
        maskb = mask_ref[0, 0:nk, :].astype(F32)
        m_prev = m_ref[...]
        m_news = []
        for h in range(nh):
            hs = slice(h * tq, (h + 1) * tq)
            s = _mm(ckv, ql_ref[hs, :], _NT)
            m_new = m_prev[:, hs]
            for n, plane in enumerate(kinds):
                rows = slice(n * tk, (n + 1) * tk)
                sm = s[rows] + maskb[rows]
                if plane is None:
                    m_new = jnp.maximum(m_new, jnp.max(sm, axis=0, keepdims=True) + far_ref[:, hs])
                else:
                    sm = sm + near_ref[plane, :, hs]
                    m_new = jnp.maximum(m_new, jnp.max(sm, axis=0, keepdims=True))
                sm_ref[rows, hs] = sm
            m_news.append(m_new)
        for h in range(nh):
            hs = slice(h * tq, (h + 1) * tq)
            m_new = m_news[h]
            p = jnp.concatenate(
                [jnp.exp2(sm_ref[n * tk:(n + 1) * tk, hs]
                          - (m_new - far_ref[:, hs] if plane is None else m_new)).astype(BF16)
                 for n, plane in enumerate(kinds)], axis=0)
            alpha = jnp.exp2(m_prev[:, hs] - m_new)
            acc_refs[h][...] = alpha * acc_refs[h][...] + _mm(ckv_aug, p)
        m_ref[...] = jnp.concatenate(m_news, axis=1)

    @pl.when(gap >= 3)
    def _():
        step(None, None)

    @pl.when(gap == 2)
    def _():
        step(None, 1)

    @pl.when(gap == 1)
    def _():
        step(1, 0)

    @pl.when(gap == 0)
    def _():
        step(0)

    @pl.when(gap <= 1)
    def _():
        def norm(h):
            a = acc_refs[h][...]
            return a[0:KV_LATENT] * (1.0 / a[KV_LATENT:KV_LATENT + 1])
        for hp in range(nh // 2):
            olat = jnp.concatenate([norm(2 * hp), norm(2 * hp + 1)], axis=0).astype(BF16)
            ot_ref[hp * LANES:(hp + 1) * LANES, :] = _mm(wuv_ref[hp], olat)
        y = _mm(ot_ref[...].T.astype(BF16), wout_ref[...])
        res = DEEPNORM_ALPHA * x_ref[0] + (1.0 + mod_ref[0, 2:3, :]) * y
        o_ref[0] = _layernorm(res, lng_ref[...], lnb_ref[...])


def _dsa_attn(q, ckv, maskt, w_uk, w_uv, band, x, mod, w_out, ln_g, ln_b):
    bsz, t, d = q.shape
    tt = ATT_T
    nt = t // tt
    nh = ATT_HEADS
    zk = jnp.zeros((nh // 2, ATT_HEAD_DIM, KV_LATENT), F32)
    wuk2 = jnp.concatenate([jnp.concatenate([w_uk[0::2], zk], axis=2),
                            jnp.concatenate([zk, w_uk[1::2]], axis=2)], axis=1).astype(BF16)
    wuv_t = jnp.swapaxes(w_uv, 1, 2)
    zv = jnp.zeros((nh // 2, ATT_HEAD_DIM, KV_LATENT), F32)
    wuv2 = jnp.concatenate([jnp.concatenate([wuv_t[0::2], zv], axis=2),
                            jnp.concatenate([zv, wuv_t[1::2]], axis=2)], axis=1).astype(BF16)
    assert nt % 2 == 0
    steps = [(i, kp) for i in range(nt) for kp in range(i // 2 + 1)]
    q_tab = jnp.asarray([p[0] for p in steps], jnp.int32)
    k_tab = jnp.asarray([p[1] for p in steps], jnp.int32)
    grid_spec = pltpu.PrefetchScalarGridSpec(
        num_scalar_prefetch=2,
        grid=(bsz, len(steps)),
        in_specs=[pl.BlockSpec((1, tt, d), lambda b, s, qt, kt: (b, qt[s], 0)),
                  pl.BlockSpec((1, 2 * tt, KV_LATENT), lambda b, s, qt, kt: (b, kt[s], 0)),
                  pl.BlockSpec((1, 2 * tt, tt), lambda b, s, qt, kt: (b, kt[s], qt[s])),
                  pl.BlockSpec((nh // 2, LANES, 2 * KV_LATENT), lambda b, s, qt, kt: (0, 0, 0)),
                  pl.BlockSpec((nh // 2, LANES, 2 * KV_LATENT), lambda b, s, qt, kt: (0, 0, 0)),
                  pl.BlockSpec((2, tt, nh * tt), lambda b, s, qt, kt: (0, 0, 0)),
                  pl.BlockSpec((1, nh * tt), lambda b, s, qt, kt: (0, 0)),
                  pl.BlockSpec((1, tt, d), lambda b, s, qt, kt: (b, qt[s], 0)),
                  pl.BlockSpec((1, 6, d), lambda b, s, qt, kt: (b, 0, 0)),
                  pl.BlockSpec((d, d), lambda b, s, qt, kt: (0, 0)),
                  pl.BlockSpec((1, d), lambda b, s, qt, kt: (0, 0)),
                  pl.BlockSpec((1, d), lambda b, s, qt, kt: (0, 0))],
        out_specs=pl.BlockSpec((1, tt, d), lambda b, s, qt, kt: (b, qt[s], 0)),
        scratch_shapes=[pltpu.VMEM((nh * tt, KV_LATENT), BF16),
                        pltpu.VMEM((1, nh * tt), F32),
                        pltpu.VMEM((2 * tt, nh * tt), F32),
                        pltpu.VMEM((d, tt), F32)]
        + [pltpu.VMEM((ACC_ROWS, tt), F32)] * nh)
    return pl.pallas_call(
        _dsa_attn_body,
        grid_spec=grid_spec,
        out_shape=jax.ShapeDtypeStruct((bsz, t, d), F32),
        compiler_params=_cparams(("parallel", "arbitrary")),
        name="dsa_attn",
    )(q_tab, k_tab, q, ckv, maskt, wuk2, wuv2, band[:2], band[2, 0:1, :],
      x, mod, w_out, ln_g.reshape(1, d), ln_b.reshape(1, d))


def kernel(x, c, ada_w, ada_b, ln_g, ln_b, ffn_w_in, ffn_w_out, rwkv_mu, rwkv_w_rkv, rwkv_w0, rwkv_w1, rwkv_w2, rwkv_a0, rwkv_a1, rwkv_a2, rwkv_v0, rwkv_v1, rwkv_v2, rwkv_g1, rwkv_g2, rwkv_k_k, rwkv_k_a, rwkv_r_k, rwkv_lnx_g, rwkv_lnx_b, rwkv_w_out, dsa_w_in, dsa_kv_norm, dsa_w_uk, dsa_w_uv, dsa_w_out, rel_bias):
    bsz, t, d = x.shape
    assert d == D_MODEL and t % (2 * ATT_T) == 0 and t % FFN_TM == 0 and t % PRE_TM == 0 and bsz % SCAN_NB == 0
    mod_all = _adaln(c, ada_w, ada_b).reshape(DEPTH, bsz, 6, d)
    band = _band_bias(rel_bias)
    k_sel = min(TOPK_MAX, t // TOPK_DIV)
    bf = lambda w: w.astype(BF16)
    v_first = None
    for i in range(DEPTH):
        mod = mod_all[i]
        j = i // 2
        if i % 2 == 0:
            p = dict(mu=rwkv_mu[j], wr=bf(rwkv_w_rkv[j, 0]), wk=bf(rwkv_w_rkv[j, 1]), wv=bf(rwkv_w_rkv[j, 2]),
                     w0=rwkv_w0[j], w1=bf(rwkv_w1[j]), w2=bf(rwkv_w2[j]),
                     a0=rwkv_a0[j], a1=bf(rwkv_a1[j]), a2=bf(rwkv_a2[j]),
                     g1=bf(rwkv_g1[j]), g2=bf(rwkv_g2[j]), k_k=rwkv_k_k[j], k_a=rwkv_k_a[j])
            if j > 0:
                p.update(v0=rwkv_v0[j - 1], v1=bf(rwkv_v1[j - 1]), v2=bf(rwkv_v2[j - 1]))
            r_s, k_s, a_s, b_s, v, g, gl = _rwkv_pre(x, mod, p, v_first if j > 0 else None)
            if j == 0:
                v_first = v
            z = _rwkv_scan(r_s, k_s, a_s, b_s, v, gl, rwkv_r_k[j], rwkv_lnx_g[j], rwkv_lnx_b[j])
            tail = (z, g, bf(rwkv_w_out[j]), ln_g[i, 0], ln_b[i, 0])
        else:
            tail = None
            q, ckv, qi, ki, wi = _dsa_proj(x, mod, dsa_w_in[j], dsa_kv_norm[j])
            maskt = _dsa_index(qi, ki, wi, k_sel)
            x = _dsa_attn(q, ckv, maskt, dsa_w_uk[j], dsa_w_uv[j], band,
                          x, mod, bf(dsa_w_out[j]), ln_g[i, 0], ln_b[i, 0])
        x = _ffn(x, mod, bf(ffn_w_in[i]), bf(ffn_w_out[i]), ln_g[i, 1], ln_b[i, 1], tail)
    return x
```

```python
import functools
import math

import numpy as np
import jax
import jax.numpy as jnp
from jax import lax
from jax.experimental import pallas as pl
from jax.experimental.pallas import tpu as pltpu

F32 = jnp.float32
BF16 = jnp.bfloat16

D_MODEL = 1024
DEPTH = 4
RWKV_HEAD = 64
RWKV_HEADS = D_MODEL // RWKV_HEAD
GN_EPS = RWKV_HEAD * 1e-5
ATT_HEADS = 16
ATT_HEAD_DIM = 64
KV_LATENT = 128
IDX_HEADS = 8
IDX_DIM = 64
TOPK_MAX = 256
TOPK_DIV = 4
REL_BUCKETS = 32
REL_MAX_DIST = 128
DEEPNORM_ALPHA = (2 * DEPTH) ** 0.25
LN_EPS = 1e-5

LANES = 128
CHUNK = 128
SCAN_NB = 2
PRE_TM = 256
PROJ_TM = 512
FFN_TM = 512
FFN_TF = 2816
ATT_T = 256
MASK_NEG = -1e30
V7X_VMEM_BYTES = 64 * 1024 * 1024
VMEM_LIMIT = V7X_VMEM_BYTES - 8 * 1024 * 1024
INT_MIN = -2 ** 31


def _cparams(sem):
    return pltpu.CompilerParams(dimension_semantics=sem, vmem_limit_bytes=VMEM_LIMIT)


def _split2(x):
    hi = x.astype(BF16)
    lo = (x - hi.astype(F32)).astype(BF16)
    return hi, lo


def _split3(x):
    hi = x.astype(BF16)
    r1 = x - hi.astype(F32)
    mid = r1.astype(BF16)
    lo = (r1 - mid.astype(F32)).astype(BF16)
    return hi, mid, lo


_NN = (((1,), (0,)), ((), ()))
_NT = (((1,), (1,)), ((), ()))


def _mm(a, b, dims=_NN):
    return lax.dot_general(a, b, dims, preferred_element_type=F32)


def _dot3(a, b, dims=_NN):
    ah, al = _split2(a)
    bh, bl = _split2(b)
    return _mm(ah, bh, dims) + (_mm(ah, bl, dims) + _mm(al, bh, dims))


def _dot1(a, b, dims=_NN):
    return _mm(a.astype(BF16), b.astype(BF16), dims)


def _dot_hilo_rhs(a, b_bf16):
    h, l = _split2(a)
    return _mm(h, b_bf16) + _mm(l, b_bf16)


def _sigmoid(x):
    return 1.0 / (1.0 + jnp.exp(-x))


def _layernorm(xr, g, b):
    mu = jnp.mean(xr, axis=-1, keepdims=True)
    xc = xr - mu
    var = jnp.mean(xc * xc, axis=-1, keepdims=True)
    return xc * lax.rsqrt(var + LN_EPS) * g + b


def _adaln_body(c_ref, w_ref, b_ref, o_ref):
    c = c_ref[...]
    cond = c * _sigmoid(c)
    o_ref[0] = _dot3(cond, w_ref[0]) + b_ref[0]


def _adaln(c, ada_w, ada_b):
    depth, d, n = ada_w.shape
    bsz = c.shape[0]
    tn = n // 4
    return pl.pallas_call(
        _adaln_body,
        grid=(depth, n // tn),
        in_specs=[
            pl.BlockSpec((bsz, d), lambda i, j: (0, 0)),
            pl.BlockSpec((1, d, tn), lambda i, j: (i, 0, j)),
            pl.BlockSpec((1, 1, tn), lambda i, j: (i, 0, j)),
        ],
        out_specs=pl.BlockSpec((1, bsz, tn), lambda i, j: (i, 0, j)),
        out_shape=jax.ShapeDtypeStruct((depth, bsz, n), F32),
        compiler_params=_cparams(("parallel", "parallel")),
        name="adaln",
    )(c, ada_w, ada_b.reshape(depth, 1, n))


def _ffn_body(mixer_tail, *refs):
    if mixer_tail:
        (x_ref, mod_ref, wg_ref, wu_ref, wo_ref, lng_ref, lnb_ref,
         zg_ref, wm_ref, lng0_ref, lnb0_ref, o_ref, hin_ref, acc_ref, xm_ref) = refs
    else:
        x_ref, mod_ref, wg_ref, wu_ref, wo_ref, lng_ref, lnb_ref, o_ref, hin_ref, acc_ref = refs
    j = pl.program_id(2)

    @pl.when(j == 0)
    def _():
        x = x_ref[0]
        if mixer_tail:
            y = _mm(zg_ref[0], wm_ref[...])
            x = _layernorm(DEEPNORM_ALPHA * x + (1.0 + mod_ref[0, 2:3, :]) * y, lng0_ref[...], lnb0_ref[...])
            xm_ref[...] = x
        hin_ref[...] = (x * (1.0 + mod_ref[0, 4:5, :]) + mod_ref[0, 3:4, :]).astype(BF16)
        acc_ref[...] = jnp.zeros_like(acc_ref)

    hin = hin_ref[...]
    gate = _mm(hin, wg_ref[...])
    up = _mm(hin, wu_ref[...])
    hid = (gate * _sigmoid(gate) * up).astype(BF16)
    acc_ref[...] += _mm(hid, wo_ref[...])

    @pl.when(j == pl.num_programs(2) - 1)
    def _():
        x = xm_ref[...] if mixer_tail else x_ref[0]
        res = DEEPNORM_ALPHA * x + (1.0 + mod_ref[0, 5:6, :]) * acc_ref[...]
        o_ref[0] = _layernorm(res, lng_ref[...], lnb_ref[...])


def _ffn(x, mod, w_in, w_out, ln_g, ln_b, mixer_tail=None):
    bsz, t, d = x.shape
    f = w_out.shape[0]
    nf = f // FFN_TF
    row = pl.BlockSpec((1, FFN_TM, d), lambda b, i, j: (b, i, 0))
    vec = pl.BlockSpec((1, d), lambda b, i, j: (0, 0))
    ins = [x, mod, w_in, w_in, w_out, ln_g.reshape(1, d), ln_b.reshape(1, d)]
    specs = [row,
             pl.BlockSpec((1, 6, d), lambda b, i, j: (b, 0, 0)),
             pl.BlockSpec((d, FFN_TF), lambda b, i, j: (0, j)),
             pl.BlockSpec((d, FFN_TF), lambda b, i, j: (0, nf + j)),
             pl.BlockSpec((FFN_TF, d), lambda b, i, j: (j, 0)),
             vec, vec]
    scratch = [pltpu.VMEM((FFN_TM, d), BF16), pltpu.VMEM((FFN_TM, d), F32)]
    if mixer_tail is not None:
        zg, w_mix, ln_g0, ln_b0 = mixer_tail
        ins += [zg, w_mix, ln_g0.reshape(1, d), ln_b0.reshape(1, d)]
        specs += [row, pl.BlockSpec((d, d), lambda b, i, j: (0, 0)), vec, vec]
        scratch += [pltpu.VMEM((FFN_TM, d), F32)]
    return pl.pallas_call(
        functools.partial(_ffn_body, mixer_tail is not None),
        grid=(bsz, t // FFN_TM, nf),
        in_specs=specs,
        out_specs=row,
        out_shape=jax.ShapeDtypeStruct((bsz, t, d), F32),
        scratch_shapes=scratch,
        compiler_params=_cparams(("parallel", "parallel", "arbitrary")),
        name="ffn",
    )(*ins)


def _rwkv_pre_body(has_vres, *refs):
    if has_vres:
        (x_ref, xp_ref, mod_ref, mu_ref, wr_ref, wk_ref, wv_ref, w1_ref, w2_ref, a1_ref, a2_ref,
         g1_ref, g2_ref, vec_ref, seg_ref, segt_ref, tri_ref, vf_ref, v1_ref, v2_ref,
         r_o, k_o, a_o, b_o, v_o, g_o, gl_o, mix_ref) = refs
    else:
        (x_ref, xp_ref, mod_ref, mu_ref, wr_ref, wk_ref, wv_ref, w1_ref, w2_ref, a1_ref, a2_ref,
         g1_ref, g2_ref, vec_ref, seg_ref, segt_ref, tri_ref,
         r_o, k_o, a_o, b_o, v_o, g_o, gl_o, mix_ref) = refs
    i = pl.program_id(1)
    sc = 1.0 + mod_ref[0, 1:2, :]
    sh = mod_ref[0, 0:1, :]
    hin = x_ref[0] * sc + sh
    tm, d = hin.shape
    prev_row = xp_ref[0, 7:8, :] * sc + sh
    prev_row = jnp.where(i == 0, 0.0, prev_row)
    rows = lax.broadcasted_iota(jnp.int32, hin.shape, 0)
    hprev = jnp.where(rows == 0, prev_row, pltpu.roll(hin, 1, 0))
    xx = hprev - hin
    for p in range(6):
        mix_ref[p] = (hin + xx * mu_ref[p:p + 1, :]).astype(BF16)
    lora_w = jnp.tanh(_mm(mix_ref[3], w1_ref[...])).astype(BF16)
    lora_a = _mm(mix_ref[4], a1_ref[...]).astype(BF16)
    lora_g = _sigmoid(_mm(mix_ref[5], g1_ref[...])).astype(BF16)
    if has_vres:
        lora_v = _mm(mix_ref[2], v1_ref[...]).astype(BF16)

    cw = 2 * LANES
    for cb in range(d // cw):
        cols = slice(cb * cw, (cb + 1) * cw)
        w0, a0, kkw, kaw = vec_ref[0:1, cols], vec_ref[1:2, cols], vec_ref[2:3, cols], vec_ref[3:4, cols]
        r = _mm(mix_ref[0], wr_ref[:, cols])
        k = _mm(mix_ref[1], wk_ref[:, cols])
        v = _mm(mix_ref[2], wv_ref[:, cols])

        wl = w0 + _mm(lora_w, w2_ref[:, cols])
        nz = -wl
        softplus = jnp.maximum(nz, 0.0) + jnp.log(1.0 + jnp.exp(-jnp.abs(nz)))
        logdec = -jnp.exp(-softplus - 0.5)

        a = _sigmoid(a0 + _mm(lora_a, a2_ref[:, cols]))
        if has_vres:
            vmix = _sigmoid(vec_ref[4:5, cols] + _mm(lora_v, v2_ref[:, cols]))
            v = v + (vf_ref[0, :, cols] - v) * vmix
        g = _mm(lora_g, g2_ref[:, cols])

        kk = k * kkw
        ss = _dot_hilo_rhs(kk * kk, seg_ref[...])
        nrm = jnp.maximum(jnp.sqrt(ss), 1e-12)
        kk = kk * _dot_hilo_rhs(1.0 / nrm, segt_ref[...])
        k = k * (1.0 + (a - 1.0) * kaw)

        cum = _dot_exact_rhs_lhs(tri_ref[...], logdec)
        ginc = jnp.exp(cum)
        ginv = jnp.exp(-cum)
        gprev = jnp.exp(cum - logdec)

        r_o[0, :, cols] = r * ginc
        k_o[0, :, cols] = k * ginv
        a_o[0, :, cols] = -kk * gprev
        b_o[0, :, cols] = kk * a * ginv
        v_o[0, :, cols] = v
        g_o[0, :, cols] = g
        for cc in range(tm // CHUNK):
            first, last = cc * CHUNK, cc * CHUNK + CHUNK - 1
            gl_o[0, cc, 0:1, cols] = jnp.exp(logdec[first:first + 1, :] - cum[first:first + 1, :])
            gl_o[0, cc, 1:2, cols] = ginc[last:last + 1, :]


def _dot_exact_rhs_lhs(m_bf16, x):
    h, mid, l = _split3(x)
    return _mm(m_bf16, h) + (_mm(m_bf16, mid) + _mm(m_bf16, l))


def _rwkv_pre(x, mod, p, v_first):
    bsz, t, d = x.shape
    tm = PRE_TM
    has_vres = v_first is not None
    row = pl.BlockSpec((1, tm, d), lambda b, i: (b, i, 0))

    def full(shape):
        return pl.BlockSpec(shape, lambda b, i: (0,) * len(shape))

    cw = 2 * LANES
    seg = np.zeros((cw, LANES), np.float32)
    seg[np.arange(cw), np.arange(cw) // RWKV_HEAD] = 1.0
    idx = np.arange(tm)
    same = idx[:, None] // CHUNK == idx[None, :] // CHUNK
    col = idx[None, :] % CHUNK
    tri = (same & (col <= idx[:, None] % CHUNK)).astype(np.float32) \
        - (same & (col <= CHUNK // 2 - 1)).astype(np.float32)
    vec_rows = [p['w0'], p['a0'], p['k_k'], p['k_a']] + ([p['v0']] if has_vres else [])
    vec = jnp.stack(vec_rows + [jnp.zeros_like(p['w0'])] * (8 - len(vec_rows)))
    dl, da, dg = p['w1'].shape[1], p['a1'].shape[1], p['g1'].shape[1]
    ins = [x, x, mod, p['mu'], p['wr'], p['wk'], p['wv'], p['w1'], p['w2'], p['a1'], p['a2'],
           p['g1'], p['g2'], vec, jnp.asarray(seg, BF16), jnp.asarray(seg.T, BF16),
           jnp.asarray(tri, BF16)]
    specs = [row,
             pl.BlockSpec((1, 8, d), lambda b, i: (b, jnp.maximum(i * (tm // 8) - 1, 0), 0)),
             pl.BlockSpec((1, 6, d), lambda b, i: (b, 0, 0)),
             full((6, d)), full((d, d)), full((d, d)), full((d, d)),
             full((d, dl)), full((dl, d)), full((d, da)), full((da, d)),
             full((d, dg)), full((dg, d)), full((8, d)),
             full((cw, LANES)), full((LANES, cw)), full((tm, tm))]
    if has_vres:
        dv = p['v1'].shape[1]
        ins += [v_first, p['v1'], p['v2']]
        specs += [row, full((d, dv)), full((dv, d))]
    act = jax.ShapeDtypeStruct((bsz, t, d), F32)
    nch = t // CHUNK
    outs = pl.pallas_call(
        functools.partial(_rwkv_pre_body, has_vres),
        grid=(bsz, t // tm),
        in_specs=specs,
        out_specs=[row] * 6 + [pl.BlockSpec((1, tm // CHUNK, 2, d), lambda b, i: (b, i, 0, 0))],
        out_shape=[act] * 6 + [jax.ShapeDtypeStruct((bsz, nch, 2, d), F32)],
        scratch_shapes=[pltpu.VMEM((6, tm, d), BF16)],
        compiler_params=_cparams(("parallel", "parallel")),
        name="rwkv_pre",
    )(*ins)
    return outs


_BNN = (((2,), (1,)), ((0,), (0,)))
_BNT = (((2,), (2,)), ((0,), (0,)))


def _rwkv_scan_body(r_ref, k_ref, a_ref, b_ref, v_ref, g_ref, gl_ref, vec_ref, o_ref, s_ref):
    c = pl.program_id(1)

    @pl.when(c == 0)
    def _():
        s_ref[...] = jnp.zeros_like(s_ref)

    L = CHUNK
    nb = r_ref.shape[0]
    pairs = [(bi, hp) for bi in range(nb) for hp in range(RWKV_HEADS // 2)]
    NP = len(pairs)
    shp = (NP, L, LANES)
    plane = lax.broadcasted_iota(jnp.int32, (1, L, 2 * L), 2)
    ph0 = plane < L
    tt = lax.broadcasted_iota(jnp.int32, (1, L, 2 * L), 1)
    ss = plane & (L - 1)
    strict = ss < tt
    incl = ss <= tt
    eye = jnp.where(ss == tt, 1.0, 0.0)
    base_bits = 4

    def same_block(bits):
        return (ss >> bits) == (tt >> bits)

    lvl0 = strict & same_block(base_bits)
    merges = [strict & same_block(bits + 1) & ((ss >> bits) != (tt >> bits))
              for bits in range(base_bits, (L - 1).bit_length())]
    dh0 = lax.broadcasted_iota(jnp.int32, (1, L, LANES), 2) < RWKV_HEAD
    ri = lax.broadcasted_iota(jnp.int32, (1, LANES, LANES), 1)
    ci = lax.broadcasted_iota(jnp.int32, (1, LANES, LANES), 2)
    blockdiag = (ri >> 6) == (ci >> 6)
    seg_mean = jnp.where(blockdiag[0], 1.0, 0.0).astype(BF16)

    def grp(ref):
        return jnp.stack([ref[bi, :, hp * LANES:(hp + 1) * LANES] for bi, hp in pairs])

    def bd(x):
        return jnp.concatenate([jnp.where(dh0, x, 0.0), jnp.where(dh0, 0.0, x)], axis=1)

    def bdp(y):
        return jnp.concatenate([jnp.where(ph0, y, 0.0), jnp.where(ph0, 0.0, y)], axis=1)

    def pmm(xp, y):
        return _dot1(xp, bd(y), _BNN)

    def ppm(xp, yp):
        return _dot1(xp, bdp(yp), _BNN)

    R, K, A, Bv, V = grp(r_ref), grp(k_ref), grp(a_ref), grp(b_ref), grp(v_ref)
    gvec = lambda n: jnp.stack([gl_ref[bi, 0, n:n + 1, hp * LANES:(hp + 1) * LANES] for bi, hp in pairs])
    S = s_ref[...] * gvec(0)

    AR = jnp.concatenate([A, R], axis=1)
    BK = jnp.concatenate([bd(Bv), bd(K)], axis=1)
    G = _dot1(AR, BK, _BNT)
    A_ab = jnp.where(strict, G[:, 0:L, 0:2 * L], 0.0)
    A_ak = jnp.where(strict, G[:, 0:L, 2 * L:4 * L], 0.0)
    A_rb = jnp.where(incl, G[:, L:2 * L, 0:2 * L], 0.0)
    A_rk = jnp.where(incl, G[:, L:2 * L, 2 * L:4 * L], 0.0)

    a0 = jnp.where(lvl0, A_ab, 0.0)
    Tm = eye + a0
    P = a0
    for _ in range(base_bits - 1):
        P = ppm(P, P)
        Tm = Tm + ppm(Tm, P)
    for lvl in merges:
        Tm = Tm + ppm(ppm(Tm, jnp.where(lvl, A_ab, 0.0)), Tm)

    PQ = _dot1(AR, S, _BNT)
    W = PQ[:, 0:L] + pmm(A_ak, V)
    U = pmm(Tm, W)
    Y = PQ[:, L:2 * L] + _dot1(jnp.concatenate([A_rb, A_rk], axis=2),
                               jnp.concatenate([bd(U), bd(V)], axis=1), _BNN)
    UV = jnp.concatenate([U, V], axis=1)
    UVt = jnp.stack([UV[hp].T for hp in range(NP)])
    BK2 = jnp.concatenate([Bv, K], axis=1)
    upd = _dot1(UVt, BK2, _BNN)
    s_ref[...] = (S + jnp.where(blockdiag, upd, 0.0)) * gvec(1)

    vrow = lambda n: jnp.stack([vec_ref[n:n + 1, hp * LANES:(hp + 1) * LANES] for _, hp in pairs])
    inv_n = 1.0 / RWKV_HEAD
    flat = lambda x: x.reshape(NP * L, LANES)
    mean = _dot_hilo_rhs(flat(Y), seg_mean).reshape(shp) * inv_n
    yc = Y - mean
    var = _dot_hilo_rhs(flat(yc * yc), seg_mean).reshape(shp) * inv_n
    yn = yc * lax.rsqrt(var + GN_EPS) * vrow(1) + vrow(2)
    bonus = _dot_hilo_rhs(flat(R * K * vrow(0)), seg_mean).reshape(shp)
    out = ((yn + bonus * V) * grp(g_ref)).astype(BF16)
    for n, (bi, hp) in enumerate(pairs):
        o_ref[bi, :, hp * LANES:(hp + 1) * LANES] = out[n]


def _rwkv_scan(r, k, a, b, v, g, gl, r_k, lnx_g, lnx_b):
    bsz, t, d = r.shape
    nb = SCAN_NB
    row = pl.BlockSpec((nb, CHUNK, d), lambda bb, c: (bb, c, 0))
    vec = jnp.stack([r_k.reshape(d), lnx_g, lnx_b] + [jnp.zeros((d,), F32)] * 5)
    return pl.pallas_call(
        _rwkv_scan_body,
        grid=(bsz // nb, t // CHUNK),
        in_specs=[row] * 6 + [
            pl.BlockSpec((nb, 1, 2, d), lambda bb, c: (bb, c, 0, 0)),
            pl.BlockSpec((8, d), lambda bb, c: (0, 0)),
        ],
        out_specs=row,
        out_shape=jax.ShapeDtypeStruct((bsz, t, d), BF16),
        scratch_shapes=[pltpu.VMEM((nb * RWKV_HEADS // 2, LANES, LANES), F32)],
        compiler_params=_cparams(("parallel", "arbitrary")),
        name="rwkv_scan",
    )(r, k, a, b, v, g, gl, vec)


IDX_COLS = 768


def _dsa_proj_body(x_ref, mod_ref, wq_ref, wc_ref, wi_ref, kvn_ref,
                   q_o, ckv_o, qi_o, ki_o, wi_o):
    hin = (x_ref[0] * (1.0 + mod_ref[0, 1:2, :]) + mod_ref[0, 0:1, :]).astype(BF16)
    q_o[0] = _mm(hin, wq_ref[...]).astype(BF16)
    ckv = _mm(hin, wc_ref[...])
    ms = jnp.mean(ckv * ckv, axis=-1, keepdims=True)
    ckv_o[0] = (ckv * lax.rsqrt(ms + 1e-6) * kvn_ref[...]).astype(BF16)
    idx = _mm(hin, wi_ref[...])
    nq = IDX_HEADS * IDX_DIM
    qi_o[0] = idx[:, 0:nq].astype(BF16)
    ki_o[0] = idx[:, nq:nq + LANES].astype(BF16)
    wi_o[0] = idx[:, nq + LANES:nq + 2 * LANES] * (IDX_HEADS ** -0.5 * IDX_DIM ** -0.5)


def _dsa_proj(x, mod, w_in, kv_norm):
    bsz, t, d = x.shape
    c1 = ATT_HEADS * ATT_HEAD_DIM
    c2 = c1 + KV_LATENT
    c3 = c2 + IDX_HEADS * IDX_DIM
    c4 = c3 + IDX_DIM
    wq = w_in[:, :c1].astype(BF16)
    wc = w_in[:, c1:c2].astype(BF16)
    pad = IDX_COLS - (c3 - c2) - 2 * IDX_DIM - IDX_HEADS
    widx = jnp.concatenate([w_in[:, c2:c3], w_in[:, c3:c4], w_in[:, c3:c4], w_in[:, c4:],
                            jnp.zeros((d, pad), F32)], axis=1).astype(BF16)
    tm = PROJ_TM

    def full(shape):
        return pl.BlockSpec(shape, lambda b, i: (0,) * len(shape))

    def row(n):
        return pl.BlockSpec((1, tm, n), lambda b, i: (b, i, 0))

    def act(n, dtype):
        return jax.ShapeDtypeStruct((bsz, t, n), dtype)

    nq = IDX_HEADS * IDX_DIM
    return pl.pallas_call(
        _dsa_proj_body,
        grid=(bsz, t // tm),
        in_specs=[row(d), pl.BlockSpec((1, 6, d), lambda b, i: (b, 0, 0)),
                  full((d, c1)), full((d, KV_LATENT)), full((d, IDX_COLS)),
                  full((1, KV_LATENT))],
        out_specs=[row(c1), row(KV_LATENT), row(nq), row(LANES), row(LANES)],
        out_shape=[act(c1, BF16), act(KV_LATENT, BF16), act(nq, BF16), act(LANES, BF16), act(LANES, F32)],
        compiler_params=_cparams(("parallel", "parallel")),
        name="dsa_proj",
    )(x, mod, wq, wc, widx, kv_norm.reshape(1, KV_LATENT))


def _dsa_index_body(k_sel, qi_ref, wi_ref, ki_ref, o_ref, key_ref):
    i = pl.program_id(1)
    tq = qi_ref.shape[1]
    t = ki_ref.shape[1]
    nchunk = t // tq
    wt = wi_ref[0].T
    lane = lax.broadcasted_iota(jnp.int32, (tq, LANES), 1)
    first = lane < IDX_DIM
    qheads = []
    for hp in range(IDX_HEADS // 2):
        qp = qi_ref[0, :, hp * LANES:(hp + 1) * LANES].astype(F32)
        qheads.append(jnp.where(first, qp, 0.0).astype(BF16))
        qheads.append(jnp.where(first, 0.0, qp).astype(BF16))
    q_all = jnp.concatenate(qheads, axis=0)
    krow = lax.broadcasted_iota(jnp.int32, (tq, tq), 0)
    qlane = lax.broadcasted_iota(jnp.int32, (tq, tq), 1)

    kf = float(k_sel)
    nbits = int(t - 1).bit_length()

    def colsum(ind):
        return jnp.sum(jnp.sum(ind.reshape(tq // 32, 4, 8, tq), axis=0), axis=0)

    def tile(nc):
        chunks = [slice(c * tq, (c + 1) * tq) for c in range(nc)]
        diag = krow <= qlane

        for c, rows in enumerate(chunks):
            kk2 = ki_ref[0, rows, :]
            s_all = _mm(kk2, q_all, _NT)
            score = jnp.zeros((tq, tq), F32)
            for h in range(IDX_HEADS):
                score = score + wt[h:h + 1, :] * jnp.maximum(s_all[:, h * tq:(h + 1) * tq], 0.0)
            score = jnp.where(score == 0.0, 0.0, score)
            bits = pltpu.bitcast(score, jnp.int32)
            skey = bits ^ ((bits >> 31) & 0x7FFFFFFF)
            key_ref[rows, :] = jnp.where(diag, skey, INT_MIN) if c == nc - 1 else skey

        def count(fn):
            acc = jnp.zeros((8, tq), F32)
            for c, rows in enumerate(chunks):
                acc = acc + colsum(fn(key_ref[rows, :], c * tq + krow))
            return jnp.sum(acc, axis=0, keepdims=True)

        def count_ge(cand):
            return count(lambda keys, kpos: jnp.where(keys >= cand, 1.0, 0.0))

        thr0 = jnp.where(count_ge(jnp.zeros((1, tq), jnp.int32)) >= kf, 0, INT_MIN).astype(jnp.int32)

        def thr_step(n, thr):
            cand = thr | jnp.left_shift(jnp.int32(1), 30 - n)
            return jnp.where(count_ge(cand) >= kf, cand, thr)

        thr = lax.fori_loop(0, 31, thr_step, thr0)
        n_gt = count(lambda keys, kpos: jnp.where(keys > thr, 1.0, 0.0))
        n_eq = count(lambda keys, kpos: jnp.where(keys == thr, 1.0, 0.0))
        need = kf - n_gt

        def tie_cut():
            def cut_step(n, cut):
                cand = cut | jnp.left_shift(jnp.int32(1), nbits - 1 - n)
                cnt = count(lambda keys, kpos: jnp.where(keys == thr, jnp.where(kpos < cand, 1.0, 0.0), 0.0))
                return jnp.where(cnt < need, cand, cut)
            return lax.fori_loop(0, nbits, cut_step, jnp.zeros((1, tq), jnp.int32))

        cut = lax.cond(jnp.max(n_eq - need) > 0.0, tie_cut, lambda: jnp.full((1, tq), t, jnp.int32))

        for c, rows in enumerate(chunks):
            keys = key_ref[rows, :]
            tie = jnp.where(keys == thr, jnp.where(c * tq + krow <= cut, 0.0, MASK_NEG), MASK_NEG)
            bias = jnp.where(keys > thr, 0.0, tie)
            if c == nc - 1:
                bias = jnp.where(diag, bias, MASK_NEG)
            o_ref[0, rows, :] = bias.astype(BF16)
        for c in range(nc, nchunk):
            o_ref[0, c * tq:(c + 1) * tq, :] = jnp.full((tq, tq), MASK_NEG, BF16)

    for nc in range(1, nchunk + 1):
        pl.when(i == nc - 1)(functools.partial(tile, nc))


def _dsa_index(qi, ki, wi, k_sel):
    bsz, t, nq = qi.shape
    tq = ATT_T
    return pl.pallas_call(
        functools.partial(_dsa_index_body, k_sel),
        grid=(bsz, t // tq),
        in_specs=[pl.BlockSpec((1, tq, nq), lambda b, i: (b, i, 0)),
                  pl.BlockSpec((1, tq, LANES), lambda b, i: (b, i, 0)),
                  pl.BlockSpec((1, t, LANES), lambda b, i: (b, 0, 0))],
        out_specs=pl.BlockSpec((1, t, tq), lambda b, i: (b, 0, i)),
        out_shape=jax.ShapeDtypeStruct((bsz, t, t), BF16),
        scratch_shapes=[pltpu.VMEM((t, tq), jnp.int32)],
        compiler_params=_cparams(("parallel", "parallel")),
        name="dsa_index",
    )(qi, wi, ki)


def _t5_bucket_np(n):
    n = np.maximum(n, 0)
    max_exact = REL_BUCKETS // 2
    nf = np.maximum(n, 1).astype(np.float32)
    large = max_exact + (np.log(nf / np.float32(max_exact)) / np.float32(math.log(REL_MAX_DIST / max_exact))
                         * np.float32(REL_BUCKETS - max_exact)).astype(np.int32)
    large = np.minimum(large, REL_BUCKETS - 1)
    return np.where(n < max_exact, n, large).astype(np.int32)


def _band_body(bkt_ref, rb_ref, o_ref):
    h = pl.program_id(1)
    bkt = bkt_ref[0]
    acc = jnp.zeros(bkt.shape, F32)
    for b in range(REL_BUCKETS):
        acc = jnp.where(bkt == b, rb_ref[b, h], acc)
    o_ref[0] = acc * LOG2E


def _band_bias(rel_bias):
    tt = ATT_T
    kc = np.arange(tt)[:, None]
    qr = np.arange(tt)[None, :]
    planes = [_t5_bucket_np(d * tt + qr - kc) for d in range(3)]
    assert (planes[2] == REL_BUCKETS - 1).all() and tt + 1 >= 113
    bkt = jnp.asarray(np.stack(planes))
    return pl.pallas_call(
        _band_body,
        grid=(3, ATT_HEADS),
        in_specs=[pl.BlockSpec((1, tt, tt), lambda d, h: (d, 0, 0)),
                  pl.BlockSpec(memory_space=pltpu.SMEM)],
        out_specs=pl.BlockSpec((1, tt, tt), lambda d, h: (d, 0, h)),
        out_shape=jax.ShapeDtypeStruct((3, tt, ATT_HEADS * tt), F32),
        compiler_params=_cparams(("parallel", "parallel")),
        name="band_bias",
    )(bkt, rel_bias)


LOG2E = 1.4426950408889634
ACC_ROWS = KV_LATENT + 16


def _dsa_attn_body(qt_ref, kt_ref, q_ref, ckv_ref, mask_ref, wuk_ref, wuv_ref, near_ref, far_ref,
                   x_ref, mod_ref, wout_ref, lng_ref, lnb_ref, o_ref,
                   ql_ref, m_ref, sm_ref, ot_ref, *acc_refs):
    i = qt_ref[pl.program_id(1)]
    kp = kt_ref[pl.program_id(1)]
    gap = i - 2 * kp
    nh = ATT_HEADS
    tq = q_ref.shape[1]
    tk = tq
    qscale = ATT_HEAD_DIM ** -0.5 * LOG2E

    @pl.when(kp == 0)
    def _():
        for hp in range(nh // 2):
            qp = q_ref[0, :, hp * LANES:(hp + 1) * LANES].astype(BF16)
            qlat = _mm(qp, wuk_ref[hp]) * qscale
            ql_ref[2 * hp * tq:(2 * hp + 1) * tq, :] = qlat[:, 0:KV_LATENT].astype(BF16)
            ql_ref[(2 * hp + 1) * tq:(2 * hp + 2) * tq, :] = qlat[:, KV_LATENT:2 * KV_LATENT].astype(BF16)
        m_ref[...] = jnp.full(m_ref.shape, MASK_NEG, F32)
        for acc_ref in acc_refs:
            acc_ref[...] = jnp.zeros_like(acc_ref)

    def step(*kinds):
        nk = len(kinds) * tk
        ckv = ckv_ref[0, 0:nk, :]
        ckv_aug = jnp.concatenate([ckv.astype(F32).T, jnp.ones((ACC_ROWS - KV_LATENT, nk), F32)],
                                  axis=0).astype(BF16)
        maskb = mask_ref[0, 0:nk, :].astype(F32)
        m_prev = m_ref[...]
        m_news = []
        for h in range(nh):
            hs = slice(h * tq, (h + 1) * tq)
            s = _mm(ckv, ql_ref[hs, :], _NT)
            m_new = m_prev[:, hs]
            for n, plane in enumerate(kinds):
                rows = slice(n * tk, (n + 1) * tk)
                sm = s[rows] + maskb[rows]
                if plane is None:
                    m_new = jnp.maximum(m_new, jnp.max(sm, axis=0, keepdims=True) + far_ref[:, hs])
                else:
                    sm = sm + near_ref[plane, :, hs]
                    m_new = jnp.maximum(m_new, jnp.max(sm, axis=0, keepdims=True))
                sm_ref[rows, hs] = sm
            m_news.append(m_new)
        for h in range(nh):
            hs = slice(h * tq, (h + 1) * tq)
            m_new = m_news[h]
            p = jnp.concatenate(
                [jnp.exp2(sm_ref[n * tk:(n + 1) * tk, hs]
                          - (m_new - far_ref[:, hs] if plane is None else m_new)).astype(BF16)
                 for n, plane in enumerate(kinds)], axis=0)
            alpha = jnp.exp2(m_prev[:, hs] - m_new)
            acc_refs[h][...] = alpha * acc_refs[h][...] + _mm(ckv_aug, p)
        m_ref[...] = jnp.concatenate(m_news, axis=1)

    @pl.when(gap >= 3)
    def _():
        step(None, None)

    @pl.when(gap == 2)
    def _():
        step(None, 1)

    @pl.when(gap == 1)
    def _():
        step(1, 0)

    @pl.when(gap == 0)
    def _():
        step(0)

    @pl.when(gap <= 1)
    def _():
        def norm(h):
            a = acc_refs[h][...]
            return a[0:KV_LATENT] * (1.0 / a[KV_LATENT:KV_LATENT + 1])
        for hp in range(nh // 2):
            olat = jnp.concatenate([norm(2 * hp), norm(2 * hp + 1)], axis=0).astype(BF16)
            ot_ref[hp * LANES:(hp + 1) * LANES, :] = _mm(wuv_ref[hp], olat)
        y = _mm(ot_ref[...].T.astype(BF16), wout_ref[...])
        res = DEEPNORM_ALPHA * x_ref[0] + (1.0 + mod_ref[0, 2:3, :]) * y
        o_ref[0] = _layernorm(res, lng_ref[...], lnb_ref[...])


def _dsa_attn(q, ckv, maskt, w_uk, w_uv, band, x, mod, w_out, ln_g, ln_b):
    bsz, t, d = q.shape
    tt = ATT_T
    nt = t // tt
    nh = ATT_HEADS
    zk = jnp.zeros((nh // 2, ATT_HEAD_DIM, KV_LATENT), F32)
    wuk2 = jnp.concatenate([jnp.concatenate([w_uk[0::2], zk], axis=2),
                            jnp.concatenate([zk, w_uk[1::2]], axis=2)], axis=1).astype(BF16)
    wuv_t = jnp.swapaxes(w_uv, 1, 2)
    zv = jnp.zeros((nh // 2, ATT_HEAD_DIM, KV_LATENT), F32)
    wuv2 = jnp.concatenate([jnp.concatenate([wuv_t[0::2], zv], axis=2),
                            jnp.concatenate([zv, wuv_t[1::2]], axis=2)], axis=1).astype(BF16)
    assert nt % 2 == 0
    steps = [(i, kp) for i in range(nt) for kp in range(i // 2 + 1)]
    q_tab = jnp.asarray([p[0] for p in steps], jnp.int32)
    k_tab = jnp.asarray([p[1] for p in steps], jnp.int32)
    grid_spec = pltpu.PrefetchScalarGridSpec(
        num_scalar_prefetch=2,
        grid=(bsz, len(steps)),
        in_specs=[pl.BlockSpec((1, tt, d), lambda b, s, qt, kt: (b, qt[s], 0)),
                  pl.BlockSpec((1, 2 * tt, KV_LATENT), lambda b, s, qt, kt: (b, kt[s], 0)),
                  pl.BlockSpec((1, 2 * tt, tt), lambda b, s, qt, kt: (b, kt[s], qt[s])),
                  pl.BlockSpec((nh // 2, LANES, 2 * KV_LATENT), lambda b, s, qt, kt: (0, 0, 0)),
                  pl.BlockSpec((nh // 2, LANES, 2 * KV_LATENT), lambda b, s, qt, kt: (0, 0, 0)),
                  pl.BlockSpec((2, tt, nh * tt), lambda b, s, qt, kt: (0, 0, 0)),
                  pl.BlockSpec((1, nh * tt), lambda b, s, qt, kt: (0, 0)),
                  pl.BlockSpec((1, tt, d), lambda b, s, qt, kt: (b, qt[s], 0)),
                  pl.BlockSpec((1, 6, d), lambda b, s, qt, kt: (b, 0, 0)),
                  pl.BlockSpec((d, d), lambda b, s, qt, kt: (0, 0)),
                  pl.BlockSpec((1, d), lambda b, s, qt, kt: (0, 0)),
                  pl.BlockSpec((1, d), lambda b, s, qt, kt: (0, 0))],
        out_specs=pl.BlockSpec((1, tt, d), lambda b, s, qt, kt: (b, qt[s], 0)),
        scratch_shapes=[pltpu.VMEM((nh * tt, KV_LATENT), BF16),
                        pltpu.VMEM((1, nh * tt), F32),
                        pltpu.VMEM((2 * tt, nh * tt), F32),
                        pltpu.VMEM((d, tt), F32)]
        + [pltpu.VMEM((ACC_ROWS, tt), F32)] * nh)
    return pl.pallas_call(
        _dsa_attn_body,
        grid_spec=grid_spec,
        out_shape=jax.ShapeDtypeStruct((bsz, t, d), F32),
        compiler_params=_cparams(("parallel", "arbitrary")),
        name="dsa_attn",
    )(q_tab, k_tab, q, ckv, maskt, wuk2, wuv2, band[:2], band[2, 0:1, :],
      x, mod, w_out, ln_g.reshape(1, d), ln_b.reshape(1, d))


def kernel(x, c, ada_w, ada_b, ln_g, ln_b, ffn_w_in, ffn_w_out, rwkv_mu, rwkv_w_rkv, rwkv_w0, rwkv_w1, rwkv_w2, rwkv_a0, rwkv_a1, rwkv_a2, rwkv_v0, rwkv_v1, rwkv_v2, rwkv_g1, rwkv_g2, rwkv_k_k, rwkv_k_a, rwkv_r_k, rwkv_lnx_g, rwkv_lnx_b, rwkv_w_out, dsa_w_in, dsa_kv_norm, dsa_w_uk, dsa_w_uv, dsa_w_out, rel_bias):
    bsz, t, d = x.shape
    assert d == D_MODEL and t % (2 * ATT_T) == 0 and t % FFN_TM == 0 and t % PRE_TM == 0 and bsz % SCAN_NB == 0
    mod_all = _adaln(c, ada_w, ada_b).reshape(DEPTH, bsz, 6, d)
    band = _band_bias(rel_bias)
    k_sel = min(TOPK_MAX, t // TOPK_DIV)
    bf = lambda w: w.astype(BF16)
    v_first = None
    for i in range(DEPTH):
        mod = mod_all[i]
        j = i // 2
        if i % 2 == 0:
            p = dict(mu=rwkv_mu[j], wr=bf(rwkv_w_rkv[j, 0]), wk=bf(rwkv_w_rkv[j, 1]), wv=bf(rwkv_w_rkv[j, 2]),
                     w0=rwkv_w0[j], w1=bf(rwkv_w1[j]), w2=bf(rwkv_w2[j]),
                     a0=rwkv_a0[j], a1=bf(rwkv_a1[j]), a2=bf(rwkv_a2[j]),
                     g1=bf(rwkv_g1[j]), g2=bf(rwkv_g2[j]), k_k=rwkv_k_k[j], k_a=rwkv_k_a[j])
            if j > 0:
                p.update(v0=rwkv_v0[j - 1], v1=bf(rwkv_v1[j - 1]), v2=bf(rwkv_v2[j - 1]))
            r_s, k_s, a_s, b_s, v, g, gl = _rwkv_pre(x, mod, p, v_first if j > 0 else None)
            if j == 0:
                v_first = v
            zg = _rwkv_scan(r_s, k_s, a_s, b_s, v, g, gl, rwkv_r_k[j], rwkv_lnx_g[j], rwkv_lnx_b[j])
            tail = (zg, bf(rwkv_w_out[j]), ln_g[i, 0], ln_b[i, 0])
        else:
            tail = None
            q, ckv, qi, ki, wi = _dsa_proj(x, mod, dsa_w_in[j], dsa_kv_norm[j])
            maskt = _dsa_index(qi, ki, wi, k_sel)
            x = _dsa_attn(q, ckv, maskt, dsa_w_uk[j], dsa_w_uv[j], band,
                          x, mod, bf(dsa_w_out[j]), ln_g[i, 0], ln_b[i, 0])
        x = _ffn(x, mod, bf(ffn_w_in[i]), bf(ffn_w_out[i]), ln_g[i, 1], ln_b[i, 1], tail)
    return x
```
